```python
import jax, jax.numpy as jnp
from jax import lax
import numpy as np


D_MODEL = 1024
BATCH = 4
SEQ = 4096
DEPTH = 1
DEC_BATCH = 128
DEC_SEQ = 1
PAST_LEN = 8192
PAGE_SIZE = 128

GDN_HEADS = 4
GDN_DK = 128
GDN_DV = 128
FOX_HEADS = 4
FOX_HEAD_DIM = 128
CONV_W = 4
CHUNK = 64
Q_BLOCK = 128
NORM_EPS = 1e-6
FORGET_BIAS_INIT = 3.0
GDN_QK_WIDTH = GDN_HEADS * GDN_DK
GDN_WIDTH = GDN_HEADS * GDN_DV
FOX_WIDTH = FOX_HEADS * FOX_HEAD_DIM
MIX_WIDTH = GDN_WIDTH + FOX_WIDTH
GDN_CONV_DIM = 2 * GDN_QK_WIDTH + GDN_WIDTH
IN_SPLIT_WIDTHS = (GDN_QK_WIDTH, GDN_QK_WIDTH, GDN_WIDTH, GDN_WIDTH, GDN_HEADS, GDN_HEADS,
                   FOX_WIDTH, FOX_WIDTH, FOX_WIDTH, FOX_WIDTH, FOX_HEADS)
D_IN = 2 * GDN_QK_WIDTH + 2 * GDN_WIDTH + 2 * GDN_HEADS + 4 * FOX_WIDTH + FOX_HEADS

kernel_name = 'hymba_gdn_fox_adaln_step'


def rms_norm(x, g):
    xf = x.astype(jnp.float32)
    y = xf * lax.rsqrt(jnp.mean(xf * xf, axis=-1, keepdims=True) + NORM_EPS)
    return (y * g.astype(jnp.float32)).astype(x.dtype)


def l2_norm(x):
    xf = x.astype(jnp.float32)
    return (xf * lax.rsqrt(jnp.sum(xf * xf, axis=-1, keepdims=True) + NORM_EPS)).astype(x.dtype)


def split_columns(p):
    offs = np.cumsum(np.array(IN_SPLIT_WIDTHS))[:-1].tolist()
    return jnp.split(p, offs, axis=-1)


def short_conv(u, buf, w):
    t = u.shape[1]
    full = jnp.concatenate([buf.astype(u.dtype), u], axis=1)
    out = full[:, 0:t] * w[0]
    for i in range(1, CONV_W):
        out = out + full[:, i:i + t] * w[i]
    return jax.nn.silu(out), full[:, t:]


def gated_delta_chunked(q, k, v, g, beta, s0):
    out_dtype = v.dtype
    bsz, t, h, dk = q.shape
    dv = v.shape[-1]
    pad = (-t) % CHUNK
    n = (t + pad) // CHUNK

    def to_chunks(a):
        a = jnp.pad(a.astype(jnp.float32), [(0, 0), (0, pad)] + [(0, 0)] * (a.ndim - 2))
        a = a.reshape((bsz, n, CHUNK) + a.shape[2:])
        return jnp.moveaxis(a, 3, 1)

    q = to_chunks(q) * (dk ** -0.5)
    k = to_chunks(k)
    v = to_chunks(v)
    g = jnp.cumsum(to_chunks(g), axis=-1)
    beta = to_chunks(beta)
    incl = jnp.tril(jnp.ones((CHUNK, CHUNK), dtype=bool))
    strict = jnp.tril(jnp.ones((CHUNK, CHUNK), dtype=bool), -1)
    gdiff = g[..., :, None] - g[..., None, :]
    decay = jnp.where(incl, jnp.exp(jnp.where(incl, gdiff, 0.0)), 0.0)
    k_beta = k * beta[..., None]
    m = jnp.where(strict, jnp.einsum('bhncd,bhnsd->bhncs', k_beta, k) * decay, 0.0)
    eye = jnp.eye(CHUNK, dtype=jnp.float32)
    t_inv = lax.linalg.triangular_solve(m + eye, jnp.broadcast_to(eye, m.shape), left_side=True, lower=True)
    u = jnp.einsum('bhncs,bhnse->bhnce', t_inv, v * beta[..., None])
    w = jnp.einsum('bhncs,bhnsd->bhncd', t_inv, k_beta * jnp.exp(g)[..., None])
    a_intra = jnp.einsum('bhncd,bhnsd->bhncs', q, k) * decay
    xs = tuple(jnp.moveaxis(a, 2, 0) for a in (q, k, u, w, g, a_intra))

    def step(s, xs_i):
        q_i, k_i, u_i, w_i, g_i, a_i = xs_i
        v_new = u_i - jnp.einsum('bhcd,bhde->bhce', w_i, s)
        o_i = (jnp.einsum('bhcd,bhde->bhce', q_i * jnp.exp(g_i)[..., None], s)
               + jnp.einsum('bhcs,bhse->bhce', a_i, v_new))
        g_last = g_i[..., -1]
        s = (s * jnp.exp(g_last)[..., None, None]
             + jnp.einsum('bhcd,bhce->bhde', k_i * jnp.exp(g_last[..., None] - g_i)[..., None], v_new))
        return s, o_i

    s_final, o = lax.scan(step, s0.astype(jnp.float32), xs)
    o = jnp.transpose(o, (1, 0, 3, 2, 4)).reshape(bsz, n * CHUNK, h, dv)[:, :t]
    return o.astype(out_dtype), s_final


def forgetting_attention_prompt(q, k, v, logf):
    bsz, t, h, d = q.shape
    cum = jnp.cumsum(logf, axis=1)
    cum_k = jnp.transpose(cum, (0, 2, 1))[:, :, None, :]
    kidx = jnp.arange(t)

    def block(i):
        start = i * Q_BLOCK
        qb = lax.dynamic_slice_in_dim(q, start, Q_BLOCK, axis=1)
        cb = lax.dynamic_slice_in_dim(cum, start, Q_BLOCK, axis=1)
        s = (jnp.einsum('bqhd,bkhd->bhqk', qb, k).astype(jnp.float32) * (d ** -0.5)
             + jnp.transpose(cb, (0, 2, 1))[..., None] - cum_k)
        qidx = start + jnp.arange(Q_BLOCK)
        s = jnp.where(kidx[None, :] <= qidx[:, None], s, -jnp.inf)
        p = jax.nn.softmax(s, axis=-1)
        return jnp.einsum('bhqk,bkhd->bqhd', p.astype(v.dtype), v)

    out = lax.map(block, jnp.arange(t // Q_BLOCK))
    return jnp.transpose(out, (1, 0, 2, 3, 4)).reshape(bsz, t, h, d)


def forgetting_attention_sample(q, k, v, logf, cache_k, cache_v, cache_logf, page_table):
    db, t, h, d = q.shape
    kp = cache_k[page_table].reshape(db, -1, h, d).astype(k.dtype)
    vp = cache_v[page_table].reshape(db, -1, h, d).astype(v.dtype)
    lfp = cache_logf[page_table].reshape(db, -1, h).astype(jnp.float32)
    past = kp.shape[1]
    k_all = jnp.concatenate([kp, k], axis=1)
    v_all = jnp.concatenate([vp, v], axis=1)
    cum = jnp.cumsum(jnp.concatenate([lfp, logf], axis=1), axis=1)
    cq = jnp.transpose(cum[:, past:], (0, 2, 1))[..., None]
    ck = jnp.transpose(cum, (0, 2, 1))[:, :, None, :]
    s = jnp.einsum('bqhd,bkhd->bhqk', q, k_all).astype(jnp.float32) * (d ** -0.5) + cq - ck
    kidx = jnp.arange(past + t)
    qidx = past + jnp.arange(t)
    s = jnp.where(kidx[None, :] <= qidx[:, None], s, -jnp.inf)
    p = jax.nn.softmax(s, axis=-1)
    return jnp.einsum('bhqk,bkhd->bqhd', p.astype(v_all.dtype), v_all)


def hybrid_layer(x, c, conv_buf, ssm0, past, norm_g, w_ada, b_ada, w_in, b_f, conv_w, a_log,
                 dt_bias, gdn_norm_g, q_norm_g, k_norm_g, w_out):
    bsz, t, _ = x.shape
    mod = jax.nn.silu(c) @ w_ada + b_ada
    shift, scale, gate = jnp.split(mod[:, None, :], 3, axis=-1)
    hdn = rms_norm(x, norm_g) * (1.0 + scale) + shift
    p = hdn @ w_in
    gq, gk, gv, gz, gb, ga, fq, fk, fv, fz, ff = split_columns(p)
    qkv, new_conv = short_conv(jnp.concatenate([gq, gk, gv], axis=-1), conv_buf, conv_w)
    aq, ak, av = jnp.split(qkv, [GDN_QK_WIDTH, 2 * GDN_QK_WIDTH], axis=-1)
    aq = l2_norm(aq.reshape(bsz, t, GDN_HEADS, GDN_DK))
    ak = l2_norm(ak.reshape(bsz, t, GDN_HEADS, GDN_DK))
    av = av.reshape(bsz, t, GDN_HEADS, GDN_DV)
    beta = jax.nn.sigmoid(gb.astype(jnp.float32))
    g = -jnp.exp(a_log.astype(jnp.float32)) * jax.nn.softplus(ga.astype(jnp.float32) + dt_bias.astype(jnp.float32))
    o_a, new_ssm = gated_delta_chunked(aq, ak, av, g, beta, ssm0)
    o_a = rms_norm(o_a, gdn_norm_g) * jax.nn.silu(gz.reshape(bsz, t, GDN_HEADS, GDN_DV))
    bq = rms_norm(fq.reshape(bsz, t, FOX_HEADS, FOX_HEAD_DIM), q_norm_g)
    bk = rms_norm(fk.reshape(bsz, t, FOX_HEADS, FOX_HEAD_DIM), k_norm_g)
    bv = fv.reshape(bsz, t, FOX_HEADS, FOX_HEAD_DIM)
    logf = jax.nn.log_sigmoid((ff + b_f).astype(jnp.float32))
    if past is None:
        o_b = forgetting_attention_prompt(bq, bk, bv, logf)
    else:
        o_b = forgetting_attention_sample(bq, bk, bv, logf, past[0], past[1], past[2], past[3])
    o_b = o_b * jax.nn.silu(fz.reshape(bsz, t, FOX_HEADS, FOX_HEAD_DIM))
    mix = jnp.concatenate([o_a.reshape(bsz, t, GDN_WIDTH), o_b.reshape(bsz, t, FOX_WIDTH)], axis=-1)
    y = x + gate * (mix @ w_out)
    return y, bk, bv, logf, new_ssm, new_conv


def setup_inputs(seed: int = 0) -> dict:
    key = jax.random.key(seed)
    ks = jax.random.split(key, 24)
    n_pages = PAST_LEN // PAGE_SIZE
    n_used = DEC_BATCH * n_pages
    n_pool = n_used + n_used // 4
    f32 = jnp.float32
    x_prompt = jax.random.normal(ks[0], (BATCH, SEQ, D_MODEL), f32)
    x_sample = jax.random.normal(ks[1], (DEC_BATCH, DEC_SEQ, D_MODEL), f32)
    cache_k = jax.random.normal(ks[2], (DEPTH, n_pool, PAGE_SIZE, FOX_HEADS, FOX_HEAD_DIM), f32)
    cache_v = jax.random.normal(ks[3], (DEPTH, n_pool, PAGE_SIZE, FOX_HEADS, FOX_HEAD_DIM), f32)
    cache_logf = jax.nn.log_sigmoid(FORGET_BIAS_INIT + jax.random.normal(ks[4], (DEPTH, n_pool, PAGE_SIZE, FOX_HEADS), f32))
    state_ssm = 0.1 * jax.random.normal(ks[5], (DEPTH, DEC_BATCH, GDN_HEADS, GDN_DK, GDN_DV), f32)
    state_conv = jax.random.normal(ks[6], (DEPTH, DEC_BATCH, CONV_W - 1, GDN_CONV_DIM), f32)
    page_table = jax.random.permutation(ks[7], n_pool)[:n_used].reshape(DEC_BATCH, n_pages).astype(jnp.int32)
    c_prompt = jax.random.normal(ks[8], (BATCH, D_MODEL), f32)
    c_sample = jax.random.normal(ks[9], (DEC_BATCH, D_MODEL), f32)
    norm_g = 1.0 + 0.01 * jax.random.normal(ks[10], (DEPTH, D_MODEL), f32)
    w_ada = jax.random.normal(ks[11], (DEPTH, D_MODEL, 3 * D_MODEL), f32) * (D_MODEL ** -0.5)
    b_ada = 0.02 * jax.random.normal(ks[12], (DEPTH, 3 * D_MODEL), f32)
    w_in = jax.random.normal(ks[13], (DEPTH, D_MODEL, D_IN), f32) * (D_MODEL ** -0.5)
    b_f = FORGET_BIAS_INIT + 0.1 * jax.random.normal(ks[14], (DEPTH, FOX_HEADS), f32)
    conv_w = jax.random.normal(ks[15], (DEPTH, CONV_W, GDN_CONV_DIM), f32) * (CONV_W ** -0.5)
    a_log = jnp.log(jax.random.uniform(ks[16], (DEPTH, GDN_HEADS), f32, 1.0, 16.0))
    dt = jnp.exp(jax.random.uniform(ks[17], (DEPTH, GDN_HEADS), f32, math_log(1e-3), math_log(1e-1)))
    dt_bias = dt + jnp.log(-jnp.expm1(-dt))
    gdn_norm_g = 1.0 + 0.01 * jax.random.normal(ks[18], (DEPTH, GDN_DV), f32)
    q_norm_g = 1.0 + 0.01 * jax.random.normal(ks[19], (DEPTH, FOX_HEAD_DIM), f32)
    k_norm_g = 1.0 + 0.01 * jax.random.normal(ks[20], (DEPTH, FOX_HEAD_DIM), f32)
    w_out = jax.random.normal(ks[21], (DEPTH, MIX_WIDTH, D_MODEL), f32) * (MIX_WIDTH ** -0.5)
    return {'x_prompt': x_prompt, 'x_sample': x_sample, 'cache_k': cache_k, 'cache_v': cache_v,
            'cache_logf': cache_logf, 'state_ssm': state_ssm, 'state_conv': state_conv,
            'page_table': page_table, 'c_prompt': c_prompt, 'c_sample': c_sample,
            'norm_g': norm_g, 'w_ada': w_ada, 'b_ada': b_ada, 'w_in': w_in, 'b_f': b_f,
            'conv_w': conv_w, 'a_log': a_log, 'dt_bias': dt_bias, 'gdn_norm_g': gdn_norm_g,
            'q_norm_g': q_norm_g, 'k_norm_g': k_norm_g, 'w_out': w_out}


def math_log(v):
    return float(np.log(v))


def reference(x_prompt, x_sample, cache_k, cache_v, cache_logf, state_ssm, state_conv, page_table,
              c_prompt, c_sample, norm_g, w_ada, b_ada, w_in, b_f, conv_w, a_log, dt_bias,
              gdn_norm_g, q_norm_g, k_norm_g, w_out):
    y_p = x_prompt
    y_s = x_sample
    kp_l, vp_l, lfp_l, sp_l, cp_l = [], [], [], [], []
    ks_l, vs_l, lfs_l, ss_l, cs_l = [], [], [], [], []
    for l in range(DEPTH):
        wl = (norm_g[l], w_ada[l], b_ada[l], w_in[l], b_f[l], conv_w[l], a_log[l], dt_bias[l],
              gdn_norm_g[l], q_norm_g[l], k_norm_g[l], w_out[l])
        conv0 = jnp.zeros((x_prompt.shape[0], CONV_W - 1, GDN_CONV_DIM), x_prompt.dtype)
        ssm0 = jnp.zeros((x_prompt.shape[0], GDN_HEADS, GDN_DK, GDN_DV), jnp.float32)
        y_p, kp, vp, lfp, sp, cp = hybrid_layer(y_p, c_prompt, conv0, ssm0, None, *wl)
        y_s, kk, vv, lfs, ss, cs = hybrid_layer(y_s, c_sample, state_conv[l], state_ssm[l],
                                               (cache_k[l], cache_v[l], cache_logf[l], page_table), *wl)
        kp_l.append(kp); vp_l.append(vp); lfp_l.append(lfp); sp_l.append(sp); cp_l.append(cp)
        ks_l.append(kk); vs_l.append(vv); lfs_l.append(lfs); ss_l.append(ss); cs_l.append(cs)
    return (y_p, y_s,
            jnp.stack(kp_l), jnp.stack(vp_l), jnp.stack(lfp_l), jnp.stack(sp_l), jnp.stack(cp_l),
            jnp.stack(ks_l), jnp.stack(vs_l), jnp.stack(lfs_l), jnp.stack(ss_l), jnp.stack(cs_l))
```

```python
import functools

import jax
import jax.numpy as jnp
from jax import lax
from jax.experimental import pallas as pl
from jax.experimental.pallas import tpu as pltpu

F32 = jnp.float32
BF16 = jnp.bfloat16

D_MODEL = 1024
HEADS = 4
HEAD_DIM = 128
WIDTH = HEADS * HEAD_DIM
CONV_W = 4
CONV_DIM = 3 * WIDTH
CHUNK = 64
PAGE_SIZE = 128
PAGE_ROWS = PAGE_SIZE * HEADS
NORM_EPS = 1e-6
LANES = 128
SUBLANES = 8
NEG_BIG = -1e30
VMEM_LIMIT = 48 * 1024 * 1024

COL_GQKV = 0
COL_GZ = 3 * WIDTH
COL_FQ = 4 * WIDTH
COL_FK = 5 * WIDTH
COL_FV = 6 * WIDTH
COL_FZ = 7 * WIDTH
BIG_COLS = 8 * WIDTH
LANE_B, LANE_G, LANE_F, LANE_LF = 0, 4, 8, 12

PROMPT_TM = 256
SMALL_TB = 512
GDN_TB = 512
ATT_T = 512
DEC_PAGES = 16
Q_ROWS = 16


def _mm(a, b):
    return jnp.dot(a.astype(BF16), b.astype(BF16), preferred_element_type=F32)


def _mm_nt(a, b):
    return lax.dot_general(a.astype(BF16), b.astype(BF16), (((1,), (1,)), ((), ())),
                           preferred_element_type=F32)


def _mm_tn(a, b):
    return lax.dot_general(a.astype(BF16), b.astype(BF16), (((0,), (0,)), ((), ())),
                           preferred_element_type=F32)


def _mm_exact_rhs(mat_b, y):
    hi = y.astype(BF16)
    r1 = y - hi.astype(F32)
    mid = r1.astype(BF16)
    lo = (r1 - mid.astype(F32)).astype(BF16)
    dot = functools.partial(jnp.dot, preferred_element_type=F32)
    return dot(mat_b, hi) + dot(mat_b, mid) + dot(mat_b, lo)


def _mm_exact_lhs(y, mat_b):
    hi = y.astype(BF16)
    r1 = y - hi.astype(F32)
    mid = r1.astype(BF16)
    lo = (r1 - mid.astype(F32)).astype(BF16)
    dot = functools.partial(jnp.dot, preferred_element_type=F32)
    return dot(hi, mat_b) + dot(mid, mat_b) + dot(lo, mat_b)


def _sigmoid(x):
    return 1.0 / (1.0 + jnp.exp(-x))


def _silu(x):
    return x * _sigmoid(x)


def _softplus(x):
    return jnp.maximum(x, 0.0) + jnp.log(1.0 + jnp.exp(-jnp.abs(x)))


def _ones_mask(mask):
    return jnp.where(mask, 1.0, 0.0).astype(BF16)


def _params(n_axes):
    return pltpu.CompilerParams(dimension_semantics=("arbitrary",) * n_axes,
                                vmem_limit_bytes=VMEM_LIMIT)


def _ada_kernel(cp_ref, cs_ref, w_ref, b_ref, op_ref, os_ref):
    w = w_ref[...].astype(BF16)
    b = b_ref[...]
    op_ref[0] = jnp.dot(_silu(cp_ref[...]).astype(BF16), w, preferred_element_type=F32) + b
    os_ref[0] = jnp.dot(_silu(cs_ref[...]).astype(BF16), w, preferred_element_type=F32) + b


def _ada_mod(c_prompt8, c_sample, w_ada, b_ada):
    nb = c_sample.shape[0]
    return pl.pallas_call(
        _ada_kernel,
        grid=(3,),
        in_specs=[pl.BlockSpec((SUBLANES, D_MODEL), lambda k: (0, 0)),
                  pl.BlockSpec((nb, D_MODEL), lambda k: (0, 0)),
                  pl.BlockSpec((D_MODEL, D_MODEL), lambda k: (0, k)),
                  pl.BlockSpec((1, D_MODEL), lambda k: (0, k))],
        out_specs=[pl.BlockSpec((1, SUBLANES, D_MODEL), lambda k: (k, 0, 0)),
                   pl.BlockSpec((1, nb, D_MODEL), lambda k: (k, 0, 0))],
        out_shape=[jax.ShapeDtypeStruct((3, SUBLANES, D_MODEL), F32),
                   jax.ShapeDtypeStruct((3, nb, D_MODEL), F32)],
        compiler_params=_params(1),
        name="ada_mod",
    )(c_prompt8, c_sample, w_ada, b_ada)


def _inproj_kernel(x_ref, sh_ref, sc_ref, ng_ref, wb_ref, ws_ref, qg_ref, kg_ref,
                   qkv_ref, gz_ref, sm_ref, fq_ref, fk_ref, fkb_ref, fv_ref, fvb_ref, fzg_ref):
    x = x_ref[...]
    y = x * lax.rsqrt(jnp.mean(x * x, axis=-1, keepdims=True) + NORM_EPS)
    hdn = (y * ng_ref[...]) * (1.0 + sc_ref[0, 0]) + sh_ref[0, 0]
    hb = hdn.astype(BF16)
    dot = functools.partial(jnp.dot, preferred_element_type=F32)
    qkv_ref[...] = dot(hb, wb_ref[:, COL_GQKV:COL_GZ])
    gz_ref[...] = dot(hb, wb_ref[:, COL_GZ:COL_FQ])
    sm_ref[...] = dot(hb, ws_ref[...])
    fq = dot(hb, wb_ref[:, COL_FQ:COL_FK])
    fk = dot(hb, wb_ref[:, COL_FK:COL_FV])
    fv = dot(hb, wb_ref[:, COL_FV:COL_FZ])
    fz = dot(hb, wb_ref[:, COL_FZ:BIG_COLS])
    fv_ref[...] = fv
    fvb_ref[...] = fv.astype(BF16)
    fzg_ref[...] = _silu(fz).astype(BF16)
    scale = HEAD_DIM ** -0.5
    for h in range(HEADS):
        sl = slice(h * HEAD_DIM, (h + 1) * HEAD_DIM)
        qh = fq[:, sl]
        qn = (qh * lax.rsqrt(jnp.mean(qh * qh, axis=-1, keepdims=True) + NORM_EPS)) * qg_ref[...]
        fq_ref[:, sl] = (qn * scale).astype(BF16)
        kh = fk[:, sl]
        kn = (kh * lax.rsqrt(jnp.mean(kh * kh, axis=-1, keepdims=True) + NORM_EPS)) * kg_ref[...]
        fk_ref[:, sl] = kn
        fkb_ref[:, sl] = kn.astype(BF16)


def _inproj(x2d, mod4, per_row_mod, tm, norm_g, w_big, w_small, q_norm_g, k_norm_g):
    rows = x2d.shape[0]
    nblk = rows // tm
    if per_row_mod:
        mod_block = (1, 1, tm, D_MODEL)
        sh_map = lambda i: (0, 0, i, 0)
        sc_map = lambda i: (1, 0, i, 0)
    else:
        blocks_per_batch = nblk // mod4.shape[1]
        mod_block = (1, 1, 1, D_MODEL)
        sh_map = lambda i: (0, i // blocks_per_batch, 0, 0)
        sc_map = lambda i: (1, i // blocks_per_batch, 0, 0)
    row_spec = lambda n: pl.BlockSpec((tm, n), lambda i: (i, 0))
    const = lambda shape: pl.BlockSpec(shape, lambda i: (0,) * len(shape))
    out_cols = [(CONV_DIM, F32), (WIDTH, F32), (LANES, F32), (WIDTH, BF16), (WIDTH, F32),
                (WIDTH, BF16), (WIDTH, F32), (WIDTH, BF16), (WIDTH, BF16)]
    return pl.pallas_call(
        _inproj_kernel,
        grid=(nblk,),
        in_specs=[row_spec(D_MODEL),
                  pl.BlockSpec(mod_block, sh_map), pl.BlockSpec(mod_block, sc_map),
                  const((1, D_MODEL)), const((D_MODEL, BIG_COLS)), const((D_MODEL, LANES)),
                  const((1, HEAD_DIM)), const((1, HEAD_DIM))],
        out_specs=[row_spec(n) for n, _ in out_cols],
        out_shape=[jax.ShapeDtypeStruct((rows, n), dt) for n, dt in out_cols],
        compiler_params=_params(1),
        name="in_proj",
    )(x2d, mod4, mod4, norm_g, w_big, w_small, q_norm_g, k_norm_g)


def _token_scalars(x, par):
    beta = _sigmoid(x)
    g = -jnp.exp(par[0:1, :]) * _softplus(x + par[1:2, :])
    lf = -_softplus(-(x + par[2:3, :]))
    return beta, g, lf


def _small_kernel(sm_ref, par_ref, col_ref, row_ref, carry_ref):
    tb = SMALL_TB

    @pl.when(pl.program_id(1) == 0)
    def _():
        carry_ref[...] = jnp.zeros_like(carry_ref)

    x = sm_ref[...]
    beta, g, lf = _token_scalars(x, par_ref[...])
    lane = lax.broadcasted_iota(jnp.int32, (tb, LANES), 1)
    is_b = lane < LANE_G
    is_g = (lane >= LANE_G) & (lane < LANE_F)
    is_f = (lane >= LANE_F) & (lane < LANE_LF)
    gm = jnp.where(is_g, g, 0.0)
    fm = jnp.where(is_f, lf, 0.0)
    r = lax.broadcasted_iota(jnp.int32, (tb, tb), 0)
    c = lax.broadcasted_iota(jnp.int32, (tb, tb), 1)
    tri = c <= r
    same_chunk = jnp.right_shift(r, 6) == jnp.right_shift(c, 6)
    gcum = _mm_exact_rhs(_ones_mask(tri & same_chunk), gm)
    cum = _mm_exact_rhs(_ones_mask(tri), fm) + carry_ref[...]
    cum = jnp.where(is_f, cum, 0.0)
    carry_ref[...] = cum[tb - 1:tb, :]
    col = jnp.where(is_b, beta, 0.0) + gcum + cum + pltpu.roll(fm, LANE_LF - LANE_F, axis=1)
    col_ref[...] = col
    for i in range(tb // LANES):
        t = col[i * LANES:(i + 1) * LANES, :]
        row_ref[0, :, i * LANES:(i + 1) * LANES] = t.T[0:2 * SUBLANES, :]


def _small_prep(small, par, batch, seq):
    nt = seq // SMALL_TB
    return pl.pallas_call(
        _small_kernel,
        grid=(batch, nt),
        in_specs=[pl.BlockSpec((SMALL_TB, LANES), lambda b, t: (b * nt + t, 0)),
                  pl.BlockSpec((SUBLANES, LANES), lambda b, t: (0, 0))],
        out_specs=[pl.BlockSpec((SMALL_TB, LANES), lambda b, t: (b * nt + t, 0)),
                   pl.BlockSpec((1, 2 * SUBLANES, SMALL_TB), lambda b, t: (b, 0, t))],
        out_shape=[jax.ShapeDtypeStruct((batch * seq, LANES), F32),
                   jax.ShapeDtypeStruct((batch, 2 * SUBLANES, seq), F32)],
        scratch_shapes=[pltpu.VMEM((1, LANES), F32)],
        compiler_params=_params(2),
        name="token_scalars",
    )(small, par)


def _mm_hp(a, b):
    return jnp.dot(a, b, precision=lax.Precision.HIGHEST, preferred_element_type=F32)


def _unit_lower_inverse(m, eye, ri, ci):
    base = 3
    d = jnp.where(jnp.right_shift(ri, base) == jnp.right_shift(ci, base), m, 0.0)
    xp = -d
    p = eye + xp
    for _ in range(base - 1):
        xp = _mm_hp(xp, xp)
        p = p + _mm_hp(p, xp)
    size = base
    while (1 << size) < CHUNK:
        off = ((jnp.right_shift(ri, size + 1) == jnp.right_shift(ci, size + 1))
               & (jnp.right_shift(ri, size) != jnp.right_shift(ci, size)))
        p = p - _mm_hp(_mm_hp(p, jnp.where(off, m, 0.0)), p)
        size += 1
    return p


def _gdn_chunk(q, k, v, beta, gc, gr, s_old, incl, strict, eye, ri, ci):
    gdiff = gc - gr
    decay = jnp.where(incl, jnp.exp(jnp.where(incl, gdiff, 0.0)), 0.0)
    kb = k * beta
    m = jnp.where(strict, _mm_nt(kb, k) * decay, 0.0)
    tinv = _unit_lower_inverse(m, eye, ri, ci)
    eg = jnp.exp(gc)
    u = _mm(tinv, v * beta)
    w = _mm(tinv, kb * eg)
    a_intra = _mm_nt(q, k) * decay
    v_new = u - _mm(w, s_old)
    o = _mm(q * eg, s_old) + _mm(a_intra, v_new)
    g_last = gc[CHUNK - 1:CHUNK, :]
    s_new = s_old * jnp.exp(g_last) + _mm_tn(k * jnp.exp(g_last - gc), v_new)
    return o, s_new


def _gdn_kernel(qkv_ref, gz_ref, col_ref, row_ref, cw_ref, gn_ref,
                o_ref, ssm_ref, conv_ref, s_ref, xbuf_ref, qn_ref, kn_ref, vn_ref):
    tb = GDN_TB
    t = pl.program_id(1)
    halo = SUBLANES

    @pl.when(t == 0)
    def _():
        s_ref[...] = jnp.zeros_like(s_ref)
        xbuf_ref[0:halo, :] = jnp.zeros((halo, CONV_DIM), F32)

    xbuf_ref[halo:halo + tb, :] = qkv_ref[...]
    scale = HEAD_DIM ** -0.5
    for cb in range(CONV_DIM // LANES):
        sl = slice(cb * LANES, (cb + 1) * LANES)
        acc = xbuf_ref[halo - 3:halo - 3 + tb, sl] * cw_ref[0:1, sl]
        for i in range(1, CONV_W):
            acc = acc + xbuf_ref[halo - 3 + i:halo - 3 + i + tb, sl] * cw_ref[i:i + 1, sl]
        act = _silu(acc)
        part, h = divmod(cb, HEADS)
        hs = slice(h * LANES, (h + 1) * LANES)
        if part == 0:
            qn_ref[:, hs] = act * lax.rsqrt(jnp.sum(act * act, axis=-1, keepdims=True) + NORM_EPS) * scale
        elif part == 1:
            kn_ref[:, hs] = act * lax.rsqrt(jnp.sum(act * act, axis=-1, keepdims=True) + NORM_EPS)
        else:
            vn_ref[:, hs] = act
    tail = xbuf_ref[tb:tb + halo, :]
    xbuf_ref[0:halo, :] = tail

    ri = lax.broadcasted_iota(jnp.int32, (CHUNK, CHUNK), 0)
    ci = lax.broadcasted_iota(jnp.int32, (CHUNK, CHUNK), 1)
    incl = ci <= ri
    strict = ci < ri
    eye = jnp.where(ci == ri, 1.0, 0.0)

    def pair_body(i, carry):
        base = pl.multiple_of(i * LANES, LANES)
        colt = col_ref[pl.ds(base, LANES), :]
        rowt = row_ref[0, :, pl.ds(base, LANES)]
        for cc in range(LANES // CHUNK):
            r0 = pl.multiple_of(base + cc * CHUNK, CHUNK)
            rs = slice(cc * CHUNK, (cc + 1) * CHUNK)
            for h in range(HEADS):
                hs = slice(h * HEAD_DIM, (h + 1) * HEAD_DIM)
                q = qn_ref[pl.ds(r0, CHUNK), hs]
                k = kn_ref[pl.ds(r0, CHUNK), hs]
                v = vn_ref[pl.ds(r0, CHUNK), hs]
                beta = colt[rs, LANE_B + h:LANE_B + h + 1]
                gc = colt[rs, LANE_G + h:LANE_G + h + 1]
                gr = rowt[LANE_G + h:LANE_G + h + 1, rs]
                o, s_new = _gdn_chunk(q, k, v, beta, gc, gr, s_ref[h], incl, strict, eye, ri, ci)
                s_ref[h] = s_new
                on = (o * lax.rsqrt(jnp.mean(o * o, axis=-1, keepdims=True) + NORM_EPS)) * gn_ref[...]
                gate = _silu(gz_ref[pl.ds(r0, CHUNK), hs])
                o_ref[pl.ds(r0, CHUNK), hs] = (on * gate).astype(o_ref.dtype)
        return carry

    lax.fori_loop(0, tb // LANES, pair_body, 0)

    @pl.when(t == pl.num_programs(1) - 1)
    def _():
        ssm_ref[0] = s_ref[...]
        conv_ref[0] = tail[halo - (CONV_W - 1):halo, :]


def _gdn_prompt(qkv, gz, col, row, conv_w, gdn_norm_g, batch, seq):
    nt = seq // GDN_TB
    rows = batch * seq
    row_spec = lambda n: pl.BlockSpec((GDN_TB, n), lambda b, t: (b * nt + t, 0))
    return pl.pallas_call(
        _gdn_kernel,
        grid=(batch, nt),
        in_specs=[row_spec(CONV_DIM), row_spec(WIDTH), row_spec(LANES),
                  pl.BlockSpec((1, 2 * SUBLANES, GDN_TB), lambda b, t: (b, 0, t)),
                  pl.BlockSpec((CONV_W, CONV_DIM), lambda b, t: (0, 0)),
                  pl.BlockSpec((1, HEAD_DIM), lambda b, t: (0, 0))],
        out_specs=[row_spec(WIDTH),
                   pl.BlockSpec((1, HEADS, HEAD_DIM, HEAD_DIM), lambda b, t: (b, 0, 0, 0)),
                   pl.BlockSpec((1, CONV_W - 1, CONV_DIM), lambda b, t: (b, 0, 0))],
        out_shape=[jax.ShapeDtypeStruct((rows, WIDTH), BF16),
                   jax.ShapeDtypeStruct((batch, HEADS, HEAD_DIM, HEAD_DIM), F32),
                   jax.ShapeDtypeStruct((batch, CONV_W - 1, CONV_DIM), F32)],
        scratch_shapes=[pltpu.VMEM((HEADS, HEAD_DIM, HEAD_DIM), F32),
                        pltpu.VMEM((GDN_TB + SUBLANES, CONV_DIM), F32),
                        pltpu.VMEM((GDN_TB, WIDTH), F32),
                        pltpu.VMEM((GDN_TB, WIDTH), F32),
                        pltpu.VMEM((GDN_TB, WIDTH), F32)],
        compiler_params=_params(2),
        name="gdn_prompt",
    )(qkv, gz, col, row, conv_w, gdn_norm_g)


def _att_kernel(q_ref, k_ref, v_ref, ck_ref, cq_ref, gate_ref, o_ref, m_ref, l_ref, acc_ref):
    qi = pl.program_id(1)
    kj = pl.program_id(2)
    tq = tk = ATT_T

    @pl.when(kj == 0)
    def _():
        m_ref[...] = jnp.full_like(m_ref, NEG_BIG)
        l_ref[...] = jnp.zeros_like(l_ref)
        acc_ref[...] = jnp.zeros_like(acc_ref)

    @pl.when(kj <= qi)
    def _():
        rows = qi * tq + lax.broadcasted_iota(jnp.int32, (tq, tk), 0)
        cols = kj * tk + lax.broadcasted_iota(jnp.int32, (tq, tk), 1)
        visible = cols <= rows
        for h in range(HEADS):
            hs = slice(h * HEAD_DIM, (h + 1) * HEAD_DIM)
            s = lax.dot_general(q_ref[:, hs], k_ref[:, hs], (((1,), (1,)), ((), ())),
                                preferred_element_type=F32)
            ck = ck_ref[0, LANE_F + h:LANE_F + h + 1, :]
            cq = cq_ref[:, LANE_F + h:LANE_F + h + 1]
            tt = jnp.where(visible, s - ck, NEG_BIG)
            m_old = m_ref[h]
            m_new = jnp.maximum(m_old, jnp.max(tt, axis=1, keepdims=True) + cq)
            p = jnp.exp(tt - (m_new - cq))
            alpha = jnp.exp(m_old - m_new)
            l_ref[h] = alpha * l_ref[h] + jnp.sum(p, axis=1, keepdims=True)
            acc_ref[h] = alpha * acc_ref[h] + jnp.dot(p.astype(BF16), v_ref[:, hs],
                                                      preferred_element_type=F32)
            m_ref[h] = m_new

    @pl.when(kj == qi)
    def _():
        for h in range(HEADS):
            hs = slice(h * HEAD_DIM, (h + 1) * HEAD_DIM)
            o = acc_ref[h] / l_ref[h]
            o_ref[:, hs] = (o * gate_ref[:, hs].astype(F32)).astype(o_ref.dtype)


def _att_prompt(q, k, v, row, col, gate, batch, seq):
    n = seq // ATT_T
    rows = batch * seq
    qmap = lambda b, i, j: (b * n + i, 0)
    kmap = lambda b, i, j: (b * n + jnp.minimum(i, j), 0)
    return pl.pallas_call(
        _att_kernel,
        grid=(batch, n, n),
        in_specs=[pl.BlockSpec((ATT_T, WIDTH), qmap),
                  pl.BlockSpec((ATT_T, WIDTH), kmap),
                  pl.BlockSpec((ATT_T, WIDTH), kmap),
                  pl.BlockSpec((1, 2 * SUBLANES, ATT_T), lambda b, i, j: (b, 0, jnp.minimum(i, j))),
                  pl.BlockSpec((ATT_T, LANES), qmap),
                  pl.BlockSpec((ATT_T, WIDTH), qmap)],
        out_specs=pl.BlockSpec((ATT_T, WIDTH), qmap),
        out_shape=jax.ShapeDtypeStruct((rows, WIDTH), BF16),
        scratch_shapes=[pltpu.VMEM((HEADS, ATT_T, 1), F32),
                        pltpu.VMEM((HEADS, ATT_T, 1), F32),
                        pltpu.VMEM((HEADS, ATT_T, HEAD_DIM), F32)],
        compiler_params=_params(3),
        name="fox_prompt",
    )(q, k, v, row, col, gate)


def _outproj_kernel(x_ref, oa_ref, ob_ref, gate_ref, w_ref, y_ref):
    mix = (jnp.dot(oa_ref[...].astype(BF16), w_ref[0:WIDTH, :], preferred_element_type=F32)
           + jnp.dot(ob_ref[...].astype(BF16), w_ref[WIDTH:2 * WIDTH, :], preferred_element_type=F32))
    y_ref[...] = x_ref[...] + gate_ref[0, 0] * mix


def _outproj(x2d, oa, ob, mod4, per_row_mod, tm, w_out_b):
    rows = x2d.shape[0]
    nblk = rows // tm
    if per_row_mod:
        mod_block = (1, 1, tm, D_MODEL)
        g_map = lambda i: (2, 0, i, 0)
    else:
        blocks_per_batch = nblk // mod4.shape[1]
        mod_block = (1, 1, 1, D_MODEL)
        g_map = lambda i: (2, i // blocks_per_batch, 0, 0)
    row_spec = lambda n: pl.BlockSpec((tm, n), lambda i: (i, 0))
    return pl.pallas_call(
        _outproj_kernel,
        grid=(nblk,),
        in_specs=[row_spec(D_MODEL), row_spec(WIDTH), row_spec(WIDTH),
                  pl.BlockSpec(mod_block, g_map),
                  pl.BlockSpec((2 * WIDTH, D_MODEL), lambda i: (0, 0))],
        out_specs=row_spec(D_MODEL),
        out_shape=jax.ShapeDtypeStruct((rows, D_MODEL), F32),
        compiler_params=_params(1),
        name="out_proj",
    )(x2d, oa, ob, mod4, w_out_b)


def _smid_kernel(u_ref, s0_ref, s1_ref, s2_ref, cw_ref, sm_ref, par_ref,
                 qn_ref, kn_ref, vn_ref, sc_ref):
    conv = (s0_ref[...] * cw_ref[0:1, :] + s1_ref[...] * cw_ref[1:2, :]
            + s2_ref[...] * cw_ref[2:3, :] + u_ref[...] * cw_ref[3:4, :])
    act = _silu(conv)
    for h in range(HEADS):
        hs = slice(h * HEAD_DIM, (h + 1) * HEAD_DIM)
        q = act[:, h * HEAD_DIM:(h + 1) * HEAD_DIM]
        k = act[:, WIDTH + h * HEAD_DIM:WIDTH + (h + 1) * HEAD_DIM]
        qn_ref[:, hs] = q * lax.rsqrt(jnp.sum(q * q, axis=-1, keepdims=True) + NORM_EPS)
        kn_ref[:, hs] = k * lax.rsqrt(jnp.sum(k * k, axis=-1, keepdims=True) + NORM_EPS)
    vn_ref[...] = act[:, 2 * WIDTH:3 * WIDTH]
    beta, g, lf = _token_scalars(sm_ref[...], par_ref[...])
    lane = lax.broadcasted_iota(jnp.int32, sm_ref.shape, 1)
    sc_ref[...] = jnp.where(lane < LANE_G, beta, jnp.where(lane < LANE_F, g, lf))


def _sample_mid(u, s0, s1, s2, conv_w, small, par):
    nb = u.shape[0]
    full = lambda a: pl.BlockSpec(a.shape, lambda: (0,) * a.ndim)
    args = (u, s0, s1, s2, conv_w, small, par)
    return pl.pallas_call(
        _smid_kernel,
        in_specs=[full(a) for a in args],
        out_specs=[pl.BlockSpec((nb, WIDTH), lambda: (0, 0))] * 3 + [pl.BlockSpec((nb, LANES), lambda: (0, 0))],
        out_shape=[jax.ShapeDtypeStruct((nb, WIDTH), F32)] * 3 + [jax.ShapeDtypeStruct((nb, LANES), F32)],
        compiler_params=pltpu.CompilerParams(vmem_limit_bytes=VMEM_LIMIT),
        name="sample_conv",
    )(*args)


def _gdn_step_kernel(t_ref, s_ref, gz_ref, gn_ref, o_ref, sn_ref):
    scale = HEAD_DIM ** -0.5
    row = lax.broadcasted_iota(jnp.int32, (SUBLANES, HEAD_DIM), 0)
    for h in range(HEADS):
        t = t_ref[0, h]
        s_old = s_ref[0, h]
        r = _mm(t, s_old)
        k, q, v, beta, g = (t[i:i + 1, :] for i in range(5))
        eg = jnp.exp(g)
        v_new = beta * (v - eg * r[0:1, :])
        qk = jnp.sum(q * k, axis=-1, keepdims=True)
        o = scale * (eg * r[1:2, :] + qk * v_new)
        k_only = jnp.where(row == 0, t, 0.0)
        v_only = jnp.where(row == 0, jnp.broadcast_to(v_new, (SUBLANES, HEAD_DIM)), 0.0)
        sn_ref[0, h] = s_old * eg + _mm_tn(k_only, v_only)
        on = (o * lax.rsqrt(jnp.mean(o * o, axis=-1, keepdims=True) + NORM_EPS)) * gn_ref[...]
        o_ref[0, h:h + 1, :] = on * _silu(gz_ref[0, h:h + 1, :])


def _gdn_step(t8, state, gz3, gdn_norm_g):
    nb = state.shape[0]
    return pl.pallas_call(
        _gdn_step_kernel,
        grid=(nb,),
        in_specs=[pl.BlockSpec((1, HEADS, SUBLANES, HEAD_DIM), lambda b: (b, 0, 0, 0)),
                  pl.BlockSpec((1, HEADS, HEAD_DIM, HEAD_DIM), lambda b: (b, 0, 0, 0)),
                  pl.BlockSpec((1, HEADS, HEAD_DIM), lambda b: (b, 0, 0)),
                  pl.BlockSpec((1, HEAD_DIM), lambda b: (0, 0))],
        out_specs=[pl.BlockSpec((1, HEADS, HEAD_DIM), lambda b: (b, 0, 0)),
                   pl.BlockSpec((1, HEADS, HEAD_DIM, HEAD_DIM), lambda b: (b, 0, 0, 0))],
        out_shape=[jax.ShapeDtypeStruct((nb, HEADS, HEAD_DIM), F32),
                   jax.ShapeDtypeStruct((nb, HEADS, HEAD_DIM, HEAD_DIM), F32)],
        compiler_params=_params(1),
        name="gdn_step",
    )(t8, state, gz3, gdn_norm_g)


def _class_reduce(x, op):
    shift = HEADS
    while shift < LANES:
        x = op(x, pltpu.roll(x, shift, axis=1))
        shift *= 2
    return x


def _lane_to_col(v, eye8):
    return jnp.sum(jnp.where(eye8, jnp.broadcast_to(v, (SUBLANES, LANES)), 0.0), axis=1, keepdims=True)


def _dec_kernel(pt_ref, q8_ref, kn_ref, vn_ref, gate_ref, lfn_ref, *refs):
    del pt_ref
    gp = DEC_PAGES
    k_refs, v_refs, lf_refs = refs[0:gp], refs[gp:2 * gp], refs[2 * gp:3 * gp]
    o_ref = refs[3 * gp]
    qrep_ref, m_ref, l_ref, acc_ref, carry_ref, sd_ref, lf_ref = refs[3 * gp + 1:]
    g = pl.program_id(1)
    nrow = gp * HEADS

    @pl.when(g == 0)
    def _():
        qrep_ref[...] = jnp.concatenate([q8_ref[0]] * (LANES // Q_ROWS), axis=0)
        m_ref[...] = jnp.full_like(m_ref, NEG_BIG)
        l_ref[...] = jnp.zeros_like(l_ref)
        acc_ref[...] = jnp.zeros_like(acc_ref)
        carry_ref[...] = lfn_ref[0]

    ri = lax.broadcasted_iota(jnp.int32, (LANES, LANES), 0)
    ci = lax.broadcasted_iota(jnp.int32, (LANES, LANES), 1)
    eye = ri == ci
    same_head = (ri & (HEADS - 1)) == (ci & (HEADS - 1))
    eye8 = (lax.broadcasted_iota(jnp.int32, (SUBLANES, LANES), 0)
            == lax.broadcasted_iota(jnp.int32, (SUBLANES, LANES), 1))

    for i in range(gp):
        lf_ref[i * HEADS:(i + 1) * HEADS, :] = lf_refs[gp - 1 - i][0]
    lf = lf_ref[...]
    within = _mm_exact_lhs(lf, _ones_mask(same_head & (ri > ci)))
    tot = _mm_exact_lhs(lf, _ones_mask(same_head))
    si = lax.broadcasted_iota(jnp.int32, (nrow, nrow), 0)
    sj = lax.broadcasted_iota(jnp.int32, (nrow, nrow), 1)
    later = _mm_exact_rhs(_ones_mask(sj > si), tot)
    bias = within + later + carry_ref[...]
    carry_ref[...] = carry_ref[...] + jnp.sum(tot, axis=0, keepdims=True)

    qrep = qrep_ref[...]
    for r in range(gp):
        s2 = _mm_nt(k_refs[r][...], qrep)
        for a in range(HEADS):
            blk = s2[a * LANES:(a + 1) * LANES, :]
            sig = (gp - 1 - r) * HEADS + a
            sd_ref[sig:sig + 1, :] = jnp.sum(jnp.where(eye, blk, 0.0), axis=0, keepdims=True)

    tt = sd_ref[...] + bias
    mx = jnp.max(tt, axis=0, keepdims=True)
    mx = _class_reduce(jnp.broadcast_to(mx, (SUBLANES, LANES)), jnp.maximum)[0:1, :]
    m_old = m_ref[...]
    m_new = jnp.maximum(m_old, mx)
    pb = jnp.exp(tt - m_new).astype(BF16)
    pf = pb.astype(F32)
    alpha = jnp.exp(m_old - m_new)
    l_ref[...] = alpha * l_ref[...] + jnp.sum(pf, axis=0, keepdims=True)
    m_ref[...] = m_new
    acc = acc_ref[...] * _lane_to_col(alpha, eye8)
    ones = jnp.ones((LANES, LANES), BF16)
    for r in range(gp):
        parts = []
        for a in range(HEADS):
            sig = (gp - 1 - r) * HEADS + a
            prow = jnp.broadcast_to(pf[sig:sig + 1, :], (LANES, LANES))
            parts.append(jnp.where(eye, prow, 0.0).astype(BF16))
        pcol = jnp.dot(jnp.concatenate(parts, axis=0), ones, preferred_element_type=F32)
        pv = pcol * v_refs[r][...]
        acc = acc + jnp.sum(pv.reshape(PAGE_ROWS // SUBLANES, SUBLANES, LANES), axis=0)
    acc_ref[...] = acc

    @pl.when(g == pl.num_programs(1) - 1)
    def _():
        acc4 = acc[0:HEADS, :] + acc[HEADS:2 * HEADS, :]
        mcol = _lane_to_col(m_new, eye8)[0:HEADS, :]
        lsum = _class_reduce(jnp.broadcast_to(l_ref[...], (SUBLANES, LANES)), jnp.add)[0:1, :]
        lcol = _lane_to_col(lsum, eye8)[0:HEADS, :]
        q4 = q8_ref[0][0:HEADS, :].astype(F32)
        s_new = jnp.sum(q4 * kn_ref[0], axis=-1, keepdims=True)
        m_f = jnp.maximum(mcol, s_new)
        a1 = jnp.exp(mcol - m_f)
        a2 = jnp.exp(s_new - m_f)
        o = (acc4 * a1 + a2 * vn_ref[0]) / (lcol * a1 + a2)
        o_ref[0] = o * gate_ref[0].astype(F32)


def _dec_attention(page_table, q8, k_new, v_new, gate, lf_new, cache_k2, cache_v2, cache_lf3):
    nb, n_pages = page_table.shape
    gp = DEC_PAGES
    ng = n_pages // gp

    def page(r):
        return lambda b, g, pt: (pt[b, n_pages - 1 - (g * gp + r)], 0)

    def page3(r):
        return lambda b, g, pt: (pt[b, n_pages - 1 - (g * gp + r)], 0, 0)

    per_b = lambda b, g, pt: (b, 0, 0)
    in_specs = ([pl.BlockSpec((1, Q_ROWS, HEAD_DIM), per_b),
                 pl.BlockSpec((1, HEADS, HEAD_DIM), per_b),
                 pl.BlockSpec((1, HEADS, HEAD_DIM), per_b),
                 pl.BlockSpec((1, HEADS, HEAD_DIM), per_b),
                 pl.BlockSpec((1, 1, LANES), per_b)]
                + [pl.BlockSpec((PAGE_ROWS, HEAD_DIM), page(r)) for r in range(gp)]
                + [pl.BlockSpec((PAGE_ROWS, HEAD_DIM), page(r)) for r in range(gp)]
                + [pl.BlockSpec((1, HEADS, LANES), page3(r)) for r in range(gp)])
    grid_spec = pltpu.PrefetchScalarGridSpec(
        num_scalar_prefetch=1,
        grid=(nb, ng),
        in_specs=in_specs,
        out_specs=pl.BlockSpec((1, HEADS, HEAD_DIM), per_b),
        scratch_shapes=[pltpu.VMEM((LANES, HEAD_DIM), BF16),
                        pltpu.VMEM((1, LANES), F32),
                        pltpu.VMEM((1, LANES), F32),
                        pltpu.VMEM((SUBLANES, HEAD_DIM), F32),
                        pltpu.VMEM((1, LANES), F32),
                        pltpu.VMEM((gp * HEADS, LANES), F32),
                        pltpu.VMEM((gp * HEADS, LANES), F32)])
    return pl.pallas_call(
        _dec_kernel,
        grid_spec=grid_spec,
        out_shape=jax.ShapeDtypeStruct((nb, HEADS, HEAD_DIM), F32),
        compiler_params=_params(2),
        name="fox_decode",
    )(page_table, q8, k_new, v_new, gate, lf_new,
      *([cache_k2] * gp), *([cache_v2] * gp), *([cache_lf3] * gp))


def _layer(x_prompt, x_sample, cache_k, cache_v, cache_logf, state_ssm, state_conv, page_table,
           c_prompt, c_sample, norm_g, w_ada, b_ada, w_in, b_f, conv_w, a_log, dt_bias,
           gdn_norm_g, q_norm_g, k_norm_g, w_out):
    batch, seq, _ = x_prompt.shape
    nb = x_sample.shape[0]
    w = WIDTH
    o_small = 4 * w
    o_fox = o_small + 2 * HEADS
    w_big = jnp.concatenate([w_in[:, 0:o_small], w_in[:, o_fox:o_fox + 4 * w]], axis=1).astype(BF16)
    w_small = jnp.concatenate([w_in[:, o_small:o_fox], w_in[:, o_fox + 4 * w:],
                               jnp.zeros((D_MODEL, LANES - 3 * HEADS), F32)], axis=1).astype(BF16)
    w_out_b = w_out.astype(BF16)
    par = jnp.zeros((SUBLANES, LANES), F32)
    par = par.at[0, LANE_G:LANE_G + HEADS].set(a_log)
    par = par.at[1, LANE_G:LANE_G + HEADS].set(dt_bias)
    par = par.at[2, LANE_F:LANE_F + HEADS].set(b_f)
    ng2, qg2, kg2, gg2 = (a.reshape(1, -1) for a in (norm_g, q_norm_g, k_norm_g, gdn_norm_g))

    cp8 = jnp.pad(c_prompt, ((0, SUBLANES - batch), (0, 0)))
    mod_p, mod_s = _ada_mod(cp8, c_sample, w_ada, b_ada.reshape(1, -1))
    mod_p4 = mod_p[:, 0:batch].reshape(3, batch, 1, D_MODEL)
    mod_s4 = mod_s.reshape(3, 1, nb, D_MODEL)

    xp = x_prompt.reshape(batch * seq, D_MODEL)
    qkv, gz, small, fq, fk, fkb, fv, fvb, fzg = _inproj(xp, mod_p4, False, PROMPT_TM, ng2, w_big, w_small, qg2, kg2)
    col, row = _small_prep(small, par, batch, seq)
    o_a, ssm_p, conv_p = _gdn_prompt(qkv, gz, col, row, conv_w, gg2, batch, seq)
    o_b = _att_prompt(fq, fkb, fvb, row, col, fzg, batch, seq)
    y_p = _outproj(xp, o_a, o_b, mod_p4, False, PROMPT_TM, w_out_b).reshape(batch, seq, D_MODEL)
    k_p = fk.reshape(batch, seq, HEADS, HEAD_DIM)
    v_p = fv.reshape(batch, seq, HEADS, HEAD_DIM)
    lf_p = col[:, LANE_LF:LANE_LF + HEADS].reshape(batch, seq, HEADS)

    xs = x_sample.reshape(nb, D_MODEL)
    qkv_s, gz_s, small_s, fq_s, fk_s, _, fv_s, _, fzg_s = _inproj(xs, mod_s4, True, nb, ng2, w_big, w_small, qg2, kg2)
    qn_s, kn_s, vn_s, sc_s = _sample_mid(qkv_s, state_conv[:, 0], state_conv[:, 1], state_conv[:, 2],
                                         conv_w, small_s, par)
    h3 = lambda a: a.reshape(nb, HEADS, HEAD_DIM)
    bcast = lambda a: jnp.broadcast_to(a[:, :, None], (nb, HEADS, HEAD_DIM))
    t8 = jnp.stack([h3(kn_s), h3(qn_s), h3(vn_s), bcast(sc_s[:, LANE_B:LANE_B + HEADS]),
                    bcast(sc_s[:, LANE_G:LANE_G + HEADS])]
                   + [jnp.zeros((nb, HEADS, HEAD_DIM), F32)] * (SUBLANES - 5), axis=2)
    o_a_s, ssm_s = _gdn_step(t8, state_ssm, h3(gz_s), gg2)
    lf_s = sc_s[:, LANE_F:LANE_F + HEADS]
    q8 = jnp.tile(h3(fq_s), (1, Q_ROWS // HEADS, 1))
    lf_new = jnp.tile(lf_s, (1, LANES // HEADS)).reshape(nb, 1, LANES)
    n_pool = cache_k.shape[0]
    o_b_s = _dec_attention(page_table, q8, h3(fk_s), h3(fv_s), h3(fzg_s), lf_new,
                           cache_k.reshape(n_pool * PAGE_ROWS, HEAD_DIM),
                           cache_v.reshape(n_pool * PAGE_ROWS, HEAD_DIM),
                           cache_logf.reshape(n_pool, HEADS, LANES))
    y_s = _outproj(xs, o_a_s.reshape(nb, w), o_b_s.reshape(nb, w), mod_s4, True, nb, w_out_b).reshape(nb, 1, D_MODEL)
    conv_s = jnp.stack([state_conv[:, 1], state_conv[:, 2], qkv_s], axis=1)
    return (y_p, y_s, k_p, v_p, lf_p, ssm_p, conv_p,
            h3(fk_s).reshape(nb, 1, HEADS, HEAD_DIM), h3(fv_s).reshape(nb, 1, HEADS, HEAD_DIM),
            lf_s.reshape(nb, 1, HEADS), ssm_s, conv_s)


def kernel(x_prompt, x_sample, cache_k, cache_v, cache_logf, state_ssm, state_conv, page_table, c_prompt, c_sample, norm_g, w_ada, b_ada, w_in, b_f, conv_w, a_log, dt_bias, gdn_norm_g, q_norm_g, k_norm_g, w_out):
    assert w_ada.shape[0] == 1, "single layer"
    outs = _layer(x_prompt, x_sample, cache_k[0], cache_v[0], cache_logf[0], state_ssm[0], state_conv[0],
                  page_table, c_prompt, c_sample, norm_g[0], w_ada[0], b_ada[0], w_in[0], b_f[0], conv_w[0],
                  a_log[0], dt_bias[0], gdn_norm_g[0], q_norm_g[0], k_norm_g[0], w_out[0])
    y_p, y_s = outs[0], outs[1]
    return (y_p, y_s) + tuple(o[None] for o in outs[2:])
```

```python
import functools

import jax
import jax.numpy as jnp
from jax import lax
from jax.experimental import pallas as pl
from jax.experimental.pallas import tpu as pltpu

F32 = jnp.float32
BF16 = jnp.bfloat16

D_MODEL = 1024
HEADS = 4
HEAD_DIM = 128
WIDTH = HEADS * HEAD_DIM
CONV_W = 4
CONV_DIM = 3 * WIDTH
CHUNK = 64
PAGE_SIZE = 128
PAGE_ROWS = PAGE_SIZE * HEADS
NORM_EPS = 1e-6
LANES = 128
SUBLANES = 8
NEG_BIG = -1e30
VMEM_LIMIT = 48 * 1024 * 1024

COL_GQKV = 0
COL_GZ = 3 * WIDTH
COL_FQ = 4 * WIDTH
COL_FK = 5 * WIDTH
COL_FV = 6 * WIDTH
COL_FZ = 7 * WIDTH
BIG_COLS = 8 * WIDTH
LANE_B, LANE_G, LANE_F, LANE_LF = 0, 4, 8, 12

PROMPT_TM = 256
SMALL_TB = 512
GDN_TB = 512
GDN_GROUP = 256
ATT_T = 512
DEC_PAGES = 32
DEC_SUB = 16
DEC_VMEM_LIMIT = 56 * 1024 * 1024
Q_ROWS = 16
STEP_BATCH = 4


def _mm(a, b):
    return jnp.dot(a.astype(BF16), b.astype(BF16), preferred_element_type=F32)


def _mm_nt(a, b):
    return lax.dot_general(a.astype(BF16), b.astype(BF16), (((1,), (1,)), ((), ())),
                           preferred_element_type=F32)


def _mm_tn(a, b):
    return lax.dot_general(a.astype(BF16), b.astype(BF16), (((0,), (0,)), ((), ())),
                           preferred_element_type=F32)


def _mm_exact_rhs(mat_b, y):
    hi = y.astype(BF16)
    r1 = y - hi.astype(F32)
    mid = r1.astype(BF16)
    lo = (r1 - mid.astype(F32)).astype(BF16)
    dot = functools.partial(jnp.dot, preferred_element_type=F32)
    return dot(mat_b, hi) + dot(mat_b, mid) + dot(mat_b, lo)


def _mm_exact_lhs(y, mat_b):
    hi = y.astype(BF16)
    r1 = y - hi.astype(F32)
    mid = r1.astype(BF16)
    lo = (r1 - mid.astype(F32)).astype(BF16)
    dot = functools.partial(jnp.dot, preferred_element_type=F32)
    return dot(hi, mat_b) + dot(mid, mat_b) + dot(lo, mat_b)


def _sigmoid(x):
    return 1.0 / (1.0 + jnp.exp(-x))


def _silu(x):
    return x * _sigmoid(x)


def _softplus(x):
    return jnp.maximum(x, 0.0) + jnp.log(1.0 + jnp.exp(-jnp.abs(x)))


def _ones_mask(mask):
    return jnp.where(mask, 1.0, 0.0).astype(BF16)


def _params(n_axes):
    return pltpu.CompilerParams(dimension_semantics=("arbitrary",) * n_axes,
                                vmem_limit_bytes=VMEM_LIMIT)


def _ada_kernel(cp_ref, cs_ref, w_ref, b_ref, op_ref, os_ref):
    w = w_ref[...].astype(BF16)
    b = b_ref[...]
    op_ref[0] = jnp.dot(_silu(cp_ref[...]).astype(BF16), w, preferred_element_type=F32) + b
    os_ref[0] = jnp.dot(_silu(cs_ref[...]).astype(BF16), w, preferred_element_type=F32) + b


def _ada_mod(c_prompt8, c_sample, w_ada, b_ada):
    nb = c_sample.shape[0]
    return pl.pallas_call(
        _ada_kernel,
        grid=(3,),
        in_specs=[pl.BlockSpec((SUBLANES, D_MODEL), lambda k: (0, 0)),
                  pl.BlockSpec((nb, D_MODEL), lambda k: (0, 0)),
                  pl.BlockSpec((D_MODEL, D_MODEL), lambda k: (0, k)),
                  pl.BlockSpec((1, D_MODEL), lambda k: (0, k))],
        out_specs=[pl.BlockSpec((1, SUBLANES, D_MODEL), lambda k: (k, 0, 0)),
                   pl.BlockSpec((1, nb, D_MODEL), lambda k: (k, 0, 0))],
        out_shape=[jax.ShapeDtypeStruct((3, SUBLANES, D_MODEL), F32),
                   jax.ShapeDtypeStruct((3, nb, D_MODEL), F32)],
        compiler_params=_params(1),
        name="ada_mod",
    )(c_prompt8, c_sample, w_ada, b_ada)


def _inproj_kernel(x_ref, sh_ref, sc_ref, ng_ref, wb_ref, ws_ref, qg_ref, kg_ref,
                   qkv_ref, gz_ref, sm_ref, fq_ref, fk_ref, fkb_ref, fv_ref, fvb_ref, fzg_ref):
    x = x_ref[...]
    y = x * lax.rsqrt(jnp.mean(x * x, axis=-1, keepdims=True) + NORM_EPS)
    hdn = (y * ng_ref[...]) * (1.0 + sc_ref[0, 0]) + sh_ref[0, 0]
    hb = hdn.astype(BF16)
    dot = functools.partial(jnp.dot, preferred_element_type=F32)
    qkv_ref[...] = dot(hb, wb_ref[:, COL_GQKV:COL_GZ])
    gz_ref[...] = dot(hb, wb_ref[:, COL_GZ:COL_FQ])
    sm_ref[...] = dot(hb, ws_ref[...])
    fq = dot(hb, wb_ref[:, COL_FQ:COL_FK])
    fk = dot(hb, wb_ref[:, COL_FK:COL_FV])
    fv = dot(hb, wb_ref[:, COL_FV:COL_FZ])
    fz = dot(hb, wb_ref[:, COL_FZ:BIG_COLS])
    fvb_ref[...] = fv.astype(BF16)
    fzg_ref[...] = _silu(fz).astype(BF16)
    scale = HEAD_DIM ** -0.5
    tm = x.shape[0]
    for h in range(HEADS):
        sl = slice(h * HEAD_DIM, (h + 1) * HEAD_DIM)
        qh = fq[:, sl]
        qn = (qh * lax.rsqrt(jnp.mean(qh * qh, axis=-1, keepdims=True) + NORM_EPS)) * qg_ref[...]
        fq_ref[:, sl] = (qn * scale).astype(BF16)
        kh = fk[:, sl]
        kn = (kh * lax.rsqrt(jnp.mean(kh * kh, axis=-1, keepdims=True) + NORM_EPS)) * kg_ref[...]
        fkb_ref[:, sl] = kn.astype(BF16)
        fk_ref[pl.ds(h, tm, stride=HEADS), :] = kn
        fv_ref[pl.ds(h, tm, stride=HEADS), :] = fv[:, sl]


def _inproj(x2d, mod4, per_row_mod, tm, norm_g, w_big, w_small, q_norm_g, k_norm_g):
    rows = x2d.shape[0]
    nblk = rows // tm
    if per_row_mod:
        mod_block = (1, 1, tm, D_MODEL)
        sh_map = lambda i: (0, 0, i, 0)
        sc_map = lambda i: (1, 0, i, 0)
    else:
        blocks_per_batch = nblk // mod4.shape[1]
        mod_block = (1, 1, 1, D_MODEL)
        sh_map = lambda i: (0, i // blocks_per_batch, 0, 0)
        sc_map = lambda i: (1, i // blocks_per_batch, 0, 0)
    row_spec = lambda n: pl.BlockSpec((tm, n), lambda i: (i, 0))
    const = lambda shape: pl.BlockSpec(shape, lambda i: (0,) * len(shape))
    outs = [(CONV_DIM, F32, 1), (WIDTH, F32, 1), (LANES, F32, 1), (WIDTH, BF16, 1), (HEAD_DIM, F32, HEADS),
            (WIDTH, BF16, 1), (HEAD_DIM, F32, HEADS), (WIDTH, BF16, 1), (WIDTH, BF16, 1)]
    return pl.pallas_call(
        _inproj_kernel,
        grid=(nblk,),
        in_specs=[row_spec(D_MODEL),
                  pl.BlockSpec(mod_block, sh_map), pl.BlockSpec(mod_block, sc_map),
                  const((1, D_MODEL)), const((D_MODEL, BIG_COLS)), const((D_MODEL, LANES)),
                  const((1, HEAD_DIM)), const((1, HEAD_DIM))],
        out_specs=[pl.BlockSpec((tm * r, n), lambda i: (i, 0)) for n, _, r in outs],
        out_shape=[jax.ShapeDtypeStruct((rows * r, n), dt) for n, dt, r in outs],
        compiler_params=_params(1),
        name="in_proj",
    )(x2d, mod4, mod4, norm_g, w_big, w_small, q_norm_g, k_norm_g)


def _token_scalars(x, par):
    beta = _sigmoid(x)
    g = -jnp.exp(par[0:1, :]) * _softplus(x + par[1:2, :])
    lf = -_softplus(-(x + par[2:3, :]))
    return beta, g, lf


def _small_kernel(sm_ref, par_ref, col_ref, row_ref, carry_ref):
    tb = SMALL_TB

    @pl.when(pl.program_id(1) == 0)
    def _():
        carry_ref[...] = jnp.zeros_like(carry_ref)

    x = sm_ref[...]
    beta, g, lf = _token_scalars(x, par_ref[...])
    lane = lax.broadcasted_iota(jnp.int32, (tb, LANES), 1)
    is_b = lane < LANE_G
    is_g = (lane >= LANE_G) & (lane < LANE_F)
    is_f = (lane >= LANE_F) & (lane < LANE_LF)
    gm = jnp.where(is_g, g, 0.0)
    fm = jnp.where(is_f, lf, 0.0)
    r = lax.broadcasted_iota(jnp.int32, (tb, tb), 0)
    c = lax.broadcasted_iota(jnp.int32, (tb, tb), 1)
    tri = c <= r
    same_chunk = jnp.right_shift(r, 6) == jnp.right_shift(c, 6)
    gcum = _mm_exact_rhs(_ones_mask(tri & same_chunk), gm)
    cum = _mm_exact_rhs(_ones_mask(tri), fm) + carry_ref[...]
    cum = jnp.where(is_f, cum, 0.0)
    carry_ref[...] = cum[tb - 1:tb, :]
    col = jnp.where(is_b, beta, 0.0) + gcum + cum + pltpu.roll(fm, LANE_LF - LANE_F, axis=1)
    col_ref[...] = col
    for i in range(tb // LANES):
        t = col[i * LANES:(i + 1) * LANES, :]
        row_ref[0, :, i * LANES:(i + 1) * LANES] = t.T[0:2 * SUBLANES, :]


def _small_prep(small, par, batch, seq):
    nt = seq // SMALL_TB
    return pl.pallas_call(
        _small_kernel,
        grid=(batch, nt),
        in_specs=[pl.BlockSpec((SMALL_TB, LANES), lambda b, t: (b * nt + t, 0)),
                  pl.BlockSpec((SUBLANES, LANES), lambda b, t: (0, 0))],
        out_specs=[pl.BlockSpec((SMALL_TB, LANES), lambda b, t: (b * nt + t, 0)),
                   pl.BlockSpec((1, 2 * SUBLANES, SMALL_TB), lambda b, t: (b, 0, t))],
        out_shape=[jax.ShapeDtypeStruct((batch * seq, LANES), F32),
                   jax.ShapeDtypeStruct((batch, 2 * SUBLANES, seq), F32)],
        scratch_shapes=[pltpu.VMEM((1, LANES), F32)],
        compiler_params=_params(2),
        name="token_scalars",
    )(small, par)


def _unit_lower_inverses(ms, eye, ri, ci):
    base = 3
    same = jnp.right_shift(ri, base) == jnp.right_shift(ci, base)
    xps = [-jnp.where(same, m, 0.0) for m in ms]
    ps = [eye + xp for xp in xps]
    for _ in range(base - 1):
        xps = [_mm(xp, xp) for xp in xps]
        ps = [p + _mm(p, xp) for p, xp in zip(ps, xps)]
    size = base
    while (1 << size) < CHUNK:
        off = ((jnp.right_shift(ri, size + 1) == jnp.right_shift(ci, size + 1))
               & (jnp.right_shift(ri, size) != jnp.right_shift(ci, size)))
        pcs = [_mm(p, jnp.where(off, m, 0.0)) for p, m in zip(ps, ms)]
        ps = [p - _mm(pc, p) for p, pc in zip(ps, pcs)]
        size += 1
    return ps


def _gdn_chunk_preps(qs, ks, vs, betas, gcs, grs, incl, strict, eye, ri, ci):
    decays = [jnp.where(incl, jnp.exp(jnp.where(incl, gc - gr, 0.0)), 0.0) for gc, gr in zip(gcs, grs)]
    kbs = [k * beta for k, beta in zip(ks, betas)]
    boths = [_mm_nt(jnp.concatenate([kb, q], axis=0), k) for kb, q, k in zip(kbs, qs, ks)]
    ms = [jnp.where(strict, both[0:CHUNK] * decay, 0.0) for both, decay in zip(boths, decays)]
    a_intras = [both[CHUNK:2 * CHUNK] * decay for both, decay in zip(boths, decays)]
    tinvs = _unit_lower_inverses(ms, eye, ri, ci)
    egs = [jnp.exp(gc) for gc in gcs]
    uws = [_mm(tinv, jnp.concatenate([v * beta, kb * eg], axis=1))
           for tinv, v, beta, kb, eg in zip(tinvs, vs, betas, kbs, egs)]
    preps = []
    for uw, q, k, eg, gc, a_intra in zip(uws, qs, ks, egs, gcs, a_intras):
        g_last = gc[CHUNK - 1:CHUNK, :]
        wq = jnp.concatenate([uw[:, HEAD_DIM:2 * HEAD_DIM], q * eg], axis=0)
        preps.append((uw[:, 0:HEAD_DIM], wq, a_intra, k * jnp.exp(g_last - gc), jnp.exp(g_last)))
    return preps


def _gdn_chunk_steps(preps, states):
    boths = [_mm(p[1], s) for p, s in zip(preps, states)]
    v_news = [p[0] - both[0:CHUNK] for p, both in zip(preps, boths)]
    outs = [both[CHUNK:2 * CHUNK] + _mm(p[2], v_new) for p, both, v_new in zip(preps, boths, v_news)]
    new_states = [s * p[4] + _mm_tn(p[3], v_new) for p, s, v_new in zip(preps, states, v_news)]
    return outs, new_states


def _gdn_kernel(qkv_ref, gz_ref, col_ref, row_ref, cw_ref, gn_ref,
                o_ref, ssm_ref, conv_ref, s_ref, xbuf_ref, qn_ref, kn_ref, vn_ref):
    tb = GDN_TB
    t = pl.program_id(1)
    halo = SUBLANES

    @pl.when(t == 0)
    def _():
        s_ref[...] = jnp.zeros_like(s_ref)
        xbuf_ref[0:halo, :] = jnp.zeros((halo, CONV_DIM), F32)

    xbuf_ref[halo:halo + tb, :] = qkv_ref[...]
    scale = HEAD_DIM ** -0.5
    for cb in range(CONV_DIM // LANES):
        sl = slice(cb * LANES, (cb + 1) * LANES)
        acc = xbuf_ref[halo - 3:halo - 3 + tb, sl] * cw_ref[0:1, sl]
        for i in range(1, CONV_W):
            acc = acc + xbuf_ref[halo - 3 + i:halo - 3 + i + tb, sl] * cw_ref[i:i + 1, sl]
        act = _silu(acc)
        part, h = divmod(cb, HEADS)
        hs = slice(h * LANES, (h + 1) * LANES)
        if part == 0:
            qn_ref[:, hs] = act * lax.rsqrt(jnp.sum(act * act, axis=-1, keepdims=True) + NORM_EPS) * scale
        elif part == 1:
            kn_ref[:, hs] = act * lax.rsqrt(jnp.sum(act * act, axis=-1, keepdims=True) + NORM_EPS)
        else:
            vn_ref[:, hs] = act
    tail = xbuf_ref[tb:tb + halo, :]
    xbuf_ref[0:halo, :] = tail

    ri = lax.broadcasted_iota(jnp.int32, (CHUNK, CHUNK), 0)
    ci = lax.broadcasted_iota(jnp.int32, (CHUNK, CHUNK), 1)
    incl = ci <= ri
    strict = ci < ri
    eye = jnp.where(ci == ri, 1.0, 0.0)

    def group_body(i, carry):
        base = pl.multiple_of(i * GDN_GROUP, GDN_GROUP)
        colt = col_ref[pl.ds(base, GDN_GROUP), :]
        rowt = row_ref[0, :, pl.ds(base, GDN_GROUP)]
        n_cc = GDN_GROUP // CHUNK
        items = [(cc, h) for cc in range(n_cc) for h in range(HEADS)]
        rows = lambda cc: pl.ds(pl.multiple_of(base + cc * CHUNK, CHUNK), CHUNK)
        rs = lambda cc: slice(cc * CHUNK, (cc + 1) * CHUNK)
        hs = lambda h: slice(h * HEAD_DIM, (h + 1) * HEAD_DIM)
        preps = _gdn_chunk_preps(
            [qn_ref[rows(cc), hs(h)] for cc, h in items],
            [kn_ref[rows(cc), hs(h)] for cc, h in items],
            [vn_ref[rows(cc), hs(h)] for cc, h in items],
            [colt[rs(cc), LANE_B + h:LANE_B + h + 1] for cc, h in items],
            [colt[rs(cc), LANE_G + h:LANE_G + h + 1] for cc, h in items],
            [rowt[LANE_G + h:LANE_G + h + 1, rs(cc)] for cc, h in items],
            incl, strict, eye, ri, ci)
        states = [s_ref[h] for h in range(HEADS)]
        for cc in range(n_cc):
            outs, states = _gdn_chunk_steps(preps[cc * HEADS:(cc + 1) * HEADS], states)
            for h, o in enumerate(outs):
                on = (o * lax.rsqrt(jnp.mean(o * o, axis=-1, keepdims=True) + NORM_EPS)) * gn_ref[...]
                o_ref[rows(cc), hs(h)] = (on * _silu(gz_ref[rows(cc), hs(h)])).astype(o_ref.dtype)
        for h in range(HEADS):
            s_ref[h] = states[h]
        return carry

    lax.fori_loop(0, tb // GDN_GROUP, group_body, 0)

    @pl.when(t == pl.num_programs(1) - 1)
    def _():
        ssm_ref[0] = s_ref[...]
        conv_ref[0] = tail[halo - (CONV_W - 1):halo, :]


def _gdn_prompt(qkv, gz, col, row, conv_w, gdn_norm_g, batch, seq):
    nt = seq // GDN_TB
    rows = batch * seq
    row_spec = lambda n: pl.BlockSpec((GDN_TB, n), lambda b, t: (b * nt + t, 0))
    return pl.pallas_call(
        _gdn_kernel,
        grid=(batch, nt),
        in_specs=[row_spec(CONV_DIM), row_spec(WIDTH), row_spec(LANES),
                  pl.BlockSpec((1, 2 * SUBLANES, GDN_TB), lambda b, t: (b, 0, t)),
                  pl.BlockSpec((CONV_W, CONV_DIM), lambda b, t: (0, 0)),
                  pl.BlockSpec((1, HEAD_DIM), lambda b, t: (0, 0))],
        out_specs=[row_spec(WIDTH),
                   pl.BlockSpec((1, HEADS, HEAD_DIM, HEAD_DIM), lambda b, t: (b, 0, 0, 0)),
                   pl.BlockSpec((1, CONV_W - 1, CONV_DIM), lambda b, t: (b, 0, 0))],
        out_shape=[jax.ShapeDtypeStruct((rows, WIDTH), BF16),
                   jax.ShapeDtypeStruct((batch, HEADS, HEAD_DIM, HEAD_DIM), F32),
                   jax.ShapeDtypeStruct((batch, CONV_W - 1, CONV_DIM), F32)],
        scratch_shapes=[pltpu.VMEM((HEADS, HEAD_DIM, HEAD_DIM), F32),
                        pltpu.VMEM((GDN_TB + SUBLANES, CONV_DIM), F32),
                        pltpu.VMEM((GDN_TB, WIDTH), F32),
                        pltpu.VMEM((GDN_TB, WIDTH), F32),
                        pltpu.VMEM((GDN_TB, WIDTH), F32)],
        compiler_params=_params(2),
        name="gdn_prompt",
    )(qkv, gz, col, row, conv_w, gdn_norm_g)


def _att_kernel(qi_ref, kj_ref, q_ref, k_ref, v_ref, ck_ref, cq_ref, gate_ref, o_ref,
                m_ref, l_ref, acc_ref, cqb_ref):
    step = pl.program_id(1)
    qi = qi_ref[step]
    kj = kj_ref[step]
    tq = tk = ATT_T
    heads = range(HEADS)
    hs = lambda h: slice(h * HEAD_DIM, (h + 1) * HEAD_DIM)

    @pl.when(kj == 0)
    def _():
        m_ref[...] = jnp.full_like(m_ref, NEG_BIG)
        l_ref[...] = jnp.zeros_like(l_ref)
        acc_ref[...] = jnp.zeros_like(acc_ref)
        for h in heads:
            cqb_ref[h] = jnp.broadcast_to(cq_ref[:, LANE_F + h:LANE_F + h + 1], (tq, LANES))

    def update(diagonal):
        ss = [lax.dot_general(q_ref[:, hs(h)], k_ref[:, hs(h)], (((1,), (1,)), ((), ())),
                              preferred_element_type=F32) for h in heads]
        tts = [s - ck_ref[0, LANE_F + h:LANE_F + h + 1, :] for h, s in zip(heads, ss)]
        if diagonal:
            visible = (lax.broadcasted_iota(jnp.int32, (tq, tk), 1)
                       <= lax.broadcasted_iota(jnp.int32, (tq, tk), 0))
            tts = [jnp.where(visible, tt, NEG_BIG) for tt in tts]
        cqs = [cqb_ref[h] for h in heads]
        m_olds = [m_ref[h] for h in heads]
        m_news = [jnp.maximum(m_old, jnp.broadcast_to(jnp.max(tt, axis=1, keepdims=True), (tq, LANES)) + cq)
                  for m_old, tt, cq in zip(m_olds, tts, cqs)]
        shifts = [jnp.concatenate([m_new - cq] * (tk // LANES), axis=1) for m_new, cq in zip(m_news, cqs)]
        ps = [jnp.exp(tt - shift).astype(BF16) for tt, shift in zip(tts, shifts)]
        ones = jnp.ones((tk, HEAD_DIM), BF16)
        pvs = [jnp.dot(p, jnp.concatenate([v_ref[:, hs(h)], ones], axis=1), preferred_element_type=F32)
               for h, p in zip(heads, ps)]
        alphas = [jnp.exp(m_old - m_new) for m_old, m_new in zip(m_olds, m_news)]
        accs = [alpha * acc_ref[h] + pv[:, 0:HEAD_DIM] for h, alpha, pv in zip(heads, alphas, pvs)]
        ls = [alpha * l_ref[h] + pv[:, HEAD_DIM:2 * HEAD_DIM] for h, alpha, pv in zip(heads, alphas, pvs)]
        return m_news, ls, accs

    @pl.when(kj < qi)
    def _():
        m_news, ls, accs = update(False)
        for h in heads:
            m_ref[h] = m_news[h]
            l_ref[h] = ls[h]
            acc_ref[h] = accs[h]

    @pl.when(kj == qi)
    def _():
        _, ls, accs = update(True)
        for h in heads:
            o_ref[:, hs(h)] = ((accs[h] / ls[h]) * gate_ref[:, hs(h)].astype(F32)).astype(o_ref.dtype)


def _att_prompt(q, k, v, row, col, gate, batch, seq):
    n = seq // ATT_T
    rows = batch * seq
    pairs = [(i, j) for i in range(n) for j in range(i + 1)]
    qi_tab = jnp.asarray([i for i, _ in pairs], jnp.int32)
    kj_tab = jnp.asarray([j for _, j in pairs], jnp.int32)
    qmap = lambda b, s, qi, kj: (b * n + qi[s], 0)
    kmap = lambda b, s, qi, kj: (b * n + kj[s], 0)
    grid_spec = pltpu.PrefetchScalarGridSpec(
        num_scalar_prefetch=2,
        grid=(batch, len(pairs)),
        in_specs=[pl.BlockSpec((ATT_T, WIDTH), qmap),
                  pl.BlockSpec((ATT_T, WIDTH), kmap),
                  pl.BlockSpec((ATT_T, WIDTH), kmap),
                  pl.BlockSpec((1, 2 * SUBLANES, ATT_T), lambda b, s, qi, kj: (b, 0, kj[s])),
                  pl.BlockSpec((ATT_T, LANES), qmap),
                  pl.BlockSpec((ATT_T, WIDTH), qmap)],
        out_specs=pl.BlockSpec((ATT_T, WIDTH), qmap),
        scratch_shapes=[pltpu.VMEM((HEADS, ATT_T, LANES), F32)] * 4)
    return pl.pallas_call(
        _att_kernel,
        grid_spec=grid_spec,
        out_shape=jax.ShapeDtypeStruct((rows, WIDTH), BF16),
        compiler_params=_params(2),
        name="fox_prompt",
    )(qi_tab, kj_tab, q, k, v, row, col, gate)


def _outproj_kernel(x_ref, oa_ref, ob_ref, gate_ref, w_ref, y_ref):
    mix = (jnp.dot(oa_ref[...].astype(BF16), w_ref[0:WIDTH, :], preferred_element_type=F32)
           + jnp.dot(ob_ref[...].astype(BF16), w_ref[WIDTH:2 * WIDTH, :], preferred_element_type=F32))
    y_ref[...] = x_ref[...] + gate_ref[0, 0] * mix


def _outproj(x2d, oa, ob, mod4, per_row_mod, tm, w_out_b):
    rows = x2d.shape[0]
    nblk = rows // tm
    if per_row_mod:
        mod_block = (1, 1, tm, D_MODEL)
        g_map = lambda i: (2, 0, i, 0)
    else:
        blocks_per_batch = nblk // mod4.shape[1]
        mod_block = (1, 1, 1, D_MODEL)
        g_map = lambda i: (2, i // blocks_per_batch, 0, 0)
    row_spec = lambda n: pl.BlockSpec((tm, n), lambda i: (i, 0))
    return pl.pallas_call(
        _outproj_kernel,
        grid=(nblk,),
        in_specs=[row_spec(D_MODEL), row_spec(WIDTH), row_spec(WIDTH),
                  pl.BlockSpec(mod_block, g_map),
                  pl.BlockSpec((2 * WIDTH, D_MODEL), lambda i: (0, 0))],
        out_specs=row_spec(D_MODEL),
        out_shape=jax.ShapeDtypeStruct((rows, D_MODEL), F32),
        compiler_params=_params(1),
        name="out_proj",
    )(x2d, oa, ob, mod4, w_out_b)


def _smid_kernel(u_ref, s0_ref, s1_ref, s2_ref, cw_ref, sm_ref, par_ref,
                 qn_ref, kn_ref, vn_ref, sc_ref):
    conv = (s0_ref[...] * cw_ref[0:1, :] + s1_ref[...] * cw_ref[1:2, :]
            + s2_ref[...] * cw_ref[2:3, :] + u_ref[...] * cw_ref[3:4, :])
    act = _silu(conv)
    for h in range(HEADS):
        hs = slice(h * HEAD_DIM, (h + 1) * HEAD_DIM)
        q = act[:, h * HEAD_DIM:(h + 1) * HEAD_DIM]
        k = act[:, WIDTH + h * HEAD_DIM:WIDTH + (h + 1) * HEAD_DIM]
        qn_ref[:, hs] = q * lax.rsqrt(jnp.sum(q * q, axis=-1, keepdims=True) + NORM_EPS)
        kn_ref[:, hs] = k * lax.rsqrt(jnp.sum(k * k, axis=-1, keepdims=True) + NORM_EPS)
    vn_ref[...] = act[:, 2 * WIDTH:3 * WIDTH]
    beta, g, lf = _token_scalars(sm_ref[...], par_ref[...])
    lane = lax.broadcasted_iota(jnp.int32, sm_ref.shape, 1)
    sc_ref[...] = jnp.where(lane < LANE_G, beta, jnp.where(lane < LANE_F, g, lf))


def _sample_mid(u, s0, s1, s2, conv_w, small, par):
    nb = u.shape[0]
    full = lambda a: pl.BlockSpec(a.shape, lambda: (0,) * a.ndim)
    args = (u, s0, s1, s2, conv_w, small, par)
    return pl.pallas_call(
        _smid_kernel,
        in_specs=[full(a) for a in args],
        out_specs=[pl.BlockSpec((nb, WIDTH), lambda: (0, 0))] * 3 + [pl.BlockSpec((nb, LANES), lambda: (0, 0))],
        out_shape=[jax.ShapeDtypeStruct((nb, WIDTH), F32)] * 3 + [jax.ShapeDtypeStruct((nb, LANES), F32)],
        compiler_params=pltpu.CompilerParams(vmem_limit_bytes=VMEM_LIMIT),
        name="sample_conv",
    )(*args)


def _gdn_step_kernel(t_ref, s_ref, gz_ref, gn_ref, o_ref, sn_ref):
    scale = HEAD_DIM ** -0.5
    row = lax.broadcasted_iota(jnp.int32, (SUBLANES, HEAD_DIM), 0)
    items = [(b, h) for b in range(STEP_BATCH) for h in range(HEADS)]
    ts = [t_ref[b, h] for b, h in items]
    rs = [_mm(t, s_ref[b, h]) for t, (b, h) in zip(ts, items)]
    egs, v_news = [], []
    for t, r, (b, h) in zip(ts, rs, items):
        k, q, v, beta, g = (t[i:i + 1, :] for i in range(5))
        eg = jnp.exp(g)
        v_new = beta * (v - eg * r[0:1, :])
        o = scale * (eg * r[1:2, :] + jnp.sum(q * k, axis=-1, keepdims=True) * v_new)
        on = (o * lax.rsqrt(jnp.mean(o * o, axis=-1, keepdims=True) + NORM_EPS)) * gn_ref[...]
        o_ref[b, h:h + 1, :] = on * _silu(gz_ref[b, h:h + 1, :])
        egs.append(eg)
        v_news.append(v_new)
    outers = [_mm_tn(jnp.where(row == 0, t, 0.0),
                     jnp.where(row == 0, jnp.broadcast_to(v_new, (SUBLANES, HEAD_DIM)), 0.0))
              for t, v_new in zip(ts, v_news)]
    for eg, outer, (b, h) in zip(egs, outers, items):
        sn_ref[b, h] = s_ref[b, h] * eg + outer


def _gdn_step(t8, state, gz3, gdn_norm_g):
    nb = state.shape[0]
    sb = STEP_BATCH
    return pl.pallas_call(
        _gdn_step_kernel,
        grid=(nb // sb,),
        in_specs=[pl.BlockSpec((sb, HEADS, SUBLANES, HEAD_DIM), lambda b: (b, 0, 0, 0)),
                  pl.BlockSpec((sb, HEADS, HEAD_DIM, HEAD_DIM), lambda b: (b, 0, 0, 0)),
                  pl.BlockSpec((sb, HEADS, HEAD_DIM), lambda b: (b, 0, 0)),
                  pl.BlockSpec((1, HEAD_DIM), lambda b: (0, 0))],
        out_specs=[pl.BlockSpec((sb, HEADS, HEAD_DIM), lambda b: (b, 0, 0)),
                   pl.BlockSpec((sb, HEADS, HEAD_DIM, HEAD_DIM), lambda b: (b, 0, 0, 0))],
        out_shape=[jax.ShapeDtypeStruct((nb, HEADS, HEAD_DIM), F32),
                   jax.ShapeDtypeStruct((nb, HEADS, HEAD_DIM, HEAD_DIM), F32)],
        compiler_params=_params(1),
        name="gdn_step",
    )(t8, state, gz3, gdn_norm_g)


def _class_reduce(x, op):
    shift = HEADS
    while shift < LANES:
        x = op(x, pltpu.roll(x, shift, axis=1))
        shift *= 2
    return x


def _lane_to_col(v, eye8):
    return jnp.sum(jnp.where(eye8, jnp.broadcast_to(v, (SUBLANES, LANES)), 0.0), axis=1, keepdims=True)


def _dec_kernel(pt_ref, q8_ref, kn_ref, vn_ref, gate_ref, lfn_ref, *refs):
    del pt_ref
    gp = DEC_PAGES
    k_refs, v_refs, lf_refs = refs[0:gp], refs[gp:2 * gp], refs[2 * gp:3 * gp]
    o_ref = refs[3 * gp]
    qrep_ref, m_ref, l_ref, acc_ref, carry_ref, sd_ref, lf_ref = refs[3 * gp + 1:]
    g = pl.program_id(1)
    nrow = gp * HEADS

    @pl.when(g == 0)
    def _():
        qrep_ref[...] = jnp.concatenate([q8_ref[0]] * (LANES // Q_ROWS), axis=0)
        m_ref[...] = jnp.full_like(m_ref, NEG_BIG)
        l_ref[...] = jnp.zeros_like(l_ref)
        acc_ref[...] = jnp.zeros_like(acc_ref)
        carry_ref[...] = lfn_ref[0]

    ri = lax.broadcasted_iota(jnp.int32, (LANES, LANES), 0)
    ci = lax.broadcasted_iota(jnp.int32, (LANES, LANES), 1)
    eye = ri == ci
    same_head = (ri & (HEADS - 1)) == (ci & (HEADS - 1))
    eye8 = (lax.broadcasted_iota(jnp.int32, (SUBLANES, LANES), 0)
            == lax.broadcasted_iota(jnp.int32, (SUBLANES, LANES), 1))

    for i in range(gp):
        lf_ref[i * HEADS:(i + 1) * HEADS, :] = lf_refs[gp - 1 - i][0]
    lf = lf_ref[...]
    within = _mm_exact_lhs(lf, _ones_mask(same_head & (ri > ci)))
    tot = _mm_exact_lhs(lf, _ones_mask(same_head))
    si = lax.broadcasted_iota(jnp.int32, (nrow, nrow), 0)
    sj = lax.broadcasted_iota(jnp.int32, (nrow, nrow), 1)
    later = _mm_exact_rhs(_ones_mask(sj > si), tot)
    bias = within + later + carry_ref[...]
    carry_ref[...] = carry_ref[...] + jnp.sum(tot, axis=0, keepdims=True)

    qrep = qrep_ref[...]
    sig_of = lambda r, a: (gp - 1 - r) * HEADS + a
    for r in range(gp):
        s2 = _mm_nt(k_refs[r][...], qrep)
        for a in range(HEADS):
            blk = s2[a * LANES:(a + 1) * LANES, :]
            sig = sig_of(r, a)
            sd_ref[sig:sig + 1, :] = jnp.sum(jnp.where(eye, blk, 0.0), axis=0, keepdims=True)

    ones = jnp.ones((LANES, LANES), BF16)
    m_run, l_run, acc = m_ref[...], l_ref[...], acc_ref[...]
    for sub in range(gp // DEC_SUB):
        pages = range(sub * DEC_SUB, (sub + 1) * DEC_SUB)
        lo = sig_of(pages[-1], 0)
        rows = slice(lo, lo + DEC_SUB * HEADS)
        tt = sd_ref[rows, :] + bias[rows, :]
        mx = jnp.max(tt, axis=0, keepdims=True)
        mx = _class_reduce(jnp.broadcast_to(mx, (SUBLANES, LANES)), jnp.maximum)[0:1, :]
        m_new = jnp.maximum(m_run, mx)
        pf = jnp.exp(tt - m_new).astype(BF16).astype(F32)
        alpha = jnp.exp(m_run - m_new)
        l_run = alpha * l_run + jnp.sum(pf, axis=0, keepdims=True)
        m_run = m_new
        acc = acc * _lane_to_col(alpha, eye8)
        for r in pages:
            parts = []
            for a in range(HEADS):
                sig = sig_of(r, a) - lo
                prow = jnp.broadcast_to(pf[sig:sig + 1, :], (LANES, LANES))
                parts.append(jnp.where(eye, prow, 0.0).astype(BF16))
            pcol = jnp.dot(jnp.concatenate(parts, axis=0), ones, preferred_element_type=F32)
            pv = pcol * v_refs[r][...]
            acc = acc + jnp.sum(pv.reshape(PAGE_ROWS // SUBLANES, SUBLANES, LANES), axis=0)
    m_new = m_run
    m_ref[...] = m_run
    l_ref[...] = l_run
    acc_ref[...] = acc

    @pl.when(g == pl.num_programs(1) - 1)
    def _():
        acc4 = acc[0:HEADS, :] + acc[HEADS:2 * HEADS, :]
        mcol = _lane_to_col(m_new, eye8)[0:HEADS, :]
        lsum = _class_reduce(jnp.broadcast_to(l_ref[...], (SUBLANES, LANES)), jnp.add)[0:1, :]
        lcol = _lane_to_col(lsum, eye8)[0:HEADS, :]
        q4 = q8_ref[0][0:HEADS, :].astype(F32)
        s_new = jnp.sum(q4 * kn_ref[0], axis=-1, keepdims=True)
        m_f = jnp.maximum(mcol, s_new)
        a1 = jnp.exp(mcol - m_f)
        a2 = jnp.exp(s_new - m_f)
        o = (acc4 * a1 + a2 * vn_ref[0]) / (lcol * a1 + a2)
        o_ref[0] = o * gate_ref[0].astype(F32)


def _dec_attention(page_table, q8, k_new, v_new, gate, lf_new, cache_k2, cache_v2, cache_lf3):
    nb, n_pages = page_table.shape
    gp = DEC_PAGES
    ng = n_pages // gp

    def page(r):
        return lambda b, g, pt: (pt[b, n_pages - 1 - (g * gp + r)], 0)

    def page3(r):
        return lambda b, g, pt: (pt[b, n_pages - 1 - (g * gp + r)], 0, 0)

    per_b = lambda b, g, pt: (b, 0, 0)
    in_specs = ([pl.BlockSpec((1, Q_ROWS, HEAD_DIM), per_b),
                 pl.BlockSpec((1, HEADS, HEAD_DIM), per_b),
                 pl.BlockSpec((1, HEADS, HEAD_DIM), per_b),
                 pl.BlockSpec((1, HEADS, HEAD_DIM), per_b),
                 pl.BlockSpec((1, 1, LANES), per_b)]
                + [pl.BlockSpec((PAGE_ROWS, HEAD_DIM), page(r)) for r in range(gp)]
                + [pl.BlockSpec((PAGE_ROWS, HEAD_DIM), page(r)) for r in range(gp)]
                + [pl.BlockSpec((1, HEADS, LANES), page3(r)) for r in range(gp)])
    grid_spec = pltpu.PrefetchScalarGridSpec(
        num_scalar_prefetch=1,
        grid=(nb, ng),
        in_specs=in_specs,
        out_specs=pl.BlockSpec((1, HEADS, HEAD_DIM), per_b),
        scratch_shapes=[pltpu.VMEM((LANES, HEAD_DIM), BF16),
                        pltpu.VMEM((1, LANES), F32),
                        pltpu.VMEM((1, LANES), F32),
                        pltpu.VMEM((SUBLANES, HEAD_DIM), F32),
                        pltpu.VMEM((1, LANES), F32),
                        pltpu.VMEM((gp * HEADS, LANES), F32),
                        pltpu.VMEM((gp * HEADS, LANES), F32)])
    return pl.pallas_call(
        _dec_kernel,
        grid_spec=grid_spec,
        out_shape=jax.ShapeDtypeStruct((nb, HEADS, HEAD_DIM), F32),
        compiler_params=pltpu.CompilerParams(dimension_semantics=("arbitrary", "arbitrary"),
                                             vmem_limit_bytes=DEC_VMEM_LIMIT),
        name="fox_decode",
    )(page_table, q8, k_new, v_new, gate, lf_new,
      *([cache_k2] * gp), *([cache_v2] * gp), *([cache_lf3] * gp))


def _layer(x_prompt, x_sample, cache_k, cache_v, cache_logf, state_ssm, state_conv, page_table,
           c_prompt, c_sample, norm_g, w_ada, b_ada, w_in, b_f, conv_w, a_log, dt_bias,
           gdn_norm_g, q_norm_g, k_norm_g, w_out):
    batch, seq, _ = x_prompt.shape
    nb = x_sample.shape[0]
    w = WIDTH
    o_small = 4 * w
    o_fox = o_small + 2 * HEADS
    w_big = jnp.concatenate([w_in[:, 0:o_small], w_in[:, o_fox:o_fox + 4 * w]], axis=1).astype(BF16)
    w_small = jnp.concatenate([w_in[:, o_small:o_fox], w_in[:, o_fox + 4 * w:],
                               jnp.zeros((D_MODEL, LANES - 3 * HEADS), F32)], axis=1).astype(BF16)
    w_out_b = w_out.astype(BF16)
    par = jnp.zeros((SUBLANES, LANES), F32)
    par = par.at[0, LANE_G:LANE_G + HEADS].set(a_log)
    par = par.at[1, LANE_G:LANE_G + HEADS].set(dt_bias)
    par = par.at[2, LANE_F:LANE_F + HEADS].set(b_f)
    ng2, qg2, kg2, gg2 = (a.reshape(1, -1) for a in (norm_g, q_norm_g, k_norm_g, gdn_norm_g))

    cp8 = jnp.pad(c_prompt, ((0, SUBLANES - batch), (0, 0)))
    mod_p, mod_s = _ada_mod(cp8, c_sample, w_ada, b_ada.reshape(1, -1))
    mod_p4 = mod_p[:, 0:batch].reshape(3, batch, 1, D_MODEL)
    mod_s4 = mod_s.reshape(3, 1, nb, D_MODEL)

    xp = x_prompt.reshape(batch * seq, D_MODEL)
    qkv, gz, small, fq, fk, fkb, fv, fvb, fzg = _inproj(xp, mod_p4, False, PROMPT_TM, ng2, w_big, w_small, qg2, kg2)
    col, row = _small_prep(small, par, batch, seq)
    o_a, ssm_p, conv_p = _gdn_prompt(qkv, gz, col, row, conv_w, gg2, batch, seq)
    o_b = _att_prompt(fq, fkb, fvb, row, col, fzg, batch, seq)
    y_p = _outproj(xp, o_a, o_b, mod_p4, False, PROMPT_TM, w_out_b).reshape(batch, seq, D_MODEL)
    k_p = fk.reshape(batch, seq, HEADS, HEAD_DIM)
    v_p = fv.reshape(batch, seq, HEADS, HEAD_DIM)
    lf_p = col[:, LANE_LF:LANE_LF + HEADS].reshape(batch, seq, HEADS)

    xs = x_sample.reshape(nb, D_MODEL)
    qkv_s, gz_s, small_s, fq_s, fk_s, _, fv_s, _, fzg_s = _inproj(xs, mod_s4, True, nb, ng2, w_big, w_small, qg2, kg2)
    qn_s, kn_s, vn_s, sc_s = _sample_mid(qkv_s, state_conv[:, 0], state_conv[:, 1], state_conv[:, 2],
                                         conv_w, small_s, par)
    h3 = lambda a: a.reshape(nb, HEADS, HEAD_DIM)
    bcast = lambda a: jnp.broadcast_to(a[:, :, None], (nb, HEADS, HEAD_DIM))
    t8 = jnp.stack([h3(kn_s), h3(qn_s), h3(vn_s), bcast(sc_s[:, LANE_B:LANE_B + HEADS]),
                    bcast(sc_s[:, LANE_G:LANE_G + HEADS])]
                   + [jnp.zeros((nb, HEADS, HEAD_DIM), F32)] * (SUBLANES - 5), axis=2)
    o_a_s, ssm_s = _gdn_step(t8, state_ssm, h3(gz_s), gg2)
    lf_s = sc_s[:, LANE_F:LANE_F + HEADS]
    q8 = jnp.tile(h3(fq_s), (1, Q_ROWS // HEADS, 1))
    lf_new = jnp.tile(lf_s, (1, LANES // HEADS)).reshape(nb, 1, LANES)
    n_pool = cache_k.shape[0]
    o_b_s = _dec_attention(page_table, q8, h3(fk_s), h3(fv_s), h3(fzg_s), lf_new,
                           cache_k.reshape(n_pool * PAGE_ROWS, HEAD_DIM),
                           cache_v.reshape(n_pool * PAGE_ROWS, HEAD_DIM),
                           cache_logf.reshape(n_pool, HEADS, LANES))
    y_s = _outproj(xs, o_a_s.reshape(nb, w), o_b_s.reshape(nb, w), mod_s4, True, nb, w_out_b).reshape(nb, 1, D_MODEL)
    conv_s = jnp.stack([state_conv[:, 1], state_conv[:, 2], qkv_s], axis=1)
    return (y_p, y_s, k_p, v_p, lf_p, ssm_p, conv_p,
            h3(fk_s).reshape(nb, 1, HEADS, HEAD_DIM), h3(fv_s).reshape(nb, 1, HEADS, HEAD_DIM),
            lf_s.reshape(nb, 1, HEADS), ssm_s, conv_s)


def kernel(x_prompt, x_sample, cache_k, cache_v, cache_logf, state_ssm, state_conv, page_table, c_prompt, c_sample, norm_g, w_ada, b_ada, w_in, b_f, conv_w, a_log, dt_bias, gdn_norm_g, q_norm_g, k_norm_g, w_out):
    assert w_ada.shape[0] == 1, "single layer"
    outs = _layer(x_prompt, x_sample, cache_k[0], cache_v[0], cache_logf[0], state_ssm[0], state_conv[0],
                  page_table, c_prompt, c_sample, norm_g[0], w_ada[0], b_ada[0], w_in[0], b_f[0], conv_w[0],
                  a_log[0], dt_bias[0], gdn_norm_g[0], q_norm_g[0], k_norm_g[0], w_out[0])
    y_p, y_s = outs[0], outs[1]
    return (y_p, y_s) + tuple(o[None] for o in outs[2:])
```

```python
import functools

import jax
import jax.numpy as jnp
from jax import lax
from jax.experimental import pallas as pl
from jax.experimental.pallas import tpu as pltpu

F32 = jnp.float32
BF16 = jnp.bfloat16

D_MODEL = 1024
HEADS = 4
HEAD_DIM = 128
WIDTH = HEADS * HEAD_DIM
CONV_W = 4
CONV_DIM = 3 * WIDTH
CHUNK = 64
PAGE_SIZE = 128
PAGE_ROWS = PAGE_SIZE * HEADS
NORM_EPS = 1e-6
LANES = 128
SUBLANES = 8
NEG_BIG = -1e30
LOG2E = 1.4426950408889634
ATT_Q_SCALE = LOG2E * HEAD_DIM ** -0.5
VMEM_LIMIT = 48 * 1024 * 1024

COL_GQKV = 0
COL_GZ = 3 * WIDTH
COL_FQ = 4 * WIDTH
COL_FK = 5 * WIDTH
COL_FV = 6 * WIDTH
COL_FZ = 7 * WIDTH
BIG_COLS = 8 * WIDTH
LANE_B, LANE_G, LANE_F, LANE_LF = 0, 4, 8, 12

PROMPT_TM = 256
OUT_TM = 512
SMALL_TB = 512
GDN_TB = 512
GDN_GROUP = 256
ATT_T = 512
DEC_PAGES = 32
DEC_SUB = 16
DEC_VMEM_LIMIT = 56 * 1024 * 1024
Q_ROWS = 16
STEP_BATCH = 4


def _mm(a, b):
    return jnp.dot(a.astype(BF16), b.astype(BF16), preferred_element_type=F32)


def _mm_nt(a, b):
    return lax.dot_general(a.astype(BF16), b.astype(BF16), (((1,), (1,)), ((), ())),
                           preferred_element_type=F32)


def _mm_tn(a, b):
    return lax.dot_general(a.astype(BF16), b.astype(BF16), (((0,), (0,)), ((), ())),
                           preferred_element_type=F32)


def _mm_exact_rhs(mat_b, y):
    hi = y.astype(BF16)
    r1 = y - hi.astype(F32)
    mid = r1.astype(BF16)
    lo = (r1 - mid.astype(F32)).astype(BF16)
    dot = functools.partial(jnp.dot, preferred_element_type=F32)
    return dot(mat_b, hi) + dot(mat_b, mid) + dot(mat_b, lo)


def _mm_exact_lhs(y, mat_b):
    hi = y.astype(BF16)
    r1 = y - hi.astype(F32)
    mid = r1.astype(BF16)
    lo = (r1 - mid.astype(F32)).astype(BF16)
    dot = functools.partial(jnp.dot, preferred_element_type=F32)
    return dot(hi, mat_b) + dot(mid, mat_b) + dot(lo, mat_b)


def _sigmoid(x):
    return 1.0 / (1.0 + jnp.exp(-x))


def _silu(x):
    return x * _sigmoid(x)


def _softplus(x):
    return jnp.maximum(x, 0.0) + jnp.log(1.0 + jnp.exp(-jnp.abs(x)))


def _ones_mask(mask):
    return jnp.where(mask, 1.0, 0.0).astype(BF16)


def _params(n_axes):
    return pltpu.CompilerParams(dimension_semantics=("arbitrary",) * n_axes,
                                vmem_limit_bytes=VMEM_LIMIT)


def _ada_kernel(cp_ref, cs_ref, w_ref, b_ref, op_ref, os_ref):
    w = w_ref[...].astype(BF16)
    b = b_ref[...]
    op_ref[0] = jnp.dot(_silu(cp_ref[...]).astype(BF16), w, preferred_element_type=F32) + b
    os_ref[0] = jnp.dot(_silu(cs_ref[...]).astype(BF16), w, preferred_element_type=F32) + b


def _ada_mod(c_prompt8, c_sample, w_ada, b_ada):
    nb = c_sample.shape[0]
    return pl.pallas_call(
        _ada_kernel,
        grid=(3,),
        in_specs=[pl.BlockSpec((SUBLANES, D_MODEL), lambda k: (0, 0)),
                  pl.BlockSpec((nb, D_MODEL), lambda k: (0, 0)),
                  pl.BlockSpec((D_MODEL, D_MODEL), lambda k: (0, k)),
                  pl.BlockSpec((1, D_MODEL), lambda k: (0, k))],
        out_specs=[pl.BlockSpec((1, SUBLANES, D_MODEL), lambda k: (k, 0, 0)),
                   pl.BlockSpec((1, nb, D_MODEL), lambda k: (k, 0, 0))],
        out_shape=[jax.ShapeDtypeStruct((3, SUBLANES, D_MODEL), F32),
                   jax.ShapeDtypeStruct((3, nb, D_MODEL), F32)],
        compiler_params=_params(1),
        name="ada_mod",
    )(c_prompt8, c_sample, w_ada, b_ada)


def _inproj_kernel(x_ref, sh_ref, sc_ref, ng_ref, wb_ref, ws_ref, qg_ref, kg_ref,
                   qkv_ref, gz_ref, sm_ref, fq_ref, fk_ref, fkb_ref, fv_ref, fvb_ref, fzg_ref):
    x = x_ref[...]
    y = x * lax.rsqrt(jnp.mean(x * x, axis=-1, keepdims=True) + NORM_EPS)
    hdn = (y * ng_ref[...]) * (1.0 + sc_ref[0, 0]) + sh_ref[0, 0]
    hb = hdn.astype(BF16)
    dot = functools.partial(jnp.dot, preferred_element_type=F32)
    qkv_ref[...] = dot(hb, wb_ref[:, COL_GQKV:COL_GZ])
    gz_ref[...] = dot(hb, wb_ref[:, COL_GZ:COL_FQ])
    sm_ref[...] = dot(hb, ws_ref[...])
    fq = dot(hb, wb_ref[:, COL_FQ:COL_FK])
    fk = dot(hb, wb_ref[:, COL_FK:COL_FV])
    fv = dot(hb, wb_ref[:, COL_FV:COL_FZ])
    fz = dot(hb, wb_ref[:, COL_FZ:BIG_COLS])
    fvb_ref[...] = fv.astype(BF16)
    fzg_ref[...] = _silu(fz).astype(BF16)
    scale = ATT_Q_SCALE
    tm = x.shape[0]
    for h in range(HEADS):
        sl = slice(h * HEAD_DIM, (h + 1) * HEAD_DIM)
        qh = fq[:, sl]
        qn = (qh * lax.rsqrt(jnp.mean(qh * qh, axis=-1, keepdims=True) + NORM_EPS)) * qg_ref[...]
        fq_ref[:, sl] = (qn * scale).astype(BF16)
        kh = fk[:, sl]
        kn = (kh * lax.rsqrt(jnp.mean(kh * kh, axis=-1, keepdims=True) + NORM_EPS)) * kg_ref[...]
        fkb_ref[:, sl] = kn.astype(BF16)
        fk_ref[pl.ds(h, tm, stride=HEADS), :] = kn
        fv_ref[pl.ds(h, tm, stride=HEADS), :] = fv[:, sl]


def _inproj(x2d, mod4, per_row_mod, tm, norm_g, w_big, w_small, q_norm_g, k_norm_g):
    rows = x2d.shape[0]
    nblk = rows // tm
    if per_row_mod:
        mod_block = (1, 1, tm, D_MODEL)
        sh_map = lambda i: (0, 0, i, 0)
        sc_map = lambda i: (1, 0, i, 0)
    else:
        blocks_per_batch = nblk // mod4.shape[1]
        mod_block = (1, 1, 1, D_MODEL)
        sh_map = lambda i: (0, i // blocks_per_batch, 0, 0)
        sc_map = lambda i: (1, i // blocks_per_batch, 0, 0)
    row_spec = lambda n: pl.BlockSpec((tm, n), lambda i: (i, 0))
    const = lambda shape: pl.BlockSpec(shape, lambda i: (0,) * len(shape))
    outs = [(CONV_DIM, F32, 1), (WIDTH, F32, 1), (LANES, F32, 1), (WIDTH, BF16, 1), (HEAD_DIM, F32, HEADS),
            (WIDTH, BF16, 1), (HEAD_DIM, F32, HEADS), (WIDTH, BF16, 1), (WIDTH, BF16, 1)]
    return pl.pallas_call(
        _inproj_kernel,
        grid=(nblk,),
        in_specs=[row_spec(D_MODEL),
                  pl.BlockSpec(mod_block, sh_map), pl.BlockSpec(mod_block, sc_map),
                  const((1, D_MODEL)), const((D_MODEL, BIG_COLS)), const((D_MODEL, LANES)),
                  const((1, HEAD_DIM)), const((1, HEAD_DIM))],
        out_specs=[pl.BlockSpec((tm * r, n), lambda i: (i, 0)) for n, _, r in outs],
        out_shape=[jax.ShapeDtypeStruct((rows * r, n), dt) for n, dt, r in outs],
        compiler_params=_params(1),
        name="in_proj",
    )(x2d, mod4, mod4, norm_g, w_big, w_small, q_norm_g, k_norm_g)


def _token_scalars(x, par):
    beta = _sigmoid(x)
    g = -jnp.exp(par[0:1, :]) * _softplus(x + par[1:2, :])
    lf = -_softplus(-(x + par[2:3, :]))
    return beta, g, lf


def _small_kernel(sm_ref, par_ref, col_ref, row_ref, carry_ref):
    tb = SMALL_TB

    @pl.when(pl.program_id(1) == 0)
    def _():
        carry_ref[...] = jnp.zeros_like(carry_ref)

    x = sm_ref[...]
    beta, g, lf = _token_scalars(x, par_ref[...])
    lane = lax.broadcasted_iota(jnp.int32, (tb, LANES), 1)
    is_b = lane < LANE_G
    is_g = (lane >= LANE_G) & (lane < LANE_F)
    is_f = (lane >= LANE_F) & (lane < LANE_LF)
    gm = jnp.where(is_g, g, 0.0)
    fm = jnp.where(is_f, lf, 0.0)
    r = lax.broadcasted_iota(jnp.int32, (tb, tb), 0)
    c = lax.broadcasted_iota(jnp.int32, (tb, tb), 1)
    tri = c <= r
    same_chunk = jnp.right_shift(r, 6) == jnp.right_shift(c, 6)
    gcum = _mm_exact_rhs(_ones_mask(tri & same_chunk), gm)
    cum = _mm_exact_rhs(_ones_mask(tri), fm) + carry_ref[...]
    cum = jnp.where(is_f, cum, 0.0)
    carry_ref[...] = cum[tb - 1:tb, :]
    col = jnp.where(is_b, beta, 0.0) + gcum + cum + pltpu.roll(fm, LANE_LF - LANE_F, axis=1)
    col_ref[...] = col
    for i in range(tb // LANES):
        t = col[i * LANES:(i + 1) * LANES, :]
        row_ref[0, :, i * LANES:(i + 1) * LANES] = t.T[0:2 * SUBLANES, :]


def _small_prep(small, par, batch, seq):
    nt = seq // SMALL_TB
    return pl.pallas_call(
        _small_kernel,
        grid=(batch, nt),
        in_specs=[pl.BlockSpec((SMALL_TB, LANES), lambda b, t: (b * nt + t, 0)),
                  pl.BlockSpec((SUBLANES, LANES), lambda b, t: (0, 0))],
        out_specs=[pl.BlockSpec((SMALL_TB, LANES), lambda b, t: (b * nt + t, 0)),
                   pl.BlockSpec((1, 2 * SUBLANES, SMALL_TB), lambda b, t: (b, 0, t))],
        out_shape=[jax.ShapeDtypeStruct((batch * seq, LANES), F32),
                   jax.ShapeDtypeStruct((batch, 2 * SUBLANES, seq), F32)],
        scratch_shapes=[pltpu.VMEM((1, LANES), F32)],
        compiler_params=_params(2),
        name="token_scalars",
    )(small, par)


def _unit_lower_inverses(ms, eye, ri, ci):
    base = 3
    same = jnp.right_shift(ri, base) == jnp.right_shift(ci, base)
    xps = [-jnp.where(same, m, 0.0) for m in ms]
    ps = [eye + xp for xp in xps]
    for _ in range(base - 1):
        xps = [_mm(xp, xp) for xp in xps]
        ps = [p + _mm(p, xp) for p, xp in zip(ps, xps)]
    size = base
    while (1 << size) < CHUNK:
        off = ((jnp.right_shift(ri, size + 1) == jnp.right_shift(ci, size + 1))
               & (jnp.right_shift(ri, size) != jnp.right_shift(ci, size)))
        pcs = [_mm(p, jnp.where(off, m, 0.0)) for p, m in zip(ps, ms)]
        ps = [p - _mm(pc, p) for p, pc in zip(ps, pcs)]
        size += 1
    return ps


def _gdn_chunk_preps(qs, ks, vs, betas, gcs, grs, incl, strict, eye, ri, ci):
    decays = [jnp.where(incl, jnp.exp(jnp.where(incl, gc - gr, 0.0)), 0.0) for gc, gr in zip(gcs, grs)]
    kbs = [k * beta for k, beta in zip(ks, betas)]
    boths = [_mm_nt(jnp.concatenate([kb, q], axis=0), k) for kb, q, k in zip(kbs, qs, ks)]
    ms = [jnp.where(strict, both[0:CHUNK] * decay, 0.0) for both, decay in zip(boths, decays)]
    a_intras = [both[CHUNK:2 * CHUNK] * decay for both, decay in zip(boths, decays)]
    tinvs = _unit_lower_inverses(ms, eye, ri, ci)
    egs = [jnp.exp(gc) for gc in gcs]
    uws = [_mm(tinv, jnp.concatenate([v * beta, kb * eg], axis=1))
           for tinv, v, beta, kb, eg in zip(tinvs, vs, betas, kbs, egs)]
    preps = []
    for uw, q, k, eg, gc, a_intra in zip(uws, qs, ks, egs, gcs, a_intras):
        g_last = gc[CHUNK - 1:CHUNK, :]
        wq = jnp.concatenate([uw[:, HEAD_DIM:2 * HEAD_DIM], q * eg], axis=0)
        preps.append((uw[:, 0:HEAD_DIM], wq, a_intra, k * jnp.exp(g_last - gc), jnp.exp(g_last)))
    return preps


def _gdn_chunk_steps(preps, states):
    boths = [_mm(p[1], s) for p, s in zip(preps, states)]
    v_news = [p[0] - both[0:CHUNK] for p, both in zip(preps, boths)]
    outs = [both[CHUNK:2 * CHUNK] + _mm(p[2], v_new) for p, both, v_new in zip(preps, boths, v_news)]
    new_states = [s * p[4] + _mm_tn(p[3], v_new) for p, s, v_new in zip(preps, states, v_news)]
    return outs, new_states


def _gdn_kernel(qkv_ref, gz_ref, col_ref, row_ref, cw_ref, gn_ref,
                o_ref, ssm_ref, conv_ref, s_ref, xbuf_ref, qn_ref, kn_ref, vn_ref):
    tb = GDN_TB
    t = pl.program_id(1)
    halo = SUBLANES

    @pl.when(t == 0)
    def _():
        s_ref[...] = jnp.zeros_like(s_ref)
        xbuf_ref[0:halo, :] = jnp.zeros((halo, CONV_DIM), F32)

    xbuf_ref[halo:halo + tb, :] = qkv_ref[...]
    scale = HEAD_DIM ** -0.5
    for cb in range(CONV_DIM // LANES):
        sl = slice(cb * LANES, (cb + 1) * LANES)
        acc = xbuf_ref[halo - 3:halo - 3 + tb, sl] * cw_ref[0:1, sl]
        for i in range(1, CONV_W):
            acc = acc + xbuf_ref[halo - 3 + i:halo - 3 + i + tb, sl] * cw_ref[i:i + 1, sl]
        act = _silu(acc)
        part, h = divmod(cb, HEADS)
        hs = slice(h * LANES, (h + 1) * LANES)
        if part == 0:
            qn_ref[:, hs] = act * lax.rsqrt(jnp.sum(act * act, axis=-1, keepdims=True) + NORM_EPS) * scale
        elif part == 1:
            kn_ref[:, hs] = act * lax.rsqrt(jnp.sum(act * act, axis=-1, keepdims=True) + NORM_EPS)
        else:
            vn_ref[:, hs] = act
    tail = xbuf_ref[tb:tb + halo, :]
    xbuf_ref[0:halo, :] = tail

    ri = lax.broadcasted_iota(jnp.int32, (CHUNK, CHUNK), 0)
    ci = lax.broadcasted_iota(jnp.int32, (CHUNK, CHUNK), 1)
    incl = ci <= ri
    strict = ci < ri
    eye = jnp.where(ci == ri, 1.0, 0.0)

    def group_body(i, carry):
        base = pl.multiple_of(i * GDN_GROUP, GDN_GROUP)
        colt = col_ref[pl.ds(base, GDN_GROUP), :]
        rowt = row_ref[0, :, pl.ds(base, GDN_GROUP)]
        n_cc = GDN_GROUP // CHUNK
        items = [(cc, h) for cc in range(n_cc) for h in range(HEADS)]
        rows = lambda cc: pl.ds(pl.multiple_of(base + cc * CHUNK, CHUNK), CHUNK)
        rs = lambda cc: slice(cc * CHUNK, (cc + 1) * CHUNK)
        hs = lambda h: slice(h * HEAD_DIM, (h + 1) * HEAD_DIM)
        preps = _gdn_chunk_preps(
            [qn_ref[rows(cc), hs(h)] for cc, h in items],
            [kn_ref[rows(cc), hs(h)] for cc, h in items],
            [vn_ref[rows(cc), hs(h)] for cc, h in items],
            [colt[rs(cc), LANE_B + h:LANE_B + h + 1] for cc, h in items],
            [colt[rs(cc), LANE_G + h:LANE_G + h + 1] for cc, h in items],
            [rowt[LANE_G + h:LANE_G + h + 1, rs(cc)] for cc, h in items],
            incl, strict, eye, ri, ci)
        states = [s_ref[h] for h in range(HEADS)]
        for cc in range(n_cc):
            outs, states = _gdn_chunk_steps(preps[cc * HEADS:(cc + 1) * HEADS], states)
            for h, o in enumerate(outs):
                on = (o * lax.rsqrt(jnp.mean(o * o, axis=-1, keepdims=True) + NORM_EPS)) * gn_ref[...]
                o_ref[rows(cc), hs(h)] = (on * _silu(gz_ref[rows(cc), hs(h)])).astype(o_ref.dtype)
        for h in range(HEADS):
            s_ref[h] = states[h]
        return carry

    lax.fori_loop(0, tb // GDN_GROUP, group_body, 0)

    @pl.when(t == pl.num_programs(1) - 1)
    def _():
        ssm_ref[0] = s_ref[...]
        conv_ref[0] = tail[halo - (CONV_W - 1):halo, :]


def _gdn_prompt(qkv, gz, col, row, conv_w, gdn_norm_g, batch, seq):
    nt = seq // GDN_TB
    rows = batch * seq
    row_spec = lambda n: pl.BlockSpec((GDN_TB, n), lambda b, t: (b * nt + t, 0))
    return pl.pallas_call(
        _gdn_kernel,
        grid=(batch, nt),
        in_specs=[row_spec(CONV_DIM), row_spec(WIDTH), row_spec(LANES),
                  pl.BlockSpec((1, 2 * SUBLANES, GDN_TB), lambda b, t: (b, 0, t)),
                  pl.BlockSpec((CONV_W, CONV_DIM), lambda b, t: (0, 0)),
                  pl.BlockSpec((1, HEAD_DIM), lambda b, t: (0, 0))],
        out_specs=[row_spec(WIDTH),
                   pl.BlockSpec((1, HEADS, HEAD_DIM, HEAD_DIM), lambda b, t: (b, 0, 0, 0)),
                   pl.BlockSpec((1, CONV_W - 1, CONV_DIM), lambda b, t: (b, 0, 0))],
        out_shape=[jax.ShapeDtypeStruct((rows, WIDTH), BF16),
                   jax.ShapeDtypeStruct((batch, HEADS, HEAD_DIM, HEAD_DIM), F32),
                   jax.ShapeDtypeStruct((batch, CONV_W - 1, CONV_DIM), F32)],
        scratch_shapes=[pltpu.VMEM((HEADS, HEAD_DIM, HEAD_DIM), F32),
                        pltpu.VMEM((GDN_TB + SUBLANES, CONV_DIM), F32),
                        pltpu.VMEM((GDN_TB, WIDTH), F32),
                        pltpu.VMEM((GDN_TB, WIDTH), F32),
                        pltpu.VMEM((GDN_TB, WIDTH), F32)],
        compiler_params=_params(2),
        name="gdn_prompt",
    )(qkv, gz, col, row, conv_w, gdn_norm_g)


def _att_kernel(qi_ref, kj_ref, q_ref, k_ref, v_ref, ck_ref, cq_ref, gate_ref, o_ref,
                m_ref, l_ref, acc_ref, cqb_ref):
    step = pl.program_id(1)
    qi = qi_ref[step]
    kj = kj_ref[step]
    tq = tk = ATT_T
    heads = range(HEADS)
    hs = lambda h: slice(h * HEAD_DIM, (h + 1) * HEAD_DIM)

    @pl.when(kj == 0)
    def _():
        m_ref[...] = jnp.full_like(m_ref, NEG_BIG)
        l_ref[...] = jnp.zeros_like(l_ref)
        acc_ref[...] = jnp.zeros_like(acc_ref)
        for h in heads:
            cqb_ref[h] = jnp.broadcast_to(cq_ref[:, LANE_F + h:LANE_F + h + 1] * LOG2E, (tq, LANES))

    def update(diagonal):
        ss = [lax.dot_general(q_ref[:, hs(h)], k_ref[:, hs(h)], (((1,), (1,)), ((), ())),
                              preferred_element_type=F32) for h in heads]
        tts = [s - ck_ref[0, LANE_F + h:LANE_F + h + 1, :] * LOG2E for h, s in zip(heads, ss)]
        if diagonal:
            visible = (lax.broadcasted_iota(jnp.int32, (tq, tk), 1)
                       <= lax.broadcasted_iota(jnp.int32, (tq, tk), 0))
            tts = [jnp.where(visible, tt, NEG_BIG) for tt in tts]
        cqs = [cqb_ref[h] for h in heads]
        m_olds = [m_ref[h] for h in heads]
        m_news = [jnp.maximum(m_old, jnp.broadcast_to(jnp.max(tt, axis=1, keepdims=True), (tq, LANES)) + cq)
                  for m_old, tt, cq in zip(m_olds, tts, cqs)]
        shifts = [jnp.concatenate([m_new - cq] * (tk // LANES), axis=1) for m_new, cq in zip(m_news, cqs)]
        ps = [jnp.exp2(tt - shift).astype(BF16) for tt, shift in zip(tts, shifts)]
        ones = jnp.ones((tk, HEAD_DIM), BF16)
        pvs = [jnp.dot(p, jnp.concatenate([v_ref[:, hs(h)], ones], axis=1), preferred_element_type=F32)
               for h, p in zip(heads, ps)]
        alphas = [jnp.exp2(m_old - m_new) for m_old, m_new in zip(m_olds, m_news)]
        accs = [alpha * acc_ref[h] + pv[:, 0:HEAD_DIM] for h, alpha, pv in zip(heads, alphas, pvs)]
        ls = [alpha * l_ref[h] + pv[:, HEAD_DIM:2 * HEAD_DIM] for h, alpha, pv in zip(heads, alphas, pvs)]
        return m_news, ls, accs

    @pl.when(kj < qi)
    def _():
        m_news, ls, accs = update(False)
        for h in heads:
            m_ref[h] = m_news[h]
            l_ref[h] = ls[h]
            acc_ref[h] = accs[h]

    @pl.when(kj == qi)
    def _():
        _, ls, accs = update(True)
        for h in heads:
            o_ref[:, hs(h)] = ((accs[h] / ls[h]) * gate_ref[:, hs(h)].astype(F32)).astype(o_ref.dtype)


def _att_prompt(q, k, v, row, col, gate, batch, seq):
    n = seq // ATT_T
    rows = batch * seq
    pairs = [(i, j) for i in range(n) for j in range(i + 1)]
    qi_tab = jnp.asarray([i for i, _ in pairs], jnp.int32)
    kj_tab = jnp.asarray([j for _, j in pairs], jnp.int32)
    qmap = lambda b, s, qi, kj: (b * n + qi[s], 0)
    kmap = lambda b, s, qi, kj: (b * n + kj[s], 0)
    grid_spec = pltpu.PrefetchScalarGridSpec(
        num_scalar_prefetch=2,
        grid=(batch, len(pairs)),
        in_specs=[pl.BlockSpec((ATT_T, WIDTH), qmap),
                  pl.BlockSpec((ATT_T, WIDTH), kmap),
                  pl.BlockSpec((ATT_T, WIDTH), kmap),
                  pl.BlockSpec((1, 2 * SUBLANES, ATT_T), lambda b, s, qi, kj: (b, 0, kj[s])),
                  pl.BlockSpec((ATT_T, LANES), qmap),
                  pl.BlockSpec((ATT_T, WIDTH), qmap)],
        out_specs=pl.BlockSpec((ATT_T, WIDTH), qmap),
        scratch_shapes=[pltpu.VMEM((HEADS, ATT_T, LANES), F32)] * 4)
    return pl.pallas_call(
        _att_kernel,
        grid_spec=grid_spec,
        out_shape=jax.ShapeDtypeStruct((rows, WIDTH), BF16),
        compiler_params=_params(2),
        name="fox_prompt",
    )(qi_tab, kj_tab, q, k, v, row, col, gate)


def _outproj_kernel(x_ref, oa_ref, ob_ref, gate_ref, w_ref, y_ref):
    mix = (jnp.dot(oa_ref[...].astype(BF16), w_ref[0:WIDTH, :], preferred_element_type=F32)
           + jnp.dot(ob_ref[...].astype(BF16), w_ref[WIDTH:2 * WIDTH, :], preferred_element_type=F32))
    y_ref[...] = x_ref[...] + gate_ref[0, 0] * mix


def _outproj(x2d, oa, ob, mod4, per_row_mod, tm, w_out_b):
    rows = x2d.shape[0]
    nblk = rows // tm
    if per_row_mod:
        mod_block = (1, 1, tm, D_MODEL)
        g_map = lambda i: (2, 0, i, 0)
    else:
        blocks_per_batch = nblk // mod4.shape[1]
        mod_block = (1, 1, 1, D_MODEL)
        g_map = lambda i: (2, i // blocks_per_batch, 0, 0)
    row_spec = lambda n: pl.BlockSpec((tm, n), lambda i: (i, 0))
    return pl.pallas_call(
        _outproj_kernel,
        grid=(nblk,),
        in_specs=[row_spec(D_MODEL), row_spec(WIDTH), row_spec(WIDTH),
                  pl.BlockSpec(mod_block, g_map),
                  pl.BlockSpec((2 * WIDTH, D_MODEL), lambda i: (0, 0))],
        out_specs=row_spec(D_MODEL),
        out_shape=jax.ShapeDtypeStruct((rows, D_MODEL), F32),
        compiler_params=_params(1),
        name="out_proj",
    )(x2d, oa, ob, mod4, w_out_b)


def _smid_kernel(u_ref, s0_ref, s1_ref, s2_ref, cw_ref, sm_ref, par_ref,
                 qn_ref, kn_ref, vn_ref, sc_ref):
    conv = (s0_ref[...] * cw_ref[0:1, :] + s1_ref[...] * cw_ref[1:2, :]
            + s2_ref[...] * cw_ref[2:3, :] + u_ref[...] * cw_ref[3:4, :])
    act = _silu(conv)
    for h in range(HEADS):
        hs = slice(h * HEAD_DIM, (h + 1) * HEAD_DIM)
        q = act[:, h * HEAD_DIM:(h + 1) * HEAD_DIM]
        k = act[:, WIDTH + h * HEAD_DIM:WIDTH + (h + 1) * HEAD_DIM]
        qn_ref[:, hs] = q * lax.rsqrt(jnp.sum(q * q, axis=-1, keepdims=True) + NORM_EPS)
        kn_ref[:, hs] = k * lax.rsqrt(jnp.sum(k * k, axis=-1, keepdims=True) + NORM_EPS)
    vn_ref[...] = act[:, 2 * WIDTH:3 * WIDTH]
    beta, g, lf = _token_scalars(sm_ref[...], par_ref[...])
    lane = lax.broadcasted_iota(jnp.int32, sm_ref.shape, 1)
    sc_ref[...] = jnp.where(lane < LANE_G, beta, jnp.where(lane < LANE_F, g, lf))


def _sample_mid(u, s0, s1, s2, conv_w, small, par):
    nb = u.shape[0]
    full = lambda a: pl.BlockSpec(a.shape, lambda: (0,) * a.ndim)
    args = (u, s0, s1, s2, conv_w, small, par)
    return pl.pallas_call(
        _smid_kernel,
        in_specs=[full(a) for a in args],
        out_specs=[pl.BlockSpec((nb, WIDTH), lambda: (0, 0))] * 3 + [pl.BlockSpec((nb, LANES), lambda: (0, 0))],
        out_shape=[jax.ShapeDtypeStruct((nb, WIDTH), F32)] * 3 + [jax.ShapeDtypeStruct((nb, LANES), F32)],
        compiler_params=pltpu.CompilerParams(vmem_limit_bytes=VMEM_LIMIT),
        name="sample_conv",
    )(*args)


def _gdn_step_kernel(t_ref, s_ref, gz_ref, gn_ref, o_ref, sn_ref):
    scale = HEAD_DIM ** -0.5
    row = lax.broadcasted_iota(jnp.int32, (SUBLANES, HEAD_DIM), 0)
    items = [(b, h) for b in range(STEP_BATCH) for h in range(HEADS)]
    ts = [t_ref[b, h] for b, h in items]
    rs = [_mm(t, s_ref[b, h]) for t, (b, h) in zip(ts, items)]
    egs, v_news = [], []
    for t, r, (b, h) in zip(ts, rs, items):
        k, q, v, beta, g = (t[i:i + 1, :] for i in range(5))
        eg = jnp.exp(g)
        v_new = beta * (v - eg * r[0:1, :])
        o = scale * (eg * r[1:2, :] + jnp.sum(q * k, axis=-1, keepdims=True) * v_new)
        on = (o * lax.rsqrt(jnp.mean(o * o, axis=-1, keepdims=True) + NORM_EPS)) * gn_ref[...]
        o_ref[b, h:h + 1, :] = on * _silu(gz_ref[b, h:h + 1, :])
        egs.append(eg)
        v_news.append(v_new)
    outers = [_mm_tn(jnp.where(row == 0, t, 0.0),
                     jnp.where(row == 0, jnp.broadcast_to(v_new, (SUBLANES, HEAD_DIM)), 0.0))
              for t, v_new in zip(ts, v_news)]
    for eg, outer, (b, h) in zip(egs, outers, items):
        sn_ref[b, h] = s_ref[b, h] * eg + outer


def _gdn_step(t8, state, gz3, gdn_norm_g):
    nb = state.shape[0]
    sb = STEP_BATCH
    return pl.pallas_call(
        _gdn_step_kernel,
        grid=(nb // sb,),
        in_specs=[pl.BlockSpec((sb, HEADS, SUBLANES, HEAD_DIM), lambda b: (b, 0, 0, 0)),
                  pl.BlockSpec((sb, HEADS, HEAD_DIM, HEAD_DIM), lambda b: (b, 0, 0, 0)),
                  pl.BlockSpec((sb, HEADS, HEAD_DIM), lambda b: (b, 0, 0)),
                  pl.BlockSpec((1, HEAD_DIM), lambda b: (0, 0))],
        out_specs=[pl.BlockSpec((sb, HEADS, HEAD_DIM), lambda b: (b, 0, 0)),
                   pl.BlockSpec((sb, HEADS, HEAD_DIM, HEAD_DIM), lambda b: (b, 0, 0, 0))],
        out_shape=[jax.ShapeDtypeStruct((nb, HEADS, HEAD_DIM), F32),
                   jax.ShapeDtypeStruct((nb, HEADS, HEAD_DIM, HEAD_DIM), F32)],
        compiler_params=_params(1),
        name="gdn_step",
    )(t8, state, gz3, gdn_norm_g)


def _class_reduce(x, op):
    shift = HEADS
    while shift < LANES:
        x = op(x, pltpu.roll(x, shift, axis=1))
        shift *= 2
    return x


def _lane_to_col(v, eye8):
    return jnp.sum(jnp.where(eye8, jnp.broadcast_to(v, (SUBLANES, LANES)), 0.0), axis=1, keepdims=True)


def _dec_kernel(pt_ref, q8_ref, kn_ref, vn_ref, gate_ref, lfn_ref, ck_hbm, cv_hbm, clf_hbm, o_ref,
                kbuf, vbuf, lfbuf, sems, sd_ref):
    b = pl.program_id(0)
    nb = pl.num_programs(0)
    n_pages = pt_ref.shape[1]
    gp = DEC_PAGES
    ng = n_pages // gp
    nrow = gp * HEADS
    sig_of = lambda r, a: (gp - 1 - r) * HEADS + a

    def group_copies(seq, g, slot):
        out = []
        for r in range(gp):
            page = pt_ref[seq, n_pages - 1 - (g * gp + r)]
            row0 = pl.multiple_of(page * PAGE_ROWS, PAGE_ROWS)
            out.append(pltpu.make_async_copy(ck_hbm.at[pl.ds(row0, PAGE_ROWS), :], kbuf.at[slot, r],
                                             sems.at[0, slot]))
            out.append(pltpu.make_async_copy(cv_hbm.at[pl.ds(row0, PAGE_ROWS), :], vbuf.at[slot, r],
                                             sems.at[1, slot]))
            out.append(pltpu.make_async_copy(clf_hbm.at[page], lfbuf.at[slot, pl.ds(sig_of(r, 0), HEADS), :],
                                             sems.at[2, slot]))
        return out

    @pl.when(b == 0)
    def _():
        for c in group_copies(0, 0, 0):
            c.start()

    ri = lax.broadcasted_iota(jnp.int32, (LANES, LANES), 0)
    ci = lax.broadcasted_iota(jnp.int32, (LANES, LANES), 1)
    eye = ri == ci
    same_head = (ri & (HEADS - 1)) == (ci & (HEADS - 1))
    eye8 = (lax.broadcasted_iota(jnp.int32, (SUBLANES, LANES), 0)
            == lax.broadcasted_iota(jnp.int32, (SUBLANES, LANES), 1))
    si = lax.broadcasted_iota(jnp.int32, (nrow, nrow), 0)
    sj = lax.broadcasted_iota(jnp.int32, (nrow, nrow), 1)
    ones = jnp.ones((LANES, LANES), BF16)
    qrep = jnp.concatenate([q8_ref[0]] * (LANES // Q_ROWS), axis=0)

    m_run = jnp.full((1, LANES), NEG_BIG, F32)
    l_run = jnp.zeros((1, LANES), F32)
    acc = jnp.zeros((SUBLANES, HEAD_DIM), F32)
    carry = lfn_ref[0]
    for g in range(ng):
        slot = g % 2
        if g + 1 < ng:
            for c in group_copies(b, g + 1, 1 - slot):
                c.start()
        else:
            @pl.when(b + 1 < nb)
            def _():
                for c in group_copies(b + 1, 0, 1 - slot):
                    c.start()
        for c in group_copies(b, g, slot):
            c.wait()

        lf = lfbuf[slot]
        within = _mm_exact_lhs(lf, _ones_mask(same_head & (ri > ci)))
        tot = _mm_exact_lhs(lf, _ones_mask(same_head))
        later = _mm_exact_rhs(_ones_mask(sj > si), tot)
        bias = (within + later + carry) * LOG2E
        carry = carry + jnp.sum(tot, axis=0, keepdims=True)

        for r in range(gp):
            s2 = _mm_nt(kbuf[slot, r], qrep)
            for a in range(HEADS):
                blk = s2[a * LANES:(a + 1) * LANES, :]
                sig = sig_of(r, a)
                sd_ref[sig:sig + 1, :] = jnp.sum(jnp.where(eye, blk, 0.0), axis=0, keepdims=True)
        for sub in range(gp // DEC_SUB):
            pages = range(sub * DEC_SUB, (sub + 1) * DEC_SUB)
            lo = sig_of(pages[-1], 0)
            rows = slice(lo, lo + DEC_SUB * HEADS)
            tt = sd_ref[rows, :] + bias[rows, :]
            mx = jnp.max(tt, axis=0, keepdims=True)
            mx = _class_reduce(jnp.broadcast_to(mx, (SUBLANES, LANES)), jnp.maximum)[0:1, :]
            m_new = jnp.maximum(m_run, mx)
            pf = jnp.exp2(tt - m_new).astype(BF16).astype(F32)
            alpha = jnp.exp2(m_run - m_new)
            l_run = alpha * l_run + jnp.sum(pf, axis=0, keepdims=True)
            m_run = m_new
            acc = acc * _lane_to_col(alpha, eye8)
            for r in pages:
                parts = []
                for a in range(HEADS):
                    sig = sig_of(r, a) - lo
                    prow = jnp.broadcast_to(pf[sig:sig + 1, :], (LANES, LANES))
                    parts.append(jnp.where(eye, prow, 0.0).astype(BF16))
                pcol = jnp.dot(jnp.concatenate(parts, axis=0), ones, preferred_element_type=F32)
                pv = pcol * vbuf[slot, r]
                acc = acc + jnp.sum(pv.reshape(PAGE_ROWS // SUBLANES, SUBLANES, LANES), axis=0)

    acc4 = acc[0:HEADS, :] + acc[HEADS:2 * HEADS, :]
    mcol = _lane_to_col(m_run, eye8)[0:HEADS, :]
    lsum = _class_reduce(jnp.broadcast_to(l_run, (SUBLANES, LANES)), jnp.add)[0:1, :]
    lcol = _lane_to_col(lsum, eye8)[0:HEADS, :]
    q4 = q8_ref[0][0:HEADS, :].astype(F32)
    s_new = jnp.sum(q4 * kn_ref[0], axis=-1, keepdims=True)
    m_f = jnp.maximum(mcol, s_new)
    a1 = jnp.exp2(mcol - m_f)
    a2 = jnp.exp2(s_new - m_f)
    o = (acc4 * a1 + a2 * vn_ref[0]) / (lcol * a1 + a2)
    o_ref[0] = o * gate_ref[0].astype(F32)


def _dec_attention(page_table, q8, k_new, v_new, gate, lf_new, cache_k2, cache_v2, cache_lf3):
    nb, n_pages = page_table.shape
    gp = DEC_PAGES
    assert n_pages % (2 * gp) == 0, "an even number of page groups per sequence keeps the buffer slots static"
    per_b = lambda b, pt: (b, 0, 0)
    hbm = pl.BlockSpec(memory_space=pl.ANY)
    grid_spec = pltpu.PrefetchScalarGridSpec(
        num_scalar_prefetch=1,
        grid=(nb,),
        in_specs=[pl.BlockSpec((1, Q_ROWS, HEAD_DIM), per_b),
                  pl.BlockSpec((1, HEADS, HEAD_DIM), per_b),
                  pl.BlockSpec((1, HEADS, HEAD_DIM), per_b),
                  pl.BlockSpec((1, HEADS, HEAD_DIM), per_b),
                  pl.BlockSpec((1, 1, LANES), per_b),
                  hbm, hbm, hbm],
        out_specs=pl.BlockSpec((1, HEADS, HEAD_DIM), per_b),
        scratch_shapes=[pltpu.VMEM((2, gp, PAGE_ROWS, HEAD_DIM), F32),
                        pltpu.VMEM((2, gp, PAGE_ROWS, HEAD_DIM), F32),
                        pltpu.VMEM((2, gp * HEADS, LANES), F32),
                        pltpu.SemaphoreType.DMA((3, 2)),
                        pltpu.VMEM((gp * HEADS, LANES), F32)])
    return pl.pallas_call(
        _dec_kernel,
        grid_spec=grid_spec,
        out_shape=jax.ShapeDtypeStruct((nb, HEADS, HEAD_DIM), F32),
        compiler_params=pltpu.CompilerParams(dimension_semantics=("arbitrary",),
                                             vmem_limit_bytes=DEC_VMEM_LIMIT),
        name="fox_decode",
    )(page_table, q8, k_new, v_new, gate, lf_new, cache_k2, cache_v2, cache_lf3)


def _layer(x_prompt, x_sample, cache_k, cache_v, cache_logf, state_ssm, state_conv, page_table,
           c_prompt, c_sample, norm_g, w_ada, b_ada, w_in, b_f, conv_w, a_log, dt_bias,
           gdn_norm_g, q_norm_g, k_norm_g, w_out):
    batch, seq, _ = x_prompt.shape
    nb = x_sample.shape[0]
    w = WIDTH
    o_small = 4 * w
    o_fox = o_small + 2 * HEADS
    w_big = jnp.concatenate([w_in[:, 0:o_small], w_in[:, o_fox:o_fox + 4 * w]], axis=1).astype(BF16)
    w_small = jnp.concatenate([w_in[:, o_small:o_fox], w_in[:, o_fox + 4 * w:],
                               jnp.zeros((D_MODEL, LANES - 3 * HEADS), F32)], axis=1).astype(BF16)
    w_out_b = w_out.astype(BF16)
    par = jnp.zeros((SUBLANES, LANES), F32)
    par = par.at[0, LANE_G:LANE_G + HEADS].set(a_log)
    par = par.at[1, LANE_G:LANE_G + HEADS].set(dt_bias)
    par = par.at[2, LANE_F:LANE_F + HEADS].set(b_f)
    ng2, qg2, kg2, gg2 = (a.reshape(1, -1) for a in (norm_g, q_norm_g, k_norm_g, gdn_norm_g))

    cp8 = jnp.pad(c_prompt, ((0, SUBLANES - batch), (0, 0)))
    mod_p, mod_s = _ada_mod(cp8, c_sample, w_ada, b_ada.reshape(1, -1))
    mod_p4 = mod_p[:, 0:batch].reshape(3, batch, 1, D_MODEL)
    mod_s4 = mod_s.reshape(3, 1, nb, D_MODEL)

    xp = x_prompt.reshape(batch * seq, D_MODEL)
    qkv, gz, small, fq, fk, fkb, fv, fvb, fzg = _inproj(xp, mod_p4, False, PROMPT_TM, ng2, w_big, w_small, qg2, kg2)
    col, row = _small_prep(small, par, batch, seq)
    o_a, ssm_p, conv_p = _gdn_prompt(qkv, gz, col, row, conv_w, gg2, batch, seq)
    o_b = _att_prompt(fq, fkb, fvb, row, col, fzg, batch, seq)
    y_p = _outproj(xp, o_a, o_b, mod_p4, False, OUT_TM, w_out_b).reshape(batch, seq, D_MODEL)
    k_p = fk.reshape(batch, seq, HEADS, HEAD_DIM)
    v_p = fv.reshape(batch, seq, HEADS, HEAD_DIM)
    lf_p = col[:, LANE_LF:LANE_LF + HEADS].reshape(batch, seq, HEADS)

    xs = x_sample.reshape(nb, D_MODEL)
    qkv_s, gz_s, small_s, fq_s, fk_s, _, fv_s, _, fzg_s = _inproj(xs, mod_s4, True, nb, ng2, w_big, w_small, qg2, kg2)
    qn_s, kn_s, vn_s, sc_s = _sample_mid(qkv_s, state_conv[:, 0], state_conv[:, 1], state_conv[:, 2],
                                         conv_w, small_s, par)
    h3 = lambda a: a.reshape(nb, HEADS, HEAD_DIM)
    bcast = lambda a: jnp.broadcast_to(a[:, :, None], (nb, HEADS, HEAD_DIM))
    t8 = jnp.stack([h3(kn_s), h3(qn_s), h3(vn_s), bcast(sc_s[:, LANE_B:LANE_B + HEADS]),
                    bcast(sc_s[:, LANE_G:LANE_G + HEADS])]
                   + [jnp.zeros((nb, HEADS, HEAD_DIM), F32)] * (SUBLANES - 5), axis=2)
    o_a_s, ssm_s = _gdn_step(t8, state_ssm, h3(gz_s), gg2)
    lf_s = sc_s[:, LANE_F:LANE_F + HEADS]
    q8 = jnp.tile(h3(fq_s), (1, Q_ROWS // HEADS, 1))
    lf_new = jnp.tile(lf_s, (1, LANES // HEADS)).reshape(nb, 1, LANES)
    n_pool = cache_k.shape[0]
    o_b_s = _dec_attention(page_table, q8, h3(fk_s), h3(fv_s), h3(fzg_s), lf_new,
                           cache_k.reshape(n_pool * PAGE_ROWS, HEAD_DIM),
                           cache_v.reshape(n_pool * PAGE_ROWS, HEAD_DIM),
                           cache_logf.reshape(n_pool, HEADS, LANES))
    y_s = _outproj(xs, o_a_s.reshape(nb, w), o_b_s.reshape(nb, w), mod_s4, True, nb, w_out_b).reshape(nb, 1, D_MODEL)
    conv_s = jnp.stack([state_conv[:, 1], state_conv[:, 2], qkv_s], axis=1)
    return (y_p, y_s, k_p, v_p, lf_p, ssm_p, conv_p,
            h3(fk_s).reshape(nb, 1, HEADS, HEAD_DIM), h3(fv_s).reshape(nb, 1, HEADS, HEAD_DIM),
            lf_s.reshape(nb, 1, HEADS), ssm_s, conv_s)


def kernel(x_prompt, x_sample, cache_k, cache_v, cache_logf, state_ssm, state_conv, page_table, c_prompt, c_sample, norm_g, w_ada, b_ada, w_in, b_f, conv_w, a_log, dt_bias, gdn_norm_g, q_norm_g, k_norm_g, w_out):
    assert w_ada.shape[0] == 1, "single layer"
    outs = _layer(x_prompt, x_sample, cache_k[0], cache_v[0], cache_logf[0], state_ssm[0], state_conv[0],
                  page_table, c_prompt, c_sample, norm_g[0], w_ada[0], b_ada[0], w_in[0], b_f[0], conv_w[0],
                  a_log[0], dt_bias[0], gdn_norm_g[0], q_norm_g[0], k_norm_g[0], w_out[0])
    y_p, y_s = outs[0], outs[1]
    return (y_p, y_s) + tuple(o[None] for o in outs[2:])
```

```python
import functools

import jax
import jax.numpy as jnp
from jax import lax
from jax.experimental import pallas as pl
from jax.experimental.pallas import tpu as pltpu

F32 = jnp.float32
BF16 = jnp.bfloat16

D_MODEL = 1024
HEADS = 4
HEAD_DIM = 128
WIDTH = HEADS * HEAD_DIM
CONV_W = 4
CONV_DIM = 3 * WIDTH
CHUNK = 64
PAGE_SIZE = 128
PAGE_ROWS = PAGE_SIZE * HEADS
NORM_EPS = 1e-6
LANES = 128
SUBLANES = 8
NEG_BIG = -1e30
LOG2E = 1.4426950408889634
ATT_Q_SCALE = LOG2E * HEAD_DIM ** -0.5
VMEM_LIMIT = 48 * 1024 * 1024

COL_GQKV = 0
COL_GZ = 3 * WIDTH
COL_FQ = 4 * WIDTH
COL_FK = 5 * WIDTH
COL_FV = 6 * WIDTH
COL_FZ = 7 * WIDTH
BIG_COLS = 8 * WIDTH
LANE_B, LANE_G, LANE_F, LANE_LF = 0, 4, 8, 12

PROMPT_TM = 256
OUT_TM = 512
SMALL_TB = 512
GDN_TB = 512
GDN_GROUP = 256
ATT_T = 512
DEC_PAGES = 32
DEC_SUB = 16
DEC_VMEM_LIMIT = 56 * 1024 * 1024
Q_ROWS = 16
STEP_BATCH = 4


def _mm(a, b):
    return jnp.dot(a.astype(BF16), b.astype(BF16), preferred_element_type=F32)


def _mm_nt(a, b):
    return lax.dot_general(a.astype(BF16), b.astype(BF16), (((1,), (1,)), ((), ())),
                           preferred_element_type=F32)


def _mm_tn(a, b):
    return lax.dot_general(a.astype(BF16), b.astype(BF16), (((0,), (0,)), ((), ())),
                           preferred_element_type=F32)


def _mm_exact_rhs(mat_b, y):
    hi = y.astype(BF16)
    r1 = y - hi.astype(F32)
    mid = r1.astype(BF16)
    lo = (r1 - mid.astype(F32)).astype(BF16)
    dot = functools.partial(jnp.dot, preferred_element_type=F32)
    return dot(mat_b, hi) + dot(mat_b, mid) + dot(mat_b, lo)


def _mm_exact_lhs(y, mat_b):
    hi = y.astype(BF16)
    r1 = y - hi.astype(F32)
    mid = r1.astype(BF16)
    lo = (r1 - mid.astype(F32)).astype(BF16)
    dot = functools.partial(jnp.dot, preferred_element_type=F32)
    return dot(hi, mat_b) + dot(mid, mat_b) + dot(lo, mat_b)


def _sigmoid(x):
    return 1.0 / (1.0 + jnp.exp(-x))


def _silu(x):
    return x * _sigmoid(x)


def _softplus(x):
    return jnp.maximum(x, 0.0) + jnp.log(1.0 + jnp.exp(-jnp.abs(x)))


def _ones_mask(mask):
    return jnp.where(mask, 1.0, 0.0).astype(BF16)


def _params(n_axes):
    return pltpu.CompilerParams(dimension_semantics=("arbitrary",) * n_axes,
                                vmem_limit_bytes=VMEM_LIMIT)


def _ada_kernel(cp_ref, cs_ref, w_ref, b_ref, op_ref, os_ref):
    w = w_ref[...].astype(BF16)
    b = b_ref[...]
    op_ref[0] = jnp.dot(_silu(cp_ref[...]).astype(BF16), w, preferred_element_type=F32) + b
    os_ref[0] = jnp.dot(_silu(cs_ref[...]).astype(BF16), w, preferred_element_type=F32) + b


def _ada_mod(c_prompt8, c_sample, w_ada, b_ada):
    nb = c_sample.shape[0]
    return pl.pallas_call(
        _ada_kernel,
        grid=(3,),
        in_specs=[pl.BlockSpec((SUBLANES, D_MODEL), lambda k: (0, 0)),
                  pl.BlockSpec((nb, D_MODEL), lambda k: (0, 0)),
                  pl.BlockSpec((D_MODEL, D_MODEL), lambda k: (0, k)),
                  pl.BlockSpec((1, D_MODEL), lambda k: (0, k))],
        out_specs=[pl.BlockSpec((1, SUBLANES, D_MODEL), lambda k: (k, 0, 0)),
                   pl.BlockSpec((1, nb, D_MODEL), lambda k: (k, 0, 0))],
        out_shape=[jax.ShapeDtypeStruct((3, SUBLANES, D_MODEL), F32),
                   jax.ShapeDtypeStruct((3, nb, D_MODEL), F32)],
        compiler_params=_params(1),
        name="ada_mod",
    )(c_prompt8, c_sample, w_ada, b_ada)


def _inproj_kernel(x_ref, sh_ref, sc_ref, ng_ref, wb_ref, ws_ref, qg_ref, kg_ref,
                   qkv_ref, gz_ref, sm_ref, fq_ref, fk_ref, fkb_ref, fv_ref, fvb_ref, fzg_ref):
    x = x_ref[...]
    y = x * lax.rsqrt(jnp.mean(x * x, axis=-1, keepdims=True) + NORM_EPS)
    hdn = (y * ng_ref[...]) * (1.0 + sc_ref[0, 0]) + sh_ref[0, 0]
    hb = hdn.astype(BF16)
    dot = functools.partial(jnp.dot, preferred_element_type=F32)
    qkv_ref[...] = dot(hb, wb_ref[:, COL_GQKV:COL_GZ])
    gz_ref[...] = dot(hb, wb_ref[:, COL_GZ:COL_FQ])
    sm_ref[...] = dot(hb, ws_ref[...])
    fq = dot(hb, wb_ref[:, COL_FQ:COL_FK])
    fk = dot(hb, wb_ref[:, COL_FK:COL_FV])
    fv = dot(hb, wb_ref[:, COL_FV:COL_FZ])
    fz = dot(hb, wb_ref[:, COL_FZ:BIG_COLS])
    fvb_ref[...] = fv.astype(BF16)
    fzg_ref[...] = _silu(fz).astype(BF16)
    scale = ATT_Q_SCALE
    tm = x.shape[0]
    for h in range(HEADS):
        sl = slice(h * HEAD_DIM, (h + 1) * HEAD_DIM)
        qh = fq[:, sl]
        qn = (qh * lax.rsqrt(jnp.mean(qh * qh, axis=-1, keepdims=True) + NORM_EPS)) * qg_ref[...]
        fq_ref[:, sl] = (qn * scale).astype(BF16)
        kh = fk[:, sl]
        kn = (kh * lax.rsqrt(jnp.mean(kh * kh, axis=-1, keepdims=True) + NORM_EPS)) * kg_ref[...]
        fkb_ref[:, sl] = kn.astype(BF16)
        fk_ref[pl.ds(h, tm, stride=HEADS), :] = kn
        fv_ref[pl.ds(h, tm, stride=HEADS), :] = fv[:, sl]


def _inproj(x2d, mod4, per_row_mod, tm, norm_g, w_big, w_small, q_norm_g, k_norm_g):
    rows = x2d.shape[0]
    nblk = rows // tm
    if per_row_mod:
        mod_block = (1, 1, tm, D_MODEL)
        sh_map = lambda i: (0, 0, i, 0)
        sc_map = lambda i: (1, 0, i, 0)
    else:
        blocks_per_batch = nblk // mod4.shape[1]
        mod_block = (1, 1, 1, D_MODEL)
        sh_map = lambda i: (0, i // blocks_per_batch, 0, 0)
        sc_map = lambda i: (1, i // blocks_per_batch, 0, 0)
    row_spec = lambda n: pl.BlockSpec((tm, n), lambda i: (i, 0))
    const = lambda shape: pl.BlockSpec(shape, lambda i: (0,) * len(shape))
    outs = [(CONV_DIM, F32, 1), (WIDTH, F32, 1), (LANES, F32, 1), (WIDTH, BF16, 1), (HEAD_DIM, F32, HEADS),
            (WIDTH, BF16, 1), (HEAD_DIM, F32, HEADS), (WIDTH, BF16, 1), (WIDTH, BF16, 1)]
    return pl.pallas_call(
        _inproj_kernel,
        grid=(nblk,),
        in_specs=[row_spec(D_MODEL),
                  pl.BlockSpec(mod_block, sh_map), pl.BlockSpec(mod_block, sc_map),
                  const((1, D_MODEL)), const((D_MODEL, BIG_COLS)), const((D_MODEL, LANES)),
                  const((1, HEAD_DIM)), const((1, HEAD_DIM))],
        out_specs=[pl.BlockSpec((tm * r, n), lambda i: (i, 0)) for n, _, r in outs],
        out_shape=[jax.ShapeDtypeStruct((rows * r, n), dt) for n, dt, r in outs],
        compiler_params=_params(1),
        name="in_proj",
    )(x2d, mod4, mod4, norm_g, w_big, w_small, q_norm_g, k_norm_g)


def _token_scalars(x, par):
    beta = _sigmoid(x)
    g = -jnp.exp(par[0:1, :]) * _softplus(x + par[1:2, :])
    lf = -_softplus(-(x + par[2:3, :]))
    return beta, g, lf


def _small_kernel(sm_ref, par_ref, col_ref, row_ref, carry_ref):
    tb = SMALL_TB

    @pl.when(pl.program_id(1) == 0)
    def _():
        carry_ref[...] = jnp.zeros_like(carry_ref)

    x = sm_ref[...]
    beta, g, lf = _token_scalars(x, par_ref[...])
    lane = lax.broadcasted_iota(jnp.int32, (tb, LANES), 1)
    is_b = lane < LANE_G
    is_g = (lane >= LANE_G) & (lane < LANE_F)
    is_f = (lane >= LANE_F) & (lane < LANE_LF)
    gm = jnp.where(is_g, g, 0.0)
    fm = jnp.where(is_f, lf, 0.0)
    r = lax.broadcasted_iota(jnp.int32, (tb, tb), 0)
    c = lax.broadcasted_iota(jnp.int32, (tb, tb), 1)
    tri = c <= r
    same_chunk = jnp.right_shift(r, 6) == jnp.right_shift(c, 6)
    gcum = _mm_exact_rhs(_ones_mask(tri & same_chunk), gm)
    cum = _mm_exact_rhs(_ones_mask(tri), fm) + carry_ref[...]
    cum = jnp.where(is_f, cum, 0.0)
    carry_ref[...] = cum[tb - 1:tb, :]
    col = jnp.where(is_b, beta, 0.0) + gcum + cum + pltpu.roll(fm, LANE_LF - LANE_F, axis=1)
    col_ref[...] = col
    for i in range(tb // LANES):
        t = col[i * LANES:(i + 1) * LANES, :]
        row_ref[0, :, i * LANES:(i + 1) * LANES] = t.T[0:2 * SUBLANES, :]


def _small_prep(small, par, batch, seq):
    nt = seq // SMALL_TB
    return pl.pallas_call(
        _small_kernel,
        grid=(batch, nt),
        in_specs=[pl.BlockSpec((SMALL_TB, LANES), lambda b, t: (b * nt + t, 0)),
                  pl.BlockSpec((SUBLANES, LANES), lambda b, t: (0, 0))],
        out_specs=[pl.BlockSpec((SMALL_TB, LANES), lambda b, t: (b * nt + t, 0)),
                   pl.BlockSpec((1, 2 * SUBLANES, SMALL_TB), lambda b, t: (b, 0, t))],
        out_shape=[jax.ShapeDtypeStruct((batch * seq, LANES), F32),
                   jax.ShapeDtypeStruct((batch, 2 * SUBLANES, seq), F32)],
        scratch_shapes=[pltpu.VMEM((1, LANES), F32)],
        compiler_params=_params(2),
        name="token_scalars",
    )(small, par)


def _unit_lower_inverses(ms, eye, ri, ci):
    base = 3
    same = jnp.right_shift(ri, base) == jnp.right_shift(ci, base)
    xps = [-jnp.where(same, m, 0.0) for m in ms]
    ps = [eye + xp for xp in xps]
    for _ in range(base - 1):
        xps = [_mm(xp, xp) for xp in xps]
        ps = [p + _mm(p, xp) for p, xp in zip(ps, xps)]
    size = base
    while (1 << size) < CHUNK:
        off = ((jnp.right_shift(ri, size + 1) == jnp.right_shift(ci, size + 1))
               & (jnp.right_shift(ri, size) != jnp.right_shift(ci, size)))
        pcs = [_mm(p, jnp.where(off, m, 0.0)) for p, m in zip(ps, ms)]
        ps = [p - _mm(pc, p) for p, pc in zip(ps, pcs)]
        size += 1
    return ps


def _gdn_chunk_preps(qs, ks, vs, betas, gcs, grs, incl, strict, eye, ri, ci):
    decays = [jnp.where(incl, jnp.exp(jnp.where(incl, gc - gr, 0.0)), 0.0) for gc, gr in zip(gcs, grs)]
    kbs = [k * beta for k, beta in zip(ks, betas)]
    boths = [_mm_nt(jnp.concatenate([kb, q], axis=0), k) for kb, q, k in zip(kbs, qs, ks)]
    ms = [jnp.where(strict, both[0:CHUNK] * decay, 0.0) for both, decay in zip(boths, decays)]
    a_intras = [both[CHUNK:2 * CHUNK] * decay for both, decay in zip(boths, decays)]
    tinvs = _unit_lower_inverses(ms, eye, ri, ci)
    egs = [jnp.exp(gc) for gc in gcs]
    uws = [_mm(tinv, jnp.concatenate([v * beta, kb * eg], axis=1))
           for tinv, v, beta, kb, eg in zip(tinvs, vs, betas, kbs, egs)]
    preps = []
    for uw, q, k, eg, gc, a_intra in zip(uws, qs, ks, egs, gcs, a_intras):
        g_last = gc[CHUNK - 1:CHUNK, :]
        wq = jnp.concatenate([uw[:, HEAD_DIM:2 * HEAD_DIM], q * eg], axis=0)
        preps.append((uw[:, 0:HEAD_DIM], wq, a_intra, k * jnp.exp(g_last - gc), jnp.exp(g_last)))
    return preps


def _gdn_chunk_steps(preps, states):
    boths = [_mm(p[1], s) for p, s in zip(preps, states)]
    v_news = [p[0] - both[0:CHUNK] for p, both in zip(preps, boths)]
    outs = [both[CHUNK:2 * CHUNK] + _mm(p[2], v_new) for p, both, v_new in zip(preps, boths, v_news)]
    new_states = [s * p[4] + _mm_tn(p[3], v_new) for p, s, v_new in zip(preps, states, v_news)]
    return outs, new_states


def _gdn_kernel(qkv_ref, gz_ref, col_ref, row_ref, cw_ref, gn_ref,
                o_ref, ssm_ref, conv_ref, s_ref, xbuf_ref, qn_ref, kn_ref, vn_ref):
    tb = GDN_TB
    t = pl.program_id(1)
    halo = SUBLANES

    @pl.when(t == 0)
    def _():
        s_ref[...] = jnp.zeros_like(s_ref)
        xbuf_ref[0:halo, :] = jnp.zeros((halo, CONV_DIM), F32)

    xbuf_ref[halo:halo + tb, :] = qkv_ref[...]
    scale = HEAD_DIM ** -0.5
    for cb in range(CONV_DIM // LANES):
        sl = slice(cb * LANES, (cb + 1) * LANES)
        acc = xbuf_ref[halo - 3:halo - 3 + tb, sl] * cw_ref[0:1, sl]
        for i in range(1, CONV_W):
            acc = acc + xbuf_ref[halo - 3 + i:halo - 3 + i + tb, sl] * cw_ref[i:i + 1, sl]
        act = _silu(acc)
        part, h = divmod(cb, HEADS)
        hs = slice(h * LANES, (h + 1) * LANES)
        if part == 0:
            qn_ref[:, hs] = act * lax.rsqrt(jnp.sum(act * act, axis=-1, keepdims=True) + NORM_EPS) * scale
        elif part == 1:
            kn_ref[:, hs] = act * lax.rsqrt(jnp.sum(act * act, axis=-1, keepdims=True) + NORM_EPS)
        else:
            vn_ref[:, hs] = act
    tail = xbuf_ref[tb:tb + halo, :]
    xbuf_ref[0:halo, :] = tail

    ri = lax.broadcasted_iota(jnp.int32, (CHUNK, CHUNK), 0)
    ci = lax.broadcasted_iota(jnp.int32, (CHUNK, CHUNK), 1)
    incl = ci <= ri
    strict = ci < ri
    eye = jnp.where(ci == ri, 1.0, 0.0)

    def group_body(i, carry):
        base = pl.multiple_of(i * GDN_GROUP, GDN_GROUP)
        colt = col_ref[pl.ds(base, GDN_GROUP), :]
        rowt = row_ref[0, :, pl.ds(base, GDN_GROUP)]
        n_cc = GDN_GROUP // CHUNK
        items = [(cc, h) for cc in range(n_cc) for h in range(HEADS)]
        rows = lambda cc: pl.ds(pl.multiple_of(base + cc * CHUNK, CHUNK), CHUNK)
        rs = lambda cc: slice(cc * CHUNK, (cc + 1) * CHUNK)
        hs = lambda h: slice(h * HEAD_DIM, (h + 1) * HEAD_DIM)
        preps = _gdn_chunk_preps(
            [qn_ref[rows(cc), hs(h)] for cc, h in items],
            [kn_ref[rows(cc), hs(h)] for cc, h in items],
            [vn_ref[rows(cc), hs(h)] for cc, h in items],
            [colt[rs(cc), LANE_B + h:LANE_B + h + 1] for cc, h in items],
            [colt[rs(cc), LANE_G + h:LANE_G + h + 1] for cc, h in items],
            [rowt[LANE_G + h:LANE_G + h + 1, rs(cc)] for cc, h in items],
            incl, strict, eye, ri, ci)
        states = [s_ref[h] for h in range(HEADS)]
        for cc in range(n_cc):
            outs, states = _gdn_chunk_steps(preps[cc * HEADS:(cc + 1) * HEADS], states)
            for h, o in enumerate(outs):
                on = (o * lax.rsqrt(jnp.mean(o * o, axis=-1, keepdims=True) + NORM_EPS)) * gn_ref[...]
                o_ref[rows(cc), hs(h)] = (on * _silu(gz_ref[rows(cc), hs(h)])).astype(o_ref.dtype)
        for h in range(HEADS):
            s_ref[h] = states[h]
        return carry

    lax.fori_loop(0, tb // GDN_GROUP, group_body, 0)

    @pl.when(t == pl.num_programs(1) - 1)
    def _():
        ssm_ref[0] = s_ref[...]
        conv_ref[0] = tail[halo - (CONV_W - 1):halo, :]


def _gdn_prompt(qkv, gz, col, row, conv_w, gdn_norm_g, batch, seq):
    nt = seq // GDN_TB
    rows = batch * seq
    row_spec = lambda n: pl.BlockSpec((GDN_TB, n), lambda b, t: (b * nt + t, 0))
    return pl.pallas_call(
        _gdn_kernel,
        grid=(batch, nt),
        in_specs=[row_spec(CONV_DIM), row_spec(WIDTH), row_spec(LANES),
                  pl.BlockSpec((1, 2 * SUBLANES, GDN_TB), lambda b, t: (b, 0, t)),
                  pl.BlockSpec((CONV_W, CONV_DIM), lambda b, t: (0, 0)),
                  pl.BlockSpec((1, HEAD_DIM), lambda b, t: (0, 0))],
        out_specs=[row_spec(WIDTH),
                   pl.BlockSpec((1, HEADS, HEAD_DIM, HEAD_DIM), lambda b, t: (b, 0, 0, 0)),
                   pl.BlockSpec((1, CONV_W - 1, CONV_DIM), lambda b, t: (b, 0, 0))],
        out_shape=[jax.ShapeDtypeStruct((rows, WIDTH), BF16),
                   jax.ShapeDtypeStruct((batch, HEADS, HEAD_DIM, HEAD_DIM), F32),
                   jax.ShapeDtypeStruct((batch, CONV_W - 1, CONV_DIM), F32)],
        scratch_shapes=[pltpu.VMEM((HEADS, HEAD_DIM, HEAD_DIM), F32),
                        pltpu.VMEM((GDN_TB + SUBLANES, CONV_DIM), F32),
                        pltpu.VMEM((GDN_TB, WIDTH), F32),
                        pltpu.VMEM((GDN_TB, WIDTH), F32),
                        pltpu.VMEM((GDN_TB, WIDTH), F32)],
        compiler_params=_params(2),
        name="gdn_prompt",
    )(qkv, gz, col, row, conv_w, gdn_norm_g)


def _att_step(qi, kj, q_ref, k_ref, v_ref, ck_ref, cq_ref, gate_ref, o_ref, m_ref, l_ref, acc_ref, cqb_ref):
    tq = tk = ATT_T
    heads = range(HEADS)
    hs = lambda h: slice(h * HEAD_DIM, (h + 1) * HEAD_DIM)

    @pl.when(kj == 0)
    def _():
        m_ref[...] = jnp.full_like(m_ref, NEG_BIG)
        l_ref[...] = jnp.zeros_like(l_ref)
        acc_ref[...] = jnp.zeros_like(acc_ref)
        for h in heads:
            cqb_ref[h] = jnp.broadcast_to(cq_ref[:, LANE_F + h:LANE_F + h + 1] * LOG2E, (tq, LANES))

    def update(diagonal):
        ss = [lax.dot_general(q_ref[:, hs(h)], k_ref[:, hs(h)], (((1,), (1,)), ((), ())),
                              preferred_element_type=F32) for h in heads]
        tts = [s - ck_ref[0, LANE_F + h:LANE_F + h + 1, :] * LOG2E for h, s in zip(heads, ss)]
        if diagonal:
            visible = (lax.broadcasted_iota(jnp.int32, (tq, tk), 1)
                       <= lax.broadcasted_iota(jnp.int32, (tq, tk), 0))
            tts = [jnp.where(visible, tt, NEG_BIG) for tt in tts]
        cqs = [cqb_ref[h] for h in heads]
        m_olds = [m_ref[h] for h in heads]
        m_news = [jnp.maximum(m_old, jnp.broadcast_to(jnp.max(tt, axis=1, keepdims=True), (tq, LANES)) + cq)
                  for m_old, tt, cq in zip(m_olds, tts, cqs)]
        shifts = [jnp.concatenate([m_new - cq] * (tk // LANES), axis=1) for m_new, cq in zip(m_news, cqs)]
        ps = [jnp.exp2(tt - shift).astype(BF16) for tt, shift in zip(tts, shifts)]
        ones = jnp.ones((tk, HEAD_DIM), BF16)
        pvs = [jnp.dot(p, jnp.concatenate([v_ref[:, hs(h)], ones], axis=1), preferred_element_type=F32)
               for h, p in zip(heads, ps)]
        alphas = [jnp.exp2(m_old - m_new) for m_old, m_new in zip(m_olds, m_news)]
        accs = [alpha * acc_ref[h] + pv[:, 0:HEAD_DIM] for h, alpha, pv in zip(heads, alphas, pvs)]
        ls = [alpha * l_ref[h] + pv[:, HEAD_DIM:2 * HEAD_DIM] for h, alpha, pv in zip(heads, alphas, pvs)]
        return m_news, ls, accs

    @pl.when(kj < qi)
    def _():
        m_news, ls, accs = update(False)
        for h in heads:
            m_ref[h] = m_news[h]
            l_ref[h] = ls[h]
            acc_ref[h] = accs[h]

    @pl.when(kj == qi)
    def _():
        _, ls, accs = update(True)
        for h in heads:
            o_ref[:, hs(h)] = ((accs[h] / ls[h]) * gate_ref[:, hs(h)].astype(F32)).astype(o_ref.dtype)


def _outproj_kernel(x_ref, oa_ref, ob_ref, gate_ref, w_ref, y_ref):
    mix = (jnp.dot(oa_ref[...].astype(BF16), w_ref[0:WIDTH, :], preferred_element_type=F32)
           + jnp.dot(ob_ref[...].astype(BF16), w_ref[WIDTH:2 * WIDTH, :], preferred_element_type=F32))
    y_ref[...] = x_ref[...] + gate_ref[0, 0] * mix


def _outproj(x2d, oa, ob, mod4, per_row_mod, tm, w_out_b):
    rows = x2d.shape[0]
    nblk = rows // tm
    if per_row_mod:
        mod_block = (1, 1, tm, D_MODEL)
        g_map = lambda i: (2, 0, i, 0)
    else:
        blocks_per_batch = nblk // mod4.shape[1]
        mod_block = (1, 1, 1, D_MODEL)
        g_map = lambda i: (2, i // blocks_per_batch, 0, 0)
    row_spec = lambda n: pl.BlockSpec((tm, n), lambda i: (i, 0))
    return pl.pallas_call(
        _outproj_kernel,
        grid=(nblk,),
        in_specs=[row_spec(D_MODEL), row_spec(WIDTH), row_spec(WIDTH),
                  pl.BlockSpec(mod_block, g_map),
                  pl.BlockSpec((2 * WIDTH, D_MODEL), lambda i: (0, 0))],
        out_specs=row_spec(D_MODEL),
        out_shape=jax.ShapeDtypeStruct((rows, D_MODEL), F32),
        compiler_params=_params(1),
        name="out_proj",
    )(x2d, oa, ob, mod4, w_out_b)


def _smid_kernel(u_ref, s0_ref, s1_ref, s2_ref, cw_ref, sm_ref, par_ref,
                 qn_ref, kn_ref, vn_ref, sc_ref):
    conv = (s0_ref[...] * cw_ref[0:1, :] + s1_ref[...] * cw_ref[1:2, :]
            + s2_ref[...] * cw_ref[2:3, :] + u_ref[...] * cw_ref[3:4, :])
    act = _silu(conv)
    for h in range(HEADS):
        hs = slice(h * HEAD_DIM, (h + 1) * HEAD_DIM)
        q = act[:, h * HEAD_DIM:(h + 1) * HEAD_DIM]
        k = act[:, WIDTH + h * HEAD_DIM:WIDTH + (h + 1) * HEAD_DIM]
        qn_ref[:, hs] = q * lax.rsqrt(jnp.sum(q * q, axis=-1, keepdims=True) + NORM_EPS)
        kn_ref[:, hs] = k * lax.rsqrt(jnp.sum(k * k, axis=-1, keepdims=True) + NORM_EPS)
    vn_ref[...] = act[:, 2 * WIDTH:3 * WIDTH]
    beta, g, lf = _token_scalars(sm_ref[...], par_ref[...])
    lane = lax.broadcasted_iota(jnp.int32, sm_ref.shape, 1)
    sc_ref[...] = jnp.where(lane < LANE_G, beta, jnp.where(lane < LANE_F, g, lf))


def _sample_mid(u, s0, s1, s2, conv_w, small, par):
    nb = u.shape[0]
    full = lambda a: pl.BlockSpec(a.shape, lambda: (0,) * a.ndim)
    args = (u, s0, s1, s2, conv_w, small, par)
    return pl.pallas_call(
        _smid_kernel,
        in_specs=[full(a) for a in args],
        out_specs=[pl.BlockSpec((nb, WIDTH), lambda: (0, 0))] * 3 + [pl.BlockSpec((nb, LANES), lambda: (0, 0))],
        out_shape=[jax.ShapeDtypeStruct((nb, WIDTH), F32)] * 3 + [jax.ShapeDtypeStruct((nb, LANES), F32)],
        compiler_params=pltpu.CompilerParams(vmem_limit_bytes=VMEM_LIMIT),
        name="sample_conv",
    )(*args)


def _gdn_step_kernel(t_ref, s_ref, gz_ref, gn_ref, o_ref, sn_ref):
    scale = HEAD_DIM ** -0.5
    row = lax.broadcasted_iota(jnp.int32, (SUBLANES, HEAD_DIM), 0)
    items = [(b, h) for b in range(STEP_BATCH) for h in range(HEADS)]
    ts = [t_ref[b, h] for b, h in items]
    rs = [_mm(t, s_ref[b, h]) for t, (b, h) in zip(ts, items)]
    egs, v_news = [], []
    for t, r, (b, h) in zip(ts, rs, items):
        k, q, v, beta, g = (t[i:i + 1, :] for i in range(5))
        eg = jnp.exp(g)
        v_new = beta * (v - eg * r[0:1, :])
        o = scale * (eg * r[1:2, :] + jnp.sum(q * k, axis=-1, keepdims=True) * v_new)
        on = (o * lax.rsqrt(jnp.mean(o * o, axis=-1, keepdims=True) + NORM_EPS)) * gn_ref[...]
        o_ref[b, h:h + 1, :] = on * _silu(gz_ref[b, h:h + 1, :])
        egs.append(eg)
        v_news.append(v_new)
    outers = [_mm_tn(jnp.where(row == 0, t, 0.0),
                     jnp.where(row == 0, jnp.broadcast_to(v_new, (SUBLANES, HEAD_DIM)), 0.0))
              for t, v_new in zip(ts, v_news)]
    for eg, outer, (b, h) in zip(egs, outers, items):
        sn_ref[b, h] = s_ref[b, h] * eg + outer


def _gdn_step(t8, state, gz3, gdn_norm_g):
    nb = state.shape[0]
    sb = STEP_BATCH
    return pl.pallas_call(
        _gdn_step_kernel,
        grid=(nb // sb,),
        in_specs=[pl.BlockSpec((sb, HEADS, SUBLANES, HEAD_DIM), lambda b: (b, 0, 0, 0)),
                  pl.BlockSpec((sb, HEADS, HEAD_DIM, HEAD_DIM), lambda b: (b, 0, 0, 0)),
                  pl.BlockSpec((sb, HEADS, HEAD_DIM), lambda b: (b, 0, 0)),
                  pl.BlockSpec((1, HEAD_DIM), lambda b: (0, 0))],
        out_specs=[pl.BlockSpec((sb, HEADS, HEAD_DIM), lambda b: (b, 0, 0)),
                   pl.BlockSpec((sb, HEADS, HEAD_DIM, HEAD_DIM), lambda b: (b, 0, 0, 0))],
        out_shape=[jax.ShapeDtypeStruct((nb, HEADS, HEAD_DIM), F32),
                   jax.ShapeDtypeStruct((nb, HEADS, HEAD_DIM, HEAD_DIM), F32)],
        compiler_params=_params(1),
        name="gdn_step",
    )(t8, state, gz3, gdn_norm_g)


def _class_reduce(x, op):
    shift = HEADS
    while shift < LANES:
        x = op(x, pltpu.roll(x, shift, axis=1))
        shift *= 2
    return x


def _lane_to_col(v, eye8):
    return jnp.sum(jnp.where(eye8, jnp.broadcast_to(v, (SUBLANES, LANES)), 0.0), axis=1, keepdims=True)


def _dec_sig(r, a):
    return (DEC_PAGES - 1 - r) * HEADS + a


def _dec_group_copies(pt_ref, hbm, bufs, sems, seq, g, slot):
    (ck_hbm, cv_hbm, clf_hbm), (kbuf, vbuf, lfbuf) = hbm, bufs
    n_pages = pt_ref.shape[1]
    out = []
    for r in range(DEC_PAGES):
        page = pt_ref[seq, n_pages - 1 - (g * DEC_PAGES + r)]
        row0 = pl.multiple_of(page * PAGE_ROWS, PAGE_ROWS)
        out.append(pltpu.make_async_copy(ck_hbm.at[pl.ds(row0, PAGE_ROWS), :], kbuf.at[slot, r], sems.at[0, slot]))
        out.append(pltpu.make_async_copy(cv_hbm.at[pl.ds(row0, PAGE_ROWS), :], vbuf.at[slot, r], sems.at[1, slot]))
        out.append(pltpu.make_async_copy(clf_hbm.at[page], lfbuf.at[slot, pl.ds(_dec_sig(r, 0), HEADS), :],
                                         sems.at[2, slot]))
    return out


def _dec_start(b, pt_ref, hbm, bufs, sems):
    @pl.when(b == 0)
    def _():
        for c in _dec_group_copies(pt_ref, hbm, bufs, sems, 0, 0, 0):
            c.start()

    for c in _dec_group_copies(pt_ref, hbm, bufs, sems, b, 1, 1):
        c.start()


def _dec_finish(b, nb, pt_ref, q8_ref, kn_ref, vn_ref, gate_ref, lfn_ref, hbm, o_ref, bufs, sems, sd_ref):
    kbuf, vbuf, lfbuf = bufs
    n_pages = pt_ref.shape[1]
    gp = DEC_PAGES
    ng = n_pages // gp
    nrow = gp * HEADS
    sig_of = _dec_sig
    group_copies = functools.partial(_dec_group_copies, pt_ref, hbm, bufs, sems)

    ri = lax.broadcasted_iota(jnp.int32, (LANES, LANES), 0)
    ci = lax.broadcasted_iota(jnp.int32, (LANES, LANES), 1)
    eye = ri == ci
    same_head = (ri & (HEADS - 1)) == (ci & (HEADS - 1))
    eye8 = (lax.broadcasted_iota(jnp.int32, (SUBLANES, LANES), 0)
            == lax.broadcasted_iota(jnp.int32, (SUBLANES, LANES), 1))
    si = lax.broadcasted_iota(jnp.int32, (nrow, nrow), 0)
    sj = lax.broadcasted_iota(jnp.int32, (nrow, nrow), 1)
    ones = jnp.ones((LANES, LANES), BF16)
    qrep = jnp.concatenate([q8_ref[0]] * (LANES // Q_ROWS), axis=0)

    m_run = jnp.full((1, LANES), NEG_BIG, F32)
    l_run = jnp.zeros((1, LANES), F32)
    acc = jnp.zeros((SUBLANES, HEAD_DIM), F32)
    carry = lfn_ref[0]
    for g in range(ng):
        slot = g % 2
        if 0 < g < ng - 1:
            for c in group_copies(b, g + 1, 1 - slot):
                c.start()
        elif g == ng - 1:
            @pl.when(b + 1 < nb)
            def _():
                for c in group_copies(b + 1, 0, 1 - slot):
                    c.start()
        for c in group_copies(b, g, slot):
            c.wait()

        lf = lfbuf[slot]
        within = _mm_exact_lhs(lf, _ones_mask(same_head & (ri > ci)))
        tot = _mm_exact_lhs(lf, _ones_mask(same_head))
        later = _mm_exact_rhs(_ones_mask(sj > si), tot)
        bias = (within + later + carry) * LOG2E
        carry = carry + jnp.sum(tot, axis=0, keepdims=True)

        for r in range(gp):
            s2 = _mm_nt(kbuf[slot, r], qrep)
            for a in range(HEADS):
                blk = s2[a * LANES:(a + 1) * LANES, :]
                sig = sig_of(r, a)
                sd_ref[sig:sig + 1, :] = jnp.sum(jnp.where(eye, blk, 0.0), axis=0, keepdims=True)
        for sub in range(gp // DEC_SUB):
            pages = range(sub * DEC_SUB, (sub + 1) * DEC_SUB)
            lo = sig_of(pages[-1], 0)
            rows = slice(lo, lo + DEC_SUB * HEADS)
            tt = sd_ref[rows, :] + bias[rows, :]
            mx = jnp.max(tt, axis=0, keepdims=True)
            mx = _class_reduce(jnp.broadcast_to(mx, (SUBLANES, LANES)), jnp.maximum)[0:1, :]
            m_new = jnp.maximum(m_run, mx)
            pf = jnp.exp2(tt - m_new).astype(BF16).astype(F32)
            alpha = jnp.exp2(m_run - m_new)
            l_run = alpha * l_run + jnp.sum(pf, axis=0, keepdims=True)
            m_run = m_new
            acc = acc * _lane_to_col(alpha, eye8)
            for r in pages:
                parts = []
                for a in range(HEADS):
                    sig = sig_of(r, a) - lo
                    prow = jnp.broadcast_to(pf[sig:sig + 1, :], (LANES, LANES))
                    parts.append(jnp.where(eye, prow, 0.0).astype(BF16))
                pcol = jnp.dot(jnp.concatenate(parts, axis=0), ones, preferred_element_type=F32)
                pv = pcol * vbuf[slot, r]
                acc = acc + jnp.sum(pv.reshape(PAGE_ROWS // SUBLANES, SUBLANES, LANES), axis=0)

    acc4 = acc[0:HEADS, :] + acc[HEADS:2 * HEADS, :]
    mcol = _lane_to_col(m_run, eye8)[0:HEADS, :]
    lsum = _class_reduce(jnp.broadcast_to(l_run, (SUBLANES, LANES)), jnp.add)[0:1, :]
    lcol = _lane_to_col(lsum, eye8)[0:HEADS, :]
    q4 = q8_ref[0][0:HEADS, :].astype(F32)
    s_new = jnp.sum(q4 * kn_ref[0], axis=-1, keepdims=True)
    m_f = jnp.maximum(mcol, s_new)
    a1 = jnp.exp2(mcol - m_f)
    a2 = jnp.exp2(s_new - m_f)
    o = (acc4 * a1 + a2 * vn_ref[0]) / (lcol * a1 + a2)
    o_ref[0] = o * gate_ref[0].astype(F32)


def _fox_kernel(pt_ref, ab_ref, qi_ref, kj_ref,
                q8_ref, kn_ref, vn_ref, gate_ref, lfn_ref, ck_hbm, cv_hbm, clf_hbm,
                aq_ref, ak_ref, av_ref, ack_ref, acq_ref, agate_ref,
                o_ref, ao_ref,
                kbuf, vbuf, lfbuf, sems, sd_ref, m_ref, l_ref, acc_ref, cqb_ref, *, n_att):
    del ab_ref
    step = pl.program_id(0)
    n_seq = pt_ref.shape[0]
    hbm, bufs = (ck_hbm, cv_hbm, clf_hbm), (kbuf, vbuf, lfbuf)

    @pl.when(step < n_seq)
    def _():
        _dec_start(step, pt_ref, hbm, bufs, sems)

    @pl.when(step < n_att)
    def _():
        _att_step(qi_ref[step], kj_ref[step], aq_ref, ak_ref, av_ref, ack_ref, acq_ref, agate_ref, ao_ref,
                  m_ref, l_ref, acc_ref, cqb_ref)

    @pl.when(step < n_seq)
    def _():
        _dec_finish(step, n_seq, pt_ref, q8_ref, kn_ref, vn_ref, gate_ref, lfn_ref, hbm, o_ref, bufs, sems, sd_ref)


def _fox_attention(page_table, q8, k_new, v_new, gate, lf_new, cache_k2, cache_v2, cache_lf3,
                   aq, ak, av, row, col, agate, batch, seq):
    nb, n_pages = page_table.shape
    gp = DEC_PAGES
    assert n_pages % (2 * gp) == 0, "an even number of page groups per sequence keeps the buffer slots static"
    n = seq // ATT_T
    steps = [(bb, i, j) for bb in range(batch) for i in range(n) for j in range(i + 1)]
    n_att = len(steps)
    n_steps = max(nb, n_att)
    steps = steps + [steps[-1]] * (n_steps - n_att)
    ab_tab, qi_tab, kj_tab = (jnp.asarray([s[k] for s in steps], jnp.int32) for k in range(3))

    per_b = lambda s, pt, ab, qi, kj: (jnp.minimum(s, nb - 1), 0, 0)
    qmap = lambda s, pt, ab, qi, kj: (ab[s] * n + qi[s], 0)
    kmap = lambda s, pt, ab, qi, kj: (ab[s] * n + kj[s], 0)
    hbm = pl.BlockSpec(memory_space=pl.ANY)
    grid_spec = pltpu.PrefetchScalarGridSpec(
        num_scalar_prefetch=4,
        grid=(n_steps,),
        in_specs=[pl.BlockSpec((1, Q_ROWS, HEAD_DIM), per_b),
                  pl.BlockSpec((1, HEADS, HEAD_DIM), per_b),
                  pl.BlockSpec((1, HEADS, HEAD_DIM), per_b),
                  pl.BlockSpec((1, HEADS, HEAD_DIM), per_b),
                  pl.BlockSpec((1, 1, LANES), per_b),
                  hbm, hbm, hbm,
                  pl.BlockSpec((ATT_T, WIDTH), qmap),
                  pl.BlockSpec((ATT_T, WIDTH), kmap),
                  pl.BlockSpec((ATT_T, WIDTH), kmap),
                  pl.BlockSpec((1, 2 * SUBLANES, ATT_T), lambda s, pt, ab, qi, kj: (ab[s], 0, kj[s])),
                  pl.BlockSpec((ATT_T, LANES), qmap),
                  pl.BlockSpec((ATT_T, WIDTH), qmap)],
        out_specs=[pl.BlockSpec((1, HEADS, HEAD_DIM), per_b),
                   pl.BlockSpec((ATT_T, WIDTH), qmap)],
        scratch_shapes=[pltpu.VMEM((2, gp, PAGE_ROWS, HEAD_DIM), F32),
                        pltpu.VMEM((2, gp, PAGE_ROWS, HEAD_DIM), F32),
                        pltpu.VMEM((2, gp * HEADS, LANES), F32),
                        pltpu.SemaphoreType.DMA((3, 2)),
                        pltpu.VMEM((gp * HEADS, LANES), F32)]
                       + [pltpu.VMEM((HEADS, ATT_T, LANES), F32)] * 4)
    return pl.pallas_call(
        functools.partial(_fox_kernel, n_att=n_att),
        grid_spec=grid_spec,
        out_shape=[jax.ShapeDtypeStruct((nb, HEADS, HEAD_DIM), F32),
                   jax.ShapeDtypeStruct((batch * seq, WIDTH), BF16)],
        compiler_params=pltpu.CompilerParams(dimension_semantics=("arbitrary",),
                                             vmem_limit_bytes=DEC_VMEM_LIMIT),
        name="fox_attention",
    )(page_table, ab_tab, qi_tab, kj_tab, q8, k_new, v_new, gate, lf_new, cache_k2, cache_v2, cache_lf3,
      aq, ak, av, row, col, agate)


def _layer(x_prompt, x_sample, cache_k, cache_v, cache_logf, state_ssm, state_conv, page_table,
           c_prompt, c_sample, norm_g, w_ada, b_ada, w_in, b_f, conv_w, a_log, dt_bias,
           gdn_norm_g, q_norm_g, k_norm_g, w_out):
    batch, seq, _ = x_prompt.shape
    nb = x_sample.shape[0]
    w = WIDTH
    o_small = 4 * w
    o_fox = o_small + 2 * HEADS
    w_big = jnp.concatenate([w_in[:, 0:o_small], w_in[:, o_fox:o_fox + 4 * w]], axis=1).astype(BF16)
    w_small = jnp.concatenate([w_in[:, o_small:o_fox], w_in[:, o_fox + 4 * w:],
                               jnp.zeros((D_MODEL, LANES - 3 * HEADS), F32)], axis=1).astype(BF16)
    w_out_b = w_out.astype(BF16)
    par = jnp.zeros((SUBLANES, LANES), F32)
    par = par.at[0, LANE_G:LANE_G + HEADS].set(a_log)
    par = par.at[1, LANE_G:LANE_G + HEADS].set(dt_bias)
    par = par.at[2, LANE_F:LANE_F + HEADS].set(b_f)
    ng2, qg2, kg2, gg2 = (a.reshape(1, -1) for a in (norm_g, q_norm_g, k_norm_g, gdn_norm_g))

    cp8 = jnp.pad(c_prompt, ((0, SUBLANES - batch), (0, 0)))
    mod_p, mod_s = _ada_mod(cp8, c_sample, w_ada, b_ada.reshape(1, -1))
    mod_p4 = mod_p[:, 0:batch].reshape(3, batch, 1, D_MODEL)
    mod_s4 = mod_s.reshape(3, 1, nb, D_MODEL)

    xp = x_prompt.reshape(batch * seq, D_MODEL)
    qkv, gz, small, fq, fk, fkb, fv, fvb, fzg = _inproj(xp, mod_p4, False, PROMPT_TM, ng2, w_big, w_small, qg2, kg2)
    col, row = _small_prep(small, par, batch, seq)
    o_a, ssm_p, conv_p = _gdn_prompt(qkv, gz, col, row, conv_w, gg2, batch, seq)
    k_p = fk.reshape(batch, seq, HEADS, HEAD_DIM)
    v_p = fv.reshape(batch, seq, HEADS, HEAD_DIM)
    lf_p = col[:, LANE_LF:LANE_LF + HEADS].reshape(batch, seq, HEADS)

    xs = x_sample.reshape(nb, D_MODEL)
    qkv_s, gz_s, small_s, fq_s, fk_s, _, fv_s, _, fzg_s = _inproj(xs, mod_s4, True, nb, ng2, w_big, w_small, qg2, kg2)
    qn_s, kn_s, vn_s, sc_s = _sample_mid(qkv_s, state_conv[:, 0], state_conv[:, 1], state_conv[:, 2],
                                         conv_w, small_s, par)
    h3 = lambda a: a.reshape(nb, HEADS, HEAD_DIM)
    bcast = lambda a: jnp.broadcast_to(a[:, :, None], (nb, HEADS, HEAD_DIM))
    t8 = jnp.stack([h3(kn_s), h3(qn_s), h3(vn_s), bcast(sc_s[:, LANE_B:LANE_B + HEADS]),
                    bcast(sc_s[:, LANE_G:LANE_G + HEADS])]
                   + [jnp.zeros((nb, HEADS, HEAD_DIM), F32)] * (SUBLANES - 5), axis=2)
    o_a_s, ssm_s = _gdn_step(t8, state_ssm, h3(gz_s), gg2)
    lf_s = sc_s[:, LANE_F:LANE_F + HEADS]
    q8 = jnp.tile(h3(fq_s), (1, Q_ROWS // HEADS, 1))
    lf_new = jnp.tile(lf_s, (1, LANES // HEADS)).reshape(nb, 1, LANES)
    n_pool = cache_k.shape[0]
    o_b_s, o_b = _fox_attention(page_table, q8, h3(fk_s), h3(fv_s), h3(fzg_s), lf_new,
                                cache_k.reshape(n_pool * PAGE_ROWS, HEAD_DIM),
                                cache_v.reshape(n_pool * PAGE_ROWS, HEAD_DIM),
                                cache_logf.reshape(n_pool, HEADS, LANES),
                                fq, fkb, fvb, row, col, fzg, batch, seq)
    y_p = _outproj(xp, o_a, o_b, mod_p4, False, OUT_TM, w_out_b).reshape(batch, seq, D_MODEL)
    y_s = _outproj(xs, o_a_s.reshape(nb, w), o_b_s.reshape(nb, w), mod_s4, True, nb, w_out_b).reshape(nb, 1, D_MODEL)
    conv_s = jnp.stack([state_conv[:, 1], state_conv[:, 2], qkv_s], axis=1)
    return (y_p, y_s, k_p, v_p, lf_p, ssm_p, conv_p,
            h3(fk_s).reshape(nb, 1, HEADS, HEAD_DIM), h3(fv_s).reshape(nb, 1, HEADS, HEAD_DIM),
            lf_s.reshape(nb, 1, HEADS), ssm_s, conv_s)


def kernel(x_prompt, x_sample, cache_k, cache_v, cache_logf, state_ssm, state_conv, page_table, c_prompt, c_sample, norm_g, w_ada, b_ada, w_in, b_f, conv_w, a_log, dt_bias, gdn_norm_g, q_norm_g, k_norm_g, w_out):
    assert w_ada.shape[0] == 1, "single layer"
    outs = _layer(x_prompt, x_sample, cache_k[0], cache_v[0], cache_logf[0], state_ssm[0], state_conv[0],
                  page_table, c_prompt, c_sample, norm_g[0], w_ada[0], b_ada[0], w_in[0], b_f[0], conv_w[0],
                  a_log[0], dt_bias[0], gdn_norm_g[0], q_norm_g[0], k_norm_g[0], w_out[0])
    y_p, y_s = outs[0], outs[1]
    return (y_p, y_s) + tuple(o[None] for o in outs[2:])
```

```python
import functools

import jax
import jax.numpy as jnp
from jax import lax
from jax.experimental import pallas as pl
from jax.experimental.pallas import tpu as pltpu

F32 = jnp.float32
BF16 = jnp.bfloat16

D_MODEL = 1024
HEADS = 4
HEAD_DIM = 128
WIDTH = HEADS * HEAD_DIM
CONV_W = 4
CONV_DIM = 3 * WIDTH
CHUNK = 64
PAGE_SIZE = 128
PAGE_ROWS = PAGE_SIZE * HEADS
NORM_EPS = 1e-6
LANES = 128
SUBLANES = 8
NEG_BIG = -1e30
LOG2E = 1.4426950408889634
ATT_Q_SCALE = LOG2E * HEAD_DIM ** -0.5
VMEM_LIMIT = 48 * 1024 * 1024

COL_GQKV = 0
COL_GZ = 3 * WIDTH
COL_FQ = 4 * WIDTH
COL_FK = 5 * WIDTH
COL_FV = 6 * WIDTH
COL_FZ = 7 * WIDTH
BIG_COLS = 8 * WIDTH
LANE_B, LANE_G, LANE_F, LANE_LF = 0, 4, 8, 12

PROMPT_TM = 256
OUT_TM = 512
SMALL_TB = 512
GDN_TB = 128
ATT_T = 512
DEC_PAGES = 32
DEC_SUB = 16
DEC_VMEM_LIMIT = 56 * 1024 * 1024
Q_ROWS = 16
STEP_BATCH = 4


def _mm(a, b):
    return jnp.dot(a.astype(BF16), b.astype(BF16), preferred_element_type=F32)


def _mm_nt(a, b):
    return lax.dot_general(a.astype(BF16), b.astype(BF16), (((1,), (1,)), ((), ())),
                           preferred_element_type=F32)


def _mm_tn(a, b):
    return lax.dot_general(a.astype(BF16), b.astype(BF16), (((0,), (0,)), ((), ())),
                           preferred_element_type=F32)


def _mm_exact_rhs(mat_b, y):
    hi = y.astype(BF16)
    r1 = y - hi.astype(F32)
    mid = r1.astype(BF16)
    lo = (r1 - mid.astype(F32)).astype(BF16)
    dot = functools.partial(jnp.dot, preferred_element_type=F32)
    return dot(mat_b, hi) + dot(mat_b, mid) + dot(mat_b, lo)


def _mm_exact_lhs(y, mat_b):
    hi = y.astype(BF16)
    r1 = y - hi.astype(F32)
    mid = r1.astype(BF16)
    lo = (r1 - mid.astype(F32)).astype(BF16)
    dot = functools.partial(jnp.dot, preferred_element_type=F32)
    return dot(hi, mat_b) + dot(mid, mat_b) + dot(lo, mat_b)


def _sigmoid(x):
    return 1.0 / (1.0 + jnp.exp(-x))


def _silu(x):
    return x * _sigmoid(x)


def _softplus(x):
    return jnp.maximum(x, 0.0) + jnp.log(1.0 + jnp.exp(-jnp.abs(x)))


def _ones_mask(mask):
    return jnp.where(mask, 1.0, 0.0).astype(BF16)


def _params(n_axes):
    return pltpu.CompilerParams(dimension_semantics=("arbitrary",) * n_axes,
                                vmem_limit_bytes=VMEM_LIMIT)


def _ada_kernel(cp_ref, cs_ref, w_ref, b_ref, op_ref, os_ref):
    w = w_ref[...].astype(BF16)
    b = b_ref[...]
    op_ref[0] = jnp.dot(_silu(cp_ref[...]).astype(BF16), w, preferred_element_type=F32) + b
    os_ref[0] = jnp.dot(_silu(cs_ref[...]).astype(BF16), w, preferred_element_type=F32) + b


def _ada_mod(c_prompt8, c_sample, w_ada, b_ada):
    nb = c_sample.shape[0]
    return pl.pallas_call(
        _ada_kernel,
        grid=(3,),
        in_specs=[pl.BlockSpec((SUBLANES, D_MODEL), lambda k: (0, 0)),
                  pl.BlockSpec((nb, D_MODEL), lambda k: (0, 0)),
                  pl.BlockSpec((D_MODEL, D_MODEL), lambda k: (0, k)),
                  pl.BlockSpec((1, D_MODEL), lambda k: (0, k))],
        out_specs=[pl.BlockSpec((1, SUBLANES, D_MODEL), lambda k: (k, 0, 0)),
                   pl.BlockSpec((1, nb, D_MODEL), lambda k: (k, 0, 0))],
        out_shape=[jax.ShapeDtypeStruct((3, SUBLANES, D_MODEL), F32),
                   jax.ShapeDtypeStruct((3, nb, D_MODEL), F32)],
        compiler_params=_params(1),
        name="ada_mod",
    )(c_prompt8, c_sample, w_ada, b_ada)


def _inproj_kernel(x_ref, sh_ref, sc_ref, ng_ref, wb_ref, ws_ref, qg_ref, kg_ref,
                   qkv_ref, gz_ref, sm_ref, fq_ref, fk_ref, fkb_ref, fv_ref, fvb_ref, fzg_ref):
    x = x_ref[...]
    y = x * lax.rsqrt(jnp.mean(x * x, axis=-1, keepdims=True) + NORM_EPS)
    hdn = (y * ng_ref[...]) * (1.0 + sc_ref[0, 0]) + sh_ref[0, 0]
    hb = hdn.astype(BF16)
    dot = functools.partial(jnp.dot, preferred_element_type=F32)
    qkv_ref[...] = dot(hb, wb_ref[:, COL_GQKV:COL_GZ])
    gz_ref[...] = dot(hb, wb_ref[:, COL_GZ:COL_FQ])
    sm_ref[...] = dot(hb, ws_ref[...])
    fq = dot(hb, wb_ref[:, COL_FQ:COL_FK])
    fk = dot(hb, wb_ref[:, COL_FK:COL_FV])
    fv = dot(hb, wb_ref[:, COL_FV:COL_FZ])
    fz = dot(hb, wb_ref[:, COL_FZ:BIG_COLS])
    fvb_ref[...] = fv.astype(BF16)
    fzg_ref[...] = _silu(fz).astype(BF16)
    scale = ATT_Q_SCALE
    tm = x.shape[0]
    for h in range(HEADS):
        sl = slice(h * HEAD_DIM, (h + 1) * HEAD_DIM)
        qh = fq[:, sl]
        qn = (qh * lax.rsqrt(jnp.mean(qh * qh, axis=-1, keepdims=True) + NORM_EPS)) * qg_ref[...]
        fq_ref[:, sl] = (qn * scale).astype(BF16)
        kh = fk[:, sl]
        kn = (kh * lax.rsqrt(jnp.mean(kh * kh, axis=-1, keepdims=True) + NORM_EPS)) * kg_ref[...]
        fkb_ref[:, sl] = kn.astype(BF16)
        fk_ref[pl.ds(h, tm, stride=HEADS), :] = kn
        fv_ref[pl.ds(h, tm, stride=HEADS), :] = fv[:, sl]


def _inproj(x2d, mod4, per_row_mod, tm, norm_g, w_big, w_small, q_norm_g, k_norm_g):
    rows = x2d.shape[0]
    nblk = rows // tm
    if per_row_mod:
        mod_block = (1, 1, tm, D_MODEL)
        sh_map = lambda i: (0, 0, i, 0)
        sc_map = lambda i: (1, 0, i, 0)
    else:
        blocks_per_batch = nblk // mod4.shape[1]
        mod_block = (1, 1, 1, D_MODEL)
        sh_map = lambda i: (0, i // blocks_per_batch, 0, 0)
        sc_map = lambda i: (1, i // blocks_per_batch, 0, 0)
    row_spec = lambda n: pl.BlockSpec((tm, n), lambda i: (i, 0))
    const = lambda shape: pl.BlockSpec(shape, lambda i: (0,) * len(shape))
    outs = [(CONV_DIM, F32, 1), (WIDTH, F32, 1), (LANES, F32, 1), (WIDTH, BF16, 1), (HEAD_DIM, F32, HEADS),
            (WIDTH, BF16, 1), (HEAD_DIM, F32, HEADS), (WIDTH, BF16, 1), (WIDTH, BF16, 1)]
    return pl.pallas_call(
        _inproj_kernel,
        grid=(nblk,),
        in_specs=[row_spec(D_MODEL),
                  pl.BlockSpec(mod_block, sh_map), pl.BlockSpec(mod_block, sc_map),
                  const((1, D_MODEL)), const((D_MODEL, BIG_COLS)), const((D_MODEL, LANES)),
                  const((1, HEAD_DIM)), const((1, HEAD_DIM))],
        out_specs=[pl.BlockSpec((tm * r, n), lambda i: (i, 0)) for n, _, r in outs],
        out_shape=[jax.ShapeDtypeStruct((rows * r, n), dt) for n, dt, r in outs],
        compiler_params=_params(1),
        name="in_proj",
    )(x2d, mod4, mod4, norm_g, w_big, w_small, q_norm_g, k_norm_g)


def _token_scalars(x, par):
    beta = _sigmoid(x)
    g = -jnp.exp(par[0:1, :]) * _softplus(x + par[1:2, :])
    lf = -_softplus(-(x + par[2:3, :]))
    return beta, g, lf


def _small_kernel(sm_ref, par_ref, col_ref, row_ref, carry_ref):
    tb = SMALL_TB

    @pl.when(pl.program_id(1) == 0)
    def _():
        carry_ref[...] = jnp.zeros_like(carry_ref)

    x = sm_ref[...]
    beta, g, lf = _token_scalars(x, par_ref[...])
    lane = lax.broadcasted_iota(jnp.int32, (tb, LANES), 1)
    is_b = lane < LANE_G
    is_g = (lane >= LANE_G) & (lane < LANE_F)
    is_f = (lane >= LANE_F) & (lane < LANE_LF)
    gm = jnp.where(is_g, g, 0.0)
    fm = jnp.where(is_f, lf, 0.0)
    r = lax.broadcasted_iota(jnp.int32, (tb, tb), 0)
    c = lax.broadcasted_iota(jnp.int32, (tb, tb), 1)
    tri = c <= r
    same_chunk = jnp.right_shift(r, 6) == jnp.right_shift(c, 6)
    gcum = _mm_exact_rhs(_ones_mask(tri & same_chunk), gm)
    cum = _mm_exact_rhs(_ones_mask(tri), fm) + carry_ref[...]
    cum = jnp.where(is_f, cum, 0.0)
    carry_ref[...] = cum[tb - 1:tb, :]
    col = jnp.where(is_b, beta, 0.0) + gcum + cum + pltpu.roll(fm, LANE_LF - LANE_F, axis=1)
    col_ref[...] = col
    for i in range(tb // LANES):
        t = col[i * LANES:(i + 1) * LANES, :]
        row_ref[0, :, i * LANES:(i + 1) * LANES] = t.T[0:2 * SUBLANES, :]


def _small_prep(small, par, batch, seq):
    nt = seq // SMALL_TB
    return pl.pallas_call(
        _small_kernel,
        grid=(batch, nt),
        in_specs=[pl.BlockSpec((SMALL_TB, LANES), lambda b, t: (b * nt + t, 0)),
                  pl.BlockSpec((SUBLANES, LANES), lambda b, t: (0, 0))],
        out_specs=[pl.BlockSpec((SMALL_TB, LANES), lambda b, t: (b * nt + t, 0)),
                   pl.BlockSpec((1, 2 * SUBLANES, SMALL_TB), lambda b, t: (b, 0, t))],
        out_shape=[jax.ShapeDtypeStruct((batch * seq, LANES), F32),
                   jax.ShapeDtypeStruct((batch, 2 * SUBLANES, seq), F32)],
        scratch_shapes=[pltpu.VMEM((1, LANES), F32)],
        compiler_params=_params(2),
        name="token_scalars",
    )(small, par)


def _unit_lower_inverses(ms, eye, ri, ci):
    base = 3
    same = jnp.right_shift(ri, base) == jnp.right_shift(ci, base)
    xps = [-jnp.where(same, m, 0.0) for m in ms]
    ps = [eye + xp for xp in xps]
    for _ in range(base - 1):
        xps = [_mm(xp, xp) for xp in xps]
        ps = [p + _mm(p, xp) for p, xp in zip(ps, xps)]
    size = base
    while (1 << size) < CHUNK:
        off = ((jnp.right_shift(ri, size + 1) == jnp.right_shift(ci, size + 1))
               & (jnp.right_shift(ri, size) != jnp.right_shift(ci, size)))
        pcs = [_mm(p, jnp.where(off, m, 0.0)) for p, m in zip(ps, ms)]
        ps = [p - _mm(pc, p) for p, pc in zip(ps, pcs)]
        size += 1
    return ps


def _gdn_chunk_preps(qs, ks, vs, betas, gcs, grs, incl, strict, eye, ri, ci):
    decays = [jnp.where(incl, jnp.exp(jnp.where(incl, gc - gr, 0.0)), 0.0) for gc, gr in zip(gcs, grs)]
    kbs = [k * beta for k, beta in zip(ks, betas)]
    boths = [_mm_nt(jnp.concatenate([kb, q], axis=0), k) for kb, q, k in zip(kbs, qs, ks)]
    ms = [jnp.where(strict, both[0:CHUNK] * decay, 0.0) for both, decay in zip(boths, decays)]
    a_intras = [both[CHUNK:2 * CHUNK] * decay for both, decay in zip(boths, decays)]
    tinvs = _unit_lower_inverses(ms, eye, ri, ci)
    egs = [jnp.exp(gc) for gc in gcs]
    uws = [_mm(tinv, jnp.concatenate([v * beta, kb * eg], axis=1))
           for tinv, v, beta, kb, eg in zip(tinvs, vs, betas, kbs, egs)]
    preps = []
    for uw, q, k, eg, gc, a_intra in zip(uws, qs, ks, egs, gcs, a_intras):
        g_last = gc[CHUNK - 1:CHUNK, :]
        wq = jnp.concatenate([uw[:, HEAD_DIM:2 * HEAD_DIM], q * eg], axis=0)
        preps.append((uw[:, 0:HEAD_DIM], wq, a_intra, k * jnp.exp(g_last - gc), jnp.exp(g_last)))
    return preps


def _gdn_chunk_steps(preps, states):
    boths = [_mm(p[1], s) for p, s in zip(preps, states)]
    v_news = [p[0] - both[0:CHUNK] for p, both in zip(preps, boths)]
    outs = [both[CHUNK:2 * CHUNK] + _mm(p[2], v_new) for p, both, v_new in zip(preps, boths, v_news)]
    new_states = [s * p[4] + _mm_tn(p[3], v_new) for p, s, v_new in zip(preps, states, v_news)]
    return outs, new_states


def _gdn_block(t, n_t, qkv_ref, gz_ref, col_ref, row_ref, cw_ref, gn_ref,
               o_ref, ssm_ref, conv_ref, s_ref, xbuf_ref, qn_ref, kn_ref, vn_ref):
    tb = GDN_TB
    halo = SUBLANES

    @pl.when(t == 0)
    def _():
        s_ref[...] = jnp.zeros_like(s_ref)
        xbuf_ref[0:halo, :] = jnp.zeros((halo, CONV_DIM), F32)

    xbuf_ref[halo:halo + tb, :] = qkv_ref[...]
    scale = HEAD_DIM ** -0.5
    for cb in range(CONV_DIM // LANES):
        sl = slice(cb * LANES, (cb + 1) * LANES)
        acc = xbuf_ref[halo - 3:halo - 3 + tb, sl] * cw_ref[0:1, sl]
        for i in range(1, CONV_W):
            acc = acc + xbuf_ref[halo - 3 + i:halo - 3 + i + tb, sl] * cw_ref[i:i + 1, sl]
        act = _silu(acc)
        part, h = divmod(cb, HEADS)
        hs = slice(h * LANES, (h + 1) * LANES)
        if part == 0:
            qn_ref[:, hs] = act * lax.rsqrt(jnp.sum(act * act, axis=-1, keepdims=True) + NORM_EPS) * scale
        elif part == 1:
            kn_ref[:, hs] = act * lax.rsqrt(jnp.sum(act * act, axis=-1, keepdims=True) + NORM_EPS)
        else:
            vn_ref[:, hs] = act
    tail = xbuf_ref[tb:tb + halo, :]
    xbuf_ref[0:halo, :] = tail

    ri = lax.broadcasted_iota(jnp.int32, (CHUNK, CHUNK), 0)
    ci = lax.broadcasted_iota(jnp.int32, (CHUNK, CHUNK), 1)
    incl = ci <= ri
    strict = ci < ri
    eye = jnp.where(ci == ri, 1.0, 0.0)

    colt = col_ref[...]
    rowt = row_ref[0]
    n_cc = tb // CHUNK
    items = [(cc, h) for cc in range(n_cc) for h in range(HEADS)]
    rs = lambda cc: slice(cc * CHUNK, (cc + 1) * CHUNK)
    hs = lambda h: slice(h * HEAD_DIM, (h + 1) * HEAD_DIM)
    preps = _gdn_chunk_preps(
        [qn_ref[rs(cc), hs(h)] for cc, h in items],
        [kn_ref[rs(cc), hs(h)] for cc, h in items],
        [vn_ref[rs(cc), hs(h)] for cc, h in items],
        [colt[rs(cc), LANE_B + h:LANE_B + h + 1] for cc, h in items],
        [colt[rs(cc), LANE_G + h:LANE_G + h + 1] for cc, h in items],
        [rowt[LANE_G + h:LANE_G + h + 1, rs(cc)] for cc, h in items],
        incl, strict, eye, ri, ci)
    states = [s_ref[h] for h in range(HEADS)]
    for cc in range(n_cc):
        outs, states = _gdn_chunk_steps(preps[cc * HEADS:(cc + 1) * HEADS], states)
        for h, o in enumerate(outs):
            on = (o * lax.rsqrt(jnp.mean(o * o, axis=-1, keepdims=True) + NORM_EPS)) * gn_ref[...]
            o_ref[rs(cc), hs(h)] = (on * _silu(gz_ref[rs(cc), hs(h)])).astype(o_ref.dtype)
    for h in range(HEADS):
        s_ref[h] = states[h]

    @pl.when(t == n_t - 1)
    def _():
        ssm_ref[0] = s_ref[...]
        conv_ref[0] = tail[halo - (CONV_W - 1):halo, :]


def _att_step(qi, kj, q_ref, k_ref, v_ref, ck_ref, cq_ref, gate_ref, o_ref, m_ref, l_ref, acc_ref, cqb_ref):
    tq = tk = ATT_T
    heads = range(HEADS)
    hs = lambda h: slice(h * HEAD_DIM, (h + 1) * HEAD_DIM)

    @pl.when(kj == 0)
    def _():
        m_ref[...] = jnp.full_like(m_ref, NEG_BIG)
        l_ref[...] = jnp.zeros_like(l_ref)
        acc_ref[...] = jnp.zeros_like(acc_ref)
        for h in heads:
            cqb_ref[h] = jnp.broadcast_to(cq_ref[:, LANE_F + h:LANE_F + h + 1] * LOG2E, (tq, LANES))

    def update(diagonal):
        ss = [lax.dot_general(q_ref[:, hs(h)], k_ref[:, hs(h)], (((1,), (1,)), ((), ())),
                              preferred_element_type=F32) for h in heads]
        tts = [s - ck_ref[0, LANE_F + h:LANE_F + h + 1, :] * LOG2E for h, s in zip(heads, ss)]
        if diagonal:
            visible = (lax.broadcasted_iota(jnp.int32, (tq, tk), 1)
                       <= lax.broadcasted_iota(jnp.int32, (tq, tk), 0))
            tts = [jnp.where(visible, tt, NEG_BIG) for tt in tts]
        cqs = [cqb_ref[h] for h in heads]
        m_olds = [m_ref[h] for h in heads]
        m_news = [jnp.maximum(m_old, jnp.broadcast_to(jnp.max(tt, axis=1, keepdims=True), (tq, LANES)) + cq)
                  for m_old, tt, cq in zip(m_olds, tts, cqs)]
        shifts = [jnp.concatenate([m_new - cq] * (tk // LANES), axis=1) for m_new, cq in zip(m_news, cqs)]
        ps = [jnp.exp2(tt - shift).astype(BF16) for tt, shift in zip(tts, shifts)]
        ones = jnp.ones((tk, HEAD_DIM), BF16)
        pvs = [jnp.dot(p, jnp.concatenate([v_ref[:, hs(h)], ones], axis=1), preferred_element_type=F32)
               for h, p in zip(heads, ps)]
        alphas = [jnp.exp2(m_old - m_new) for m_old, m_new in zip(m_olds, m_news)]
        accs = [alpha * acc_ref[h] + pv[:, 0:HEAD_DIM] for h, alpha, pv in zip(heads, alphas, pvs)]
        ls = [alpha * l_ref[h] + pv[:, HEAD_DIM:2 * HEAD_DIM] for h, alpha, pv in zip(heads, alphas, pvs)]
        return m_news, ls, accs

    @pl.when(kj < qi)
    def _():
        m_news, ls, accs = update(False)
        for h in heads:
            m_ref[h] = m_news[h]
            l_ref[h] = ls[h]
            acc_ref[h] = accs[h]

    @pl.when(kj == qi)
    def _():
        _, ls, accs = update(True)
        for h in heads:
            o_ref[:, hs(h)] = ((accs[h] / ls[h]) * gate_ref[:, hs(h)].astype(F32)).astype(o_ref.dtype)


def _outproj_kernel(x_ref, oa_ref, ob_ref, gate_ref, w_ref, y_ref):
    mix = (jnp.dot(oa_ref[...].astype(BF16), w_ref[0:WIDTH, :], preferred_element_type=F32)
           + jnp.dot(ob_ref[...].astype(BF16), w_ref[WIDTH:2 * WIDTH, :], preferred_element_type=F32))
    y_ref[...] = x_ref[...] + gate_ref[0, 0] * mix


def _outproj(x2d, oa, ob, mod4, per_row_mod, tm, w_out_b):
    rows = x2d.shape[0]
    nblk = rows // tm
    if per_row_mod:
        mod_block = (1, 1, tm, D_MODEL)
        g_map = lambda i: (2, 0, i, 0)
    else:
        blocks_per_batch = nblk // mod4.shape[1]
        mod_block = (1, 1, 1, D_MODEL)
        g_map = lambda i: (2, i // blocks_per_batch, 0, 0)
    row_spec = lambda n: pl.BlockSpec((tm, n), lambda i: (i, 0))
    return pl.pallas_call(
        _outproj_kernel,
        grid=(nblk,),
        in_specs=[row_spec(D_MODEL), row_spec(WIDTH), row_spec(WIDTH),
                  pl.BlockSpec(mod_block, g_map),
                  pl.BlockSpec((2 * WIDTH, D_MODEL), lambda i: (0, 0))],
        out_specs=row_spec(D_MODEL),
        out_shape=jax.ShapeDtypeStruct((rows, D_MODEL), F32),
        compiler_params=_params(1),
        name="out_proj",
    )(x2d, oa, ob, mod4, w_out_b)


def _smid_kernel(u_ref, s0_ref, s1_ref, s2_ref, cw_ref, sm_ref, par_ref,
                 qn_ref, kn_ref, vn_ref, sc_ref):
    conv = (s0_ref[...] * cw_ref[0:1, :] + s1_ref[...] * cw_ref[1:2, :]
            + s2_ref[...] * cw_ref[2:3, :] + u_ref[...] * cw_ref[3:4, :])
    act = _silu(conv)
    for h in range(HEADS):
        hs = slice(h * HEAD_DIM, (h + 1) * HEAD_DIM)
        q = act[:, h * HEAD_DIM:(h + 1) * HEAD_DIM]
        k = act[:, WIDTH + h * HEAD_DIM:WIDTH + (h + 1) * HEAD_DIM]
        qn_ref[:, hs] = q * lax.rsqrt(jnp.sum(q * q, axis=-1, keepdims=True) + NORM_EPS)
        kn_ref[:, hs] = k * lax.rsqrt(jnp.sum(k * k, axis=-1, keepdims=True) + NORM_EPS)
    vn_ref[...] = act[:, 2 * WIDTH:3 * WIDTH]
    beta, g, lf = _token_scalars(sm_ref[...], par_ref[...])
    lane = lax.broadcasted_iota(jnp.int32, sm_ref.shape, 1)
    sc_ref[...] = jnp.where(lane < LANE_G, beta, jnp.where(lane < LANE_F, g, lf))


def _sample_mid(u, s0, s1, s2, conv_w, small, par):
    nb = u.shape[0]
    full = lambda a: pl.BlockSpec(a.shape, lambda: (0,) * a.ndim)
    args = (u, s0, s1, s2, conv_w, small, par)
    return pl.pallas_call(
        _smid_kernel,
        in_specs=[full(a) for a in args],
        out_specs=[pl.BlockSpec((nb, WIDTH), lambda: (0, 0))] * 3 + [pl.BlockSpec((nb, LANES), lambda: (0, 0))],
        out_shape=[jax.ShapeDtypeStruct((nb, WIDTH), F32)] * 3 + [jax.ShapeDtypeStruct((nb, LANES), F32)],
        compiler_params=pltpu.CompilerParams(vmem_limit_bytes=VMEM_LIMIT),
        name="sample_conv",
    )(*args)


def _gdn_step_kernel(t_ref, s_ref, gz_ref, gn_ref, o_ref, sn_ref):
    scale = HEAD_DIM ** -0.5
    row = lax.broadcasted_iota(jnp.int32, (SUBLANES, HEAD_DIM), 0)
    items = [(b, h) for b in range(STEP_BATCH) for h in range(HEADS)]
    ts = [t_ref[b, h] for b, h in items]
    rs = [_mm(t, s_ref[b, h]) for t, (b, h) in zip(ts, items)]
    egs, v_news = [], []
    for t, r, (b, h) in zip(ts, rs, items):
        k, q, v, beta, g = (t[i:i + 1, :] for i in range(5))
        eg = jnp.exp(g)
        v_new = beta * (v - eg * r[0:1, :])
        o = scale * (eg * r[1:2, :] + jnp.sum(q * k, axis=-1, keepdims=True) * v_new)
        on = (o * lax.rsqrt(jnp.mean(o * o, axis=-1, keepdims=True) + NORM_EPS)) * gn_ref[...]
        o_ref[b, h:h + 1, :] = on * _silu(gz_ref[b, h:h + 1, :])
        egs.append(eg)
        v_news.append(v_new)
    outers = [_mm_tn(jnp.where(row == 0, t, 0.0),
                     jnp.where(row == 0, jnp.broadcast_to(v_new, (SUBLANES, HEAD_DIM)), 0.0))
              for t, v_new in zip(ts, v_news)]
    for eg, outer, (b, h) in zip(egs, outers, items):
        sn_ref[b, h] = s_ref[b, h] * eg + outer


def _gdn_step(t8, state, gz3, gdn_norm_g):
    nb = state.shape[0]
    sb = STEP_BATCH
    return pl.pallas_call(
        _gdn_step_kernel,
        grid=(nb // sb,),
        in_specs=[pl.BlockSpec((sb, HEADS, SUBLANES, HEAD_DIM), lambda b: (b, 0, 0, 0)),
                  pl.BlockSpec((sb, HEADS, HEAD_DIM, HEAD_DIM), lambda b: (b, 0, 0, 0)),
                  pl.BlockSpec((sb, HEADS, HEAD_DIM), lambda b: (b, 0, 0)),
                  pl.BlockSpec((1, HEAD_DIM), lambda b: (0, 0))],
        out_specs=[pl.BlockSpec((sb, HEADS, HEAD_DIM), lambda b: (b, 0, 0)),
                   pl.BlockSpec((sb, HEADS, HEAD_DIM, HEAD_DIM), lambda b: (b, 0, 0, 0))],
        out_shape=[jax.ShapeDtypeStruct((nb, HEADS, HEAD_DIM), F32),
                   jax.ShapeDtypeStruct((nb, HEADS, HEAD_DIM, HEAD_DIM), F32)],
        compiler_params=_params(1),
        name="gdn_step",
    )(t8, state, gz3, gdn_norm_g)


def _class_reduce(x, op):
    shift = HEADS
    while shift < LANES:
        x = op(x, pltpu.roll(x, shift, axis=1))
        shift *= 2
    return x


def _lane_to_col(v, eye8):
    return jnp.sum(jnp.where(eye8, jnp.broadcast_to(v, (SUBLANES, LANES)), 0.0), axis=1, keepdims=True)


def _dec_sig(r, a):
    return (DEC_PAGES - 1 - r) * HEADS + a


def _dec_group_copies(pt_ref, hbm, bufs, sems, seq, g, slot):
    (ck_hbm, cv_hbm, clf_hbm), (kbuf, vbuf, lfbuf) = hbm, bufs
    n_pages = pt_ref.shape[1]
    out = []
    for r in range(DEC_PAGES):
        page = pt_ref[seq, n_pages - 1 - (g * DEC_PAGES + r)]
        row0 = pl.multiple_of(page * PAGE_ROWS, PAGE_ROWS)
        out.append(pltpu.make_async_copy(ck_hbm.at[pl.ds(row0, PAGE_ROWS), :], kbuf.at[slot, r], sems.at[0, slot]))
        out.append(pltpu.make_async_copy(cv_hbm.at[pl.ds(row0, PAGE_ROWS), :], vbuf.at[slot, r], sems.at[1, slot]))
        out.append(pltpu.make_async_copy(clf_hbm.at[page], lfbuf.at[slot, pl.ds(_dec_sig(r, 0), HEADS), :],
                                         sems.at[2, slot]))
    return out


def _dec_start(b, pt_ref, hbm, bufs, sems):
    @pl.when(b == 0)
    def _():
        for c in _dec_group_copies(pt_ref, hbm, bufs, sems, 0, 0, 0):
            c.start()

    for c in _dec_group_copies(pt_ref, hbm, bufs, sems, b, 1, 1):
        c.start()


def _dec_finish(b, nb, pt_ref, q8_ref, kn_ref, vn_ref, gate_ref, lfn_ref, hbm, o_ref, bufs, sems, sd_ref,
                before_last_wait):
    kbuf, vbuf, lfbuf = bufs
    n_pages = pt_ref.shape[1]
    gp = DEC_PAGES
    ng = n_pages // gp
    nrow = gp * HEADS
    sig_of = _dec_sig
    group_copies = functools.partial(_dec_group_copies, pt_ref, hbm, bufs, sems)

    ri = lax.broadcasted_iota(jnp.int32, (LANES, LANES), 0)
    ci = lax.broadcasted_iota(jnp.int32, (LANES, LANES), 1)
    eye = ri == ci
    same_head = (ri & (HEADS - 1)) == (ci & (HEADS - 1))
    eye8 = (lax.broadcasted_iota(jnp.int32, (SUBLANES, LANES), 0)
            == lax.broadcasted_iota(jnp.int32, (SUBLANES, LANES), 1))
    si = lax.broadcasted_iota(jnp.int32, (nrow, nrow), 0)
    sj = lax.broadcasted_iota(jnp.int32, (nrow, nrow), 1)
    ones = jnp.ones((LANES, LANES), BF16)
    qrep = jnp.concatenate([q8_ref[0]] * (LANES // Q_ROWS), axis=0)

    m_run = jnp.full((1, LANES), NEG_BIG, F32)
    l_run = jnp.zeros((1, LANES), F32)
    acc = jnp.zeros((SUBLANES, HEAD_DIM), F32)
    carry = lfn_ref[0]
    for g in range(ng):
        slot = g % 2
        if 0 < g < ng - 1:
            for c in group_copies(b, g + 1, 1 - slot):
                c.start()
        elif g == ng - 1:
            @pl.when(b + 1 < nb)
            def _():
                for c in group_copies(b + 1, 0, 1 - slot):
                    c.start()
            before_last_wait()
        for c in group_copies(b, g, slot):
            c.wait()

        lf = lfbuf[slot]
        within = _mm_exact_lhs(lf, _ones_mask(same_head & (ri > ci)))
        tot = _mm_exact_lhs(lf, _ones_mask(same_head))
        later = _mm_exact_rhs(_ones_mask(sj > si), tot)
        bias = (within + later + carry) * LOG2E
        carry = carry + jnp.sum(tot, axis=0, keepdims=True)

        for r in range(gp):
            s2 = _mm_nt(kbuf[slot, r], qrep)
            for a in range(HEADS):
                blk = s2[a * LANES:(a + 1) * LANES, :]
                sig = sig_of(r, a)
                sd_ref[sig:sig + 1, :] = jnp.sum(jnp.where(eye, blk, 0.0), axis=0, keepdims=True)
        for sub in range(gp // DEC_SUB):
            pages = range(sub * DEC_SUB, (sub + 1) * DEC_SUB)
            lo = sig_of(pages[-1], 0)
            rows = slice(lo, lo + DEC_SUB * HEADS)
            tt = sd_ref[rows, :] + bias[rows, :]
            mx = jnp.max(tt, axis=0, keepdims=True)
            mx = _class_reduce(jnp.broadcast_to(mx, (SUBLANES, LANES)), jnp.maximum)[0:1, :]
            m_new = jnp.maximum(m_run, mx)
            pf = jnp.exp2(tt - m_new).astype(BF16).astype(F32)
            alpha = jnp.exp2(m_run - m_new)
            l_run = alpha * l_run + jnp.sum(pf, axis=0, keepdims=True)
            m_run = m_new
            acc = acc * _lane_to_col(alpha, eye8)
            for r in pages:
                parts = []
                for a in range(HEADS):
                    sig = sig_of(r, a) - lo
                    prow = jnp.broadcast_to(pf[sig:sig + 1, :], (LANES, LANES))
                    parts.append(jnp.where(eye, prow, 0.0).astype(BF16))
                pcol = jnp.dot(jnp.concatenate(parts, axis=0), ones, preferred_element_type=F32)
                pv = pcol * vbuf[slot, r]
                acc = acc + jnp.sum(pv.reshape(PAGE_ROWS // SUBLANES, SUBLANES, LANES), axis=0)

    acc4 = acc[0:HEADS, :] + acc[HEADS:2 * HEADS, :]
    mcol = _lane_to_col(m_run, eye8)[0:HEADS, :]
    lsum = _class_reduce(jnp.broadcast_to(l_run, (SUBLANES, LANES)), jnp.add)[0:1, :]
    lcol = _lane_to_col(lsum, eye8)[0:HEADS, :]
    q4 = q8_ref[0][0:HEADS, :].astype(F32)
    s_new = jnp.sum(q4 * kn_ref[0], axis=-1, keepdims=True)
    m_f = jnp.maximum(mcol, s_new)
    a1 = jnp.exp2(mcol - m_f)
    a2 = jnp.exp2(s_new - m_f)
    o = (acc4 * a1 + a2 * vn_ref[0]) / (lcol * a1 + a2)
    o_ref[0] = o * gate_ref[0].astype(F32)


def _mixer_kernel(pt_ref, ab_ref, qi_ref, kj_ref,
                  q8_ref, kn_ref, vn_ref, gate_ref, lfn_ref, ck_hbm, cv_hbm, clf_hbm,
                  aq_ref, ak_ref, av_ref, ack_ref, acq_ref, agate_ref,
                  gqkv_ref, ggz_ref, gcol_ref, grow_ref, gcw_ref, ggn_ref,
                  o_ref, ao_ref, go_ref, gssm_ref, gconv_ref,
                  kbuf, vbuf, lfbuf, sems, sd_ref, m_ref, l_ref, acc_ref, cqb_ref,
                  gs_ref, gx_ref, gq_ref, gk_ref, gv_ref, *, n_att, n_gdn, gdn_blocks):
    del ab_ref
    step = pl.program_id(0)
    n_seq = pt_ref.shape[0]
    hbm, bufs = (ck_hbm, cv_hbm, clf_hbm), (kbuf, vbuf, lfbuf)

    def gdn():
        _gdn_block(step % gdn_blocks, gdn_blocks, gqkv_ref, ggz_ref, gcol_ref, grow_ref, gcw_ref, ggn_ref,
                   go_ref, gssm_ref, gconv_ref, gs_ref, gx_ref, gq_ref, gk_ref, gv_ref)

    @pl.when(step < n_seq)
    def _():
        _dec_start(step, pt_ref, hbm, bufs, sems)

    @pl.when(step < n_att)
    def _():
        _att_step(qi_ref[step], kj_ref[step], aq_ref, ak_ref, av_ref, ack_ref, acq_ref, agate_ref, ao_ref,
                  m_ref, l_ref, acc_ref, cqb_ref)

    @pl.when(step < n_seq)
    def _():
        _dec_finish(step, n_seq, pt_ref, q8_ref, kn_ref, vn_ref, gate_ref, lfn_ref, hbm, o_ref, bufs, sems, sd_ref,
                    lambda: pl.when(step < n_gdn)(gdn))

    @pl.when((step >= n_seq) & (step < n_gdn))
    def _():
        gdn()


def _mixers(page_table, q8, k_new, v_new, gate, lf_new, cache_k2, cache_v2, cache_lf3,
            aq, ak, av, row, col, agate, gqkv, ggz, conv_w, gdn_norm_g, batch, seq):
    nb, n_pages = page_table.shape
    gp = DEC_PAGES
    assert n_pages % (2 * gp) == 0, "an even number of page groups per sequence keeps the buffer slots static"
    n = seq // ATT_T
    steps = [(bb, i, j) for bb in range(batch) for i in range(n) for j in range(i + 1)]
    n_att = len(steps)
    gdn_blocks = seq // GDN_TB
    n_gdn = batch * gdn_blocks
    n_steps = max(nb, n_att, n_gdn)
    steps = steps + [steps[-1]] * (n_steps - n_att)
    ab_tab, qi_tab, kj_tab = (jnp.asarray([s[k] for s in steps], jnp.int32) for k in range(3))

    per_b = lambda s, *_: (jnp.minimum(s, nb - 1), 0, 0)
    qmap = lambda s, pt, ab, qi, kj: (ab[s] * n + qi[s], 0)
    kmap = lambda s, pt, ab, qi, kj: (ab[s] * n + kj[s], 0)
    gblk = lambda s: jnp.minimum(s, n_gdn - 1)
    grow = lambda s, *_: (gblk(s), 0)
    gseq = lambda s, *_: (gblk(s) // gdn_blocks, 0, 0)
    const2 = lambda s, *_: (0, 0)
    hbm = pl.BlockSpec(memory_space=pl.ANY)
    grid_spec = pltpu.PrefetchScalarGridSpec(
        num_scalar_prefetch=4,
        grid=(n_steps,),
        in_specs=[pl.BlockSpec((1, Q_ROWS, HEAD_DIM), per_b),
                  pl.BlockSpec((1, HEADS, HEAD_DIM), per_b),
                  pl.BlockSpec((1, HEADS, HEAD_DIM), per_b),
                  pl.BlockSpec((1, HEADS, HEAD_DIM), per_b),
                  pl.BlockSpec((1, 1, LANES), per_b),
                  hbm, hbm, hbm,
                  pl.BlockSpec((ATT_T, WIDTH), qmap),
                  pl.BlockSpec((ATT_T, WIDTH), kmap),
                  pl.BlockSpec((ATT_T, WIDTH), kmap),
                  pl.BlockSpec((1, 2 * SUBLANES, ATT_T), lambda s, pt, ab, qi, kj: (ab[s], 0, kj[s])),
                  pl.BlockSpec((ATT_T, LANES), qmap),
                  pl.BlockSpec((ATT_T, WIDTH), qmap),
                  pl.BlockSpec((GDN_TB, CONV_DIM), grow),
                  pl.BlockSpec((GDN_TB, WIDTH), grow),
                  pl.BlockSpec((GDN_TB, LANES), grow),
                  pl.BlockSpec((1, 2 * SUBLANES, GDN_TB),
                               lambda s, *_: (gblk(s) // gdn_blocks, 0, gblk(s) % gdn_blocks)),
                  pl.BlockSpec((CONV_W, CONV_DIM), const2),
                  pl.BlockSpec((1, HEAD_DIM), const2)],
        out_specs=[pl.BlockSpec((1, HEADS, HEAD_DIM), per_b),
                   pl.BlockSpec((ATT_T, WIDTH), qmap),
                   pl.BlockSpec((GDN_TB, WIDTH), grow),
                   pl.BlockSpec((1, HEADS, HEAD_DIM, HEAD_DIM), lambda s, *_: (gblk(s) // gdn_blocks, 0, 0, 0)),
                   pl.BlockSpec((1, CONV_W - 1, CONV_DIM), gseq)],
        scratch_shapes=[pltpu.VMEM((2, gp, PAGE_ROWS, HEAD_DIM), F32),
                        pltpu.VMEM((2, gp, PAGE_ROWS, HEAD_DIM), F32),
                        pltpu.VMEM((2, gp * HEADS, LANES), F32),
                        pltpu.SemaphoreType.DMA((3, 2)),
                        pltpu.VMEM((gp * HEADS, LANES), F32)]
                       + [pltpu.VMEM((HEADS, ATT_T, LANES), F32)] * 4
                       + [pltpu.VMEM((HEADS, HEAD_DIM, HEAD_DIM), F32),
                          pltpu.VMEM((GDN_TB + SUBLANES, CONV_DIM), F32)]
                       + [pltpu.VMEM((GDN_TB, WIDTH), F32)] * 3)
    return pl.pallas_call(
        functools.partial(_mixer_kernel, n_att=n_att, n_gdn=n_gdn, gdn_blocks=gdn_blocks),
        grid_spec=grid_spec,
        out_shape=[jax.ShapeDtypeStruct((nb, HEADS, HEAD_DIM), F32),
                   jax.ShapeDtypeStruct((batch * seq, WIDTH), BF16),
                   jax.ShapeDtypeStruct((batch * seq, WIDTH), BF16),
                   jax.ShapeDtypeStruct((batch, HEADS, HEAD_DIM, HEAD_DIM), F32),
                   jax.ShapeDtypeStruct((batch, CONV_W - 1, CONV_DIM), F32)],
        compiler_params=pltpu.CompilerParams(dimension_semantics=("arbitrary",),
                                             vmem_limit_bytes=DEC_VMEM_LIMIT),
        name="mixers",
    )(page_table, ab_tab, qi_tab, kj_tab, q8, k_new, v_new, gate, lf_new, cache_k2, cache_v2, cache_lf3,
      aq, ak, av, row, col, agate, gqkv, ggz, col, row, conv_w, gdn_norm_g)


def _layer(x_prompt, x_sample, cache_k, cache_v, cache_logf, state_ssm, state_conv, page_table,
           c_prompt, c_sample, norm_g, w_ada, b_ada, w_in, b_f, conv_w, a_log, dt_bias,
           gdn_norm_g, q_norm_g, k_norm_g, w_out):
    batch, seq, _ = x_prompt.shape
    nb = x_sample.shape[0]
    w = WIDTH
    o_small = 4 * w
    o_fox = o_small + 2 * HEADS
    w_big = jnp.concatenate([w_in[:, 0:o_small], w_in[:, o_fox:o_fox + 4 * w]], axis=1).astype(BF16)
    w_small = jnp.concatenate([w_in[:, o_small:o_fox], w_in[:, o_fox + 4 * w:],
                               jnp.zeros((D_MODEL, LANES - 3 * HEADS), F32)], axis=1).astype(BF16)
    w_out_b = w_out.astype(BF16)
    par = jnp.zeros((SUBLANES, LANES), F32)
    par = par.at[0, LANE_G:LANE_G + HEADS].set(a_log)
    par = par.at[1, LANE_G:LANE_G + HEADS].set(dt_bias)
    par = par.at[2, LANE_F:LANE_F + HEADS].set(b_f)
    ng2, qg2, kg2, gg2 = (a.reshape(1, -1) for a in (norm_g, q_norm_g, k_norm_g, gdn_norm_g))

    cp8 = jnp.pad(c_prompt, ((0, SUBLANES - batch), (0, 0)))
    mod_p, mod_s = _ada_mod(cp8, c_sample, w_ada, b_ada.reshape(1, -1))
    mod_p4 = mod_p[:, 0:batch].reshape(3, batch, 1, D_MODEL)
    mod_s4 = mod_s.reshape(3, 1, nb, D_MODEL)

    xp = x_prompt.reshape(batch * seq, D_MODEL)
    qkv, gz, small, fq, fk, fkb, fv, fvb, fzg = _inproj(xp, mod_p4, False, PROMPT_TM, ng2, w_big, w_small, qg2, kg2)
    col, row = _small_prep(small, par, batch, seq)
    k_p = fk.reshape(batch, seq, HEADS, HEAD_DIM)
    v_p = fv.reshape(batch, seq, HEADS, HEAD_DIM)
    lf_p = col[:, LANE_LF:LANE_LF + HEADS].reshape(batch, seq, HEADS)

    xs = x_sample.reshape(nb, D_MODEL)
    qkv_s, gz_s, small_s, fq_s, fk_s, _, fv_s, _, fzg_s = _inproj(xs, mod_s4, True, nb, ng2, w_big, w_small, qg2, kg2)
    qn_s, kn_s, vn_s, sc_s = _sample_mid(qkv_s, state_conv[:, 0], state_conv[:, 1], state_conv[:, 2],
                                         conv_w, small_s, par)
    h3 = lambda a: a.reshape(nb, HEADS, HEAD_DIM)
    bcast = lambda a: jnp.broadcast_to(a[:, :, None], (nb, HEADS, HEAD_DIM))
    t8 = jnp.stack([h3(kn_s), h3(qn_s), h3(vn_s), bcast(sc_s[:, LANE_B:LANE_B + HEADS]),
                    bcast(sc_s[:, LANE_G:LANE_G + HEADS])]
                   + [jnp.zeros((nb, HEADS, HEAD_DIM), F32)] * (SUBLANES - 5), axis=2)
    o_a_s, ssm_s = _gdn_step(t8, state_ssm, h3(gz_s), gg2)
    lf_s = sc_s[:, LANE_F:LANE_F + HEADS]
    q8 = jnp.tile(h3(fq_s), (1, Q_ROWS // HEADS, 1))
    lf_new = jnp.tile(lf_s, (1, LANES // HEADS)).reshape(nb, 1, LANES)
    n_pool = cache_k.shape[0]
    o_b_s, o_b, o_a, ssm_p, conv_p = _mixers(
        page_table, q8, h3(fk_s), h3(fv_s), h3(fzg_s), lf_new,
        cache_k.reshape(n_pool * PAGE_ROWS, HEAD_DIM), cache_v.reshape(n_pool * PAGE_ROWS, HEAD_DIM),
        cache_logf.reshape(n_pool, HEADS, LANES),
        fq, fkb, fvb, row, col, fzg, qkv, gz, conv_w, gg2, batch, seq)
    y_p = _outproj(xp, o_a, o_b, mod_p4, False, OUT_TM, w_out_b).reshape(batch, seq, D_MODEL)
    y_s = _outproj(xs, o_a_s.reshape(nb, w), o_b_s.reshape(nb, w), mod_s4, True, nb, w_out_b).reshape(nb, 1, D_MODEL)
    conv_s = jnp.stack([state_conv[:, 1], state_conv[:, 2], qkv_s], axis=1)
    return (y_p, y_s, k_p, v_p, lf_p, ssm_p, conv_p,
            h3(fk_s).reshape(nb, 1, HEADS, HEAD_DIM), h3(fv_s).reshape(nb, 1, HEADS, HEAD_DIM),
            lf_s.reshape(nb, 1, HEADS), ssm_s, conv_s)


def kernel(x_prompt, x_sample, cache_k, cache_v, cache_logf, state_ssm, state_conv, page_table, c_prompt, c_sample, norm_g, w_ada, b_ada, w_in, b_f, conv_w, a_log, dt_bias, gdn_norm_g, q_norm_g, k_norm_g, w_out):
    assert w_ada.shape[0] == 1, "single layer"
    outs = _layer(x_prompt, x_sample, cache_k[0], cache_v[0], cache_logf[0], state_ssm[0], state_conv[0],
                  page_table, c_prompt, c_sample, norm_g[0], w_ada[0], b_ada[0], w_in[0], b_f[0], conv_w[0],
                  a_log[0], dt_bias[0], gdn_norm_g[0], q_norm_g[0], k_norm_g[0], w_out[0])
    y_p, y_s = outs[0], outs[1]
    return (y_p, y_s) + tuple(o[None] for o in outs[2:])
```

```python
import functools

import jax
import jax.numpy as jnp
from jax import lax
from jax.experimental import pallas as pl
from jax.experimental.pallas import tpu as pltpu

F32 = jnp.float32
BF16 = jnp.bfloat16

D_MODEL = 1024
HEADS = 4
HEAD_DIM = 128
WIDTH = HEADS * HEAD_DIM
CONV_W = 4
CONV_DIM = 3 * WIDTH
CHUNK = 64
PAGE_SIZE = 128
PAGE_ROWS = PAGE_SIZE * HEADS
NORM_EPS = 1e-6
LANES = 128
SUBLANES = 8
NEG_BIG = -1e30
LOG2E = 1.4426950408889634
ATT_Q_SCALE = LOG2E * HEAD_DIM ** -0.5
VMEM_LIMIT = 48 * 1024 * 1024

COL_GQKV = 0
COL_GZ = 3 * WIDTH
COL_FQ = 4 * WIDTH
COL_FK = 5 * WIDTH
COL_FV = 6 * WIDTH
COL_FZ = 7 * WIDTH
BIG_COLS = 8 * WIDTH
LANE_B, LANE_G, LANE_F, LANE_LF = 0, 4, 8, 12

PROMPT_TM = 256
OUT_TM = 512
SMALL_TB = 512
GDN_TB = 128
ATT_T = 512
DEC_PAGES = 32
DEC_SUB = 16
DEC_SIDE_EVERY = 8
DEC_VMEM_LIMIT = 56 * 1024 * 1024
Q_ROWS = 16
STEP_BATCH = 4


def _mm(a, b):
    return jnp.dot(a.astype(BF16), b.astype(BF16), preferred_element_type=F32)


def _mm_nt(a, b):
    return lax.dot_general(a.astype(BF16), b.astype(BF16), (((1,), (1,)), ((), ())),
                           preferred_element_type=F32)


def _mm_tn(a, b):
    return lax.dot_general(a.astype(BF16), b.astype(BF16), (((0,), (0,)), ((), ())),
                           preferred_element_type=F32)


def _mm_exact_rhs(mat_b, y):
    hi = y.astype(BF16)
    r1 = y - hi.astype(F32)
    mid = r1.astype(BF16)
    lo = (r1 - mid.astype(F32)).astype(BF16)
    dot = functools.partial(jnp.dot, preferred_element_type=F32)
    return dot(mat_b, hi) + dot(mat_b, mid) + dot(mat_b, lo)


def _mm_exact_lhs(y, mat_b):
    hi = y.astype(BF16)
    r1 = y - hi.astype(F32)
    mid = r1.astype(BF16)
    lo = (r1 - mid.astype(F32)).astype(BF16)
    dot = functools.partial(jnp.dot, preferred_element_type=F32)
    return dot(hi, mat_b) + dot(mid, mat_b) + dot(lo, mat_b)


def _sigmoid(x):
    return 1.0 / (1.0 + jnp.exp(-x))


def _silu(x):
    return x * _sigmoid(x)


def _softplus(x):
    return jnp.maximum(x, 0.0) + jnp.log(1.0 + jnp.exp(-jnp.abs(x)))


def _ones_mask(mask):
    return jnp.where(mask, 1.0, 0.0).astype(BF16)


def _params(n_axes):
    return pltpu.CompilerParams(dimension_semantics=("arbitrary",) * n_axes,
                                vmem_limit_bytes=VMEM_LIMIT)


def _ada_kernel(cp_ref, cs_ref, w_ref, b_ref, op_ref, os_ref):
    w = w_ref[...].astype(BF16)
    b = b_ref[...]
    op_ref[0] = jnp.dot(_silu(cp_ref[...]).astype(BF16), w, preferred_element_type=F32) + b
    os_ref[0] = jnp.dot(_silu(cs_ref[...]).astype(BF16), w, preferred_element_type=F32) + b


def _ada_mod(c_prompt8, c_sample, w_ada, b_ada):
    nb = c_sample.shape[0]
    return pl.pallas_call(
        _ada_kernel,
        grid=(3,),
        in_specs=[pl.BlockSpec((SUBLANES, D_MODEL), lambda k: (0, 0)),
                  pl.BlockSpec((nb, D_MODEL), lambda k: (0, 0)),
                  pl.BlockSpec((D_MODEL, D_MODEL), lambda k: (0, k)),
                  pl.BlockSpec((1, D_MODEL), lambda k: (0, k))],
        out_specs=[pl.BlockSpec((1, SUBLANES, D_MODEL), lambda k: (k, 0, 0)),
                   pl.BlockSpec((1, nb, D_MODEL), lambda k: (k, 0, 0))],
        out_shape=[jax.ShapeDtypeStruct((3, SUBLANES, D_MODEL), F32),
                   jax.ShapeDtypeStruct((3, nb, D_MODEL), F32)],
        compiler_params=_params(1),
        name="ada_mod",
    )(c_prompt8, c_sample, w_ada, b_ada)


def _inproj_kernel(x_ref, sh_ref, sc_ref, ng_ref, wb_ref, ws_ref, qg_ref, kg_ref,
                   qkv_ref, gz_ref, sm_ref, fq_ref, fk_ref, fkb_ref, fv_ref, fvb_ref, fzg_ref):
    x = x_ref[...]
    y = x * lax.rsqrt(jnp.mean(x * x, axis=-1, keepdims=True) + NORM_EPS)
    hdn = (y * ng_ref[...]) * (1.0 + sc_ref[0, 0]) + sh_ref[0, 0]
    hb = hdn.astype(BF16)
    dot = functools.partial(jnp.dot, preferred_element_type=F32)
    qkv_ref[...] = dot(hb, wb_ref[:, COL_GQKV:COL_GZ])
    gz_ref[...] = dot(hb, wb_ref[:, COL_GZ:COL_FQ])
    sm_ref[...] = dot(hb, ws_ref[...])
    fq = dot(hb, wb_ref[:, COL_FQ:COL_FK])
    fk = dot(hb, wb_ref[:, COL_FK:COL_FV])
    fv = dot(hb, wb_ref[:, COL_FV:COL_FZ])
    fz = dot(hb, wb_ref[:, COL_FZ:BIG_COLS])
    fvb_ref[...] = fv.astype(BF16)
    fzg_ref[...] = _silu(fz).astype(BF16)
    scale = ATT_Q_SCALE
    tm = x.shape[0]
    for h in range(HEADS):
        sl = slice(h * HEAD_DIM, (h + 1) * HEAD_DIM)
        qh = fq[:, sl]
        qn = (qh * lax.rsqrt(jnp.mean(qh * qh, axis=-1, keepdims=True) + NORM_EPS)) * qg_ref[...]
        fq_ref[:, sl] = (qn * scale).astype(BF16)
        kh = fk[:, sl]
        kn = (kh * lax.rsqrt(jnp.mean(kh * kh, axis=-1, keepdims=True) + NORM_EPS)) * kg_ref[...]
        fkb_ref[:, sl] = kn.astype(BF16)
        fk_ref[pl.ds(h, tm, stride=HEADS), :] = kn
        fv_ref[pl.ds(h, tm, stride=HEADS), :] = fv[:, sl]


def _inproj(x2d, mod4, per_row_mod, tm, norm_g, w_big, w_small, q_norm_g, k_norm_g):
    rows = x2d.shape[0]
    nblk = rows // tm
    if per_row_mod:
        mod_block = (1, 1, tm, D_MODEL)
        sh_map = lambda i: (0, 0, i, 0)
        sc_map = lambda i: (1, 0, i, 0)
    else:
        blocks_per_batch = nblk // mod4.shape[1]
        mod_block = (1, 1, 1, D_MODEL)
        sh_map = lambda i: (0, i // blocks_per_batch, 0, 0)
        sc_map = lambda i: (1, i // blocks_per_batch, 0, 0)
    row_spec = lambda n: pl.BlockSpec((tm, n), lambda i: (i, 0))
    const = lambda shape: pl.BlockSpec(shape, lambda i: (0,) * len(shape))
    outs = [(CONV_DIM, F32, 1), (WIDTH, F32, 1), (LANES, F32, 1), (WIDTH, BF16, 1), (HEAD_DIM, F32, HEADS),
            (WIDTH, BF16, 1), (HEAD_DIM, F32, HEADS), (WIDTH, BF16, 1), (WIDTH, BF16, 1)]
    return pl.pallas_call(
        _inproj_kernel,
        grid=(nblk,),
        in_specs=[row_spec(D_MODEL),
                  pl.BlockSpec(mod_block, sh_map), pl.BlockSpec(mod_block, sc_map),
                  const((1, D_MODEL)), const((D_MODEL, BIG_COLS)), const((D_MODEL, LANES)),
                  const((1, HEAD_DIM)), const((1, HEAD_DIM))],
        out_specs=[pl.BlockSpec((tm * r, n), lambda i: (i, 0)) for n, _, r in outs],
        out_shape=[jax.ShapeDtypeStruct((rows * r, n), dt) for n, dt, r in outs],
        compiler_params=_params(1),
        name="in_proj",
    )(x2d, mod4, mod4, norm_g, w_big, w_small, q_norm_g, k_norm_g)


def _token_scalars(x, par):
    beta = _sigmoid(x)
    g = -jnp.exp(par[0:1, :]) * _softplus(x + par[1:2, :])
    lf = -_softplus(-(x + par[2:3, :]))
    return beta, g, lf


def _small_kernel(sm_ref, par_ref, col_ref, row_ref, carry_ref):
    tb = SMALL_TB

    @pl.when(pl.program_id(1) == 0)
    def _():
        carry_ref[...] = jnp.zeros_like(carry_ref)

    x = sm_ref[...]
    beta, g, lf = _token_scalars(x, par_ref[...])
    lane = lax.broadcasted_iota(jnp.int32, (tb, LANES), 1)
    is_b = lane < LANE_G
    is_g = (lane >= LANE_G) & (lane < LANE_F)
    is_f = (lane >= LANE_F) & (lane < LANE_LF)
    gm = jnp.where(is_g, g, 0.0)
    fm = jnp.where(is_f, lf, 0.0)
    r = lax.broadcasted_iota(jnp.int32, (tb, tb), 0)
    c = lax.broadcasted_iota(jnp.int32, (tb, tb), 1)
    tri = c <= r
    same_chunk = jnp.right_shift(r, 6) == jnp.right_shift(c, 6)
    gcum = _mm_exact_rhs(_ones_mask(tri & same_chunk), gm)
    cum = _mm_exact_rhs(_ones_mask(tri), fm) + carry_ref[...]
    cum = jnp.where(is_f, cum, 0.0)
    carry_ref[...] = cum[tb - 1:tb, :]
    col = jnp.where(is_b, beta, 0.0) + gcum + cum + pltpu.roll(fm, LANE_LF - LANE_F, axis=1)
    col_ref[...] = col
    for i in range(tb // LANES):
        t = col[i * LANES:(i + 1) * LANES, :]
        row_ref[0, :, i * LANES:(i + 1) * LANES] = t.T[0:2 * SUBLANES, :]


def _small_prep(small, par, batch, seq):
    nt = seq // SMALL_TB
    return pl.pallas_call(
        _small_kernel,
        grid=(batch, nt),
        in_specs=[pl.BlockSpec((SMALL_TB, LANES), lambda b, t: (b * nt + t, 0)),
                  pl.BlockSpec((SUBLANES, LANES), lambda b, t: (0, 0))],
        out_specs=[pl.BlockSpec((SMALL_TB, LANES), lambda b, t: (b * nt + t, 0)),
                   pl.BlockSpec((1, 2 * SUBLANES, SMALL_TB), lambda b, t: (b, 0, t))],
        out_shape=[jax.ShapeDtypeStruct((batch * seq, LANES), F32),
                   jax.ShapeDtypeStruct((batch, 2 * SUBLANES, seq), F32)],
        scratch_shapes=[pltpu.VMEM((1, LANES), F32)],
        compiler_params=_params(2),
        name="token_scalars",
    )(small, par)


_GDN_TAIL = "tail"


def _run(stages):
    try:
        while True:
            next(stages)
    except StopIteration as done:
        return done.value


def _unit_lower_inverses(ms, eye, ri, ci):
    base = 3
    same = jnp.right_shift(ri, base) == jnp.right_shift(ci, base)
    xps = [-jnp.where(same, m, 0.0) for m in ms]
    ps = [eye + xp for xp in xps]
    for _ in range(base - 1):
        xps = [_mm(xp, xp) for xp in xps]
        yield
        ps = [p + _mm(p, xp) for p, xp in zip(ps, xps)]
        yield
    size = base
    while (1 << size) < CHUNK:
        off = ((jnp.right_shift(ri, size + 1) == jnp.right_shift(ci, size + 1))
               & (jnp.right_shift(ri, size) != jnp.right_shift(ci, size)))
        pcs = [_mm(p, jnp.where(off, m, 0.0)) for p, m in zip(ps, ms)]
        yield
        ps = [p - _mm(pc, p) for p, pc in zip(ps, pcs)]
        yield
        size += 1
    return ps


def _gdn_chunk_preps(qs, ks, vs, betas, gcs, grs, incl, strict, eye, ri, ci):
    decays = [jnp.where(incl, jnp.exp(jnp.where(incl, gc - gr, 0.0)), 0.0) for gc, gr in zip(gcs, grs)]
    kbs = [k * beta for k, beta in zip(ks, betas)]
    boths = [_mm_nt(jnp.concatenate([kb, q], axis=0), k) for kb, q, k in zip(kbs, qs, ks)]
    yield
    ms = [jnp.where(strict, both[0:CHUNK] * decay, 0.0) for both, decay in zip(boths, decays)]
    a_intras = [both[CHUNK:2 * CHUNK] * decay for both, decay in zip(boths, decays)]
    tinvs = yield from _unit_lower_inverses(ms, eye, ri, ci)
    egs = [jnp.exp(gc) for gc in gcs]
    uws = [_mm(tinv, jnp.concatenate([v * beta, kb * eg], axis=1))
           for tinv, v, beta, kb, eg in zip(tinvs, vs, betas, kbs, egs)]
    yield
    preps = []
    for uw, q, k, eg, gc, a_intra in zip(uws, qs, ks, egs, gcs, a_intras):
        g_last = gc[CHUNK - 1:CHUNK, :]
        wq = jnp.concatenate([uw[:, HEAD_DIM:2 * HEAD_DIM], q * eg], axis=0)
        preps.append((uw[:, 0:HEAD_DIM], wq, a_intra, k * jnp.exp(g_last - gc), jnp.exp(g_last)))
    return preps


def _gdn_chunk_steps(preps, states):
    boths = [_mm(p[1], s) for p, s in zip(preps, states)]
    yield
    v_news = [p[0] - both[0:CHUNK] for p, both in zip(preps, boths)]
    outs = [both[CHUNK:2 * CHUNK] + _mm(p[2], v_new) for p, both, v_new in zip(preps, boths, v_news)]
    new_states = [s * p[4] + _mm_tn(p[3], v_new) for p, s, v_new in zip(preps, states, v_news)]
    yield
    return outs, new_states


def _gdn_block_stages(t, n_t, qkv_ref, gz_ref, col_ref, row_ref, cw_ref, gn_ref,
                      o_ref, ssm_ref, conv_ref, s_ref, xbuf_ref, qn_ref, kn_ref, vn_ref):
    tb = GDN_TB
    halo = SUBLANES

    @pl.when(t == 0)
    def _():
        s_ref[...] = jnp.zeros_like(s_ref)
        xbuf_ref[0:halo, :] = jnp.zeros((halo, CONV_DIM), F32)

    xbuf_ref[halo:halo + tb, :] = qkv_ref[...]
    scale = HEAD_DIM ** -0.5
    for cb in range(CONV_DIM // LANES):
        sl = slice(cb * LANES, (cb + 1) * LANES)
        acc = xbuf_ref[halo - 3:halo - 3 + tb, sl] * cw_ref[0:1, sl]
        for i in range(1, CONV_W):
            acc = acc + xbuf_ref[halo - 3 + i:halo - 3 + i + tb, sl] * cw_ref[i:i + 1, sl]
        act = _silu(acc)
        part, h = divmod(cb, HEADS)
        hs = slice(h * LANES, (h + 1) * LANES)
        if part == 0:
            qn_ref[:, hs] = act * lax.rsqrt(jnp.sum(act * act, axis=-1, keepdims=True) + NORM_EPS) * scale
        elif part == 1:
            kn_ref[:, hs] = act * lax.rsqrt(jnp.sum(act * act, axis=-1, keepdims=True) + NORM_EPS)
        else:
            vn_ref[:, hs] = act
    tail = xbuf_ref[tb:tb + halo, :]
    xbuf_ref[0:halo, :] = tail
    yield

    ri = lax.broadcasted_iota(jnp.int32, (CHUNK, CHUNK), 0)
    ci = lax.broadcasted_iota(jnp.int32, (CHUNK, CHUNK), 1)
    incl = ci <= ri
    strict = ci < ri
    eye = jnp.where(ci == ri, 1.0, 0.0)

    colt = col_ref[...]
    rowt = row_ref[0]
    n_cc = tb // CHUNK
    items = [(cc, h) for cc in range(n_cc) for h in range(HEADS)]
    rs = lambda cc: slice(cc * CHUNK, (cc + 1) * CHUNK)
    hs = lambda h: slice(h * HEAD_DIM, (h + 1) * HEAD_DIM)
    preps = yield from _gdn_chunk_preps(
        [qn_ref[rs(cc), hs(h)] for cc, h in items],
        [kn_ref[rs(cc), hs(h)] for cc, h in items],
        [vn_ref[rs(cc), hs(h)] for cc, h in items],
        [colt[rs(cc), LANE_B + h:LANE_B + h + 1] for cc, h in items],
        [colt[rs(cc), LANE_G + h:LANE_G + h + 1] for cc, h in items],
        [rowt[LANE_G + h:LANE_G + h + 1, rs(cc)] for cc, h in items],
        incl, strict, eye, ri, ci)
    states = [s_ref[h] for h in range(HEADS)]
    for cc in range(n_cc):
        outs, states = yield from _gdn_chunk_steps(preps[cc * HEADS:(cc + 1) * HEADS], states)
        for h, o in enumerate(outs):
            on = (o * lax.rsqrt(jnp.mean(o * o, axis=-1, keepdims=True) + NORM_EPS)) * gn_ref[...]
            o_ref[rs(cc), hs(h)] = (on * _silu(gz_ref[rs(cc), hs(h)])).astype(o_ref.dtype)
    for h in range(HEADS):
        s_ref[h] = states[h]
    yield _GDN_TAIL

    @pl.when(t == n_t - 1)
    def _():
        ssm_ref[0] = s_ref[...]
        conv_ref[0] = tail[halo - (CONV_W - 1):halo, :]


def _att_step(qi, kj, q_ref, k_ref, v_ref, ck_ref, cq_ref, gate_ref, o_ref, m_ref, l_ref, acc_ref, cqb_ref):
    tq = tk = ATT_T
    heads = range(HEADS)
    hs = lambda h: slice(h * HEAD_DIM, (h + 1) * HEAD_DIM)

    @pl.when(kj == 0)
    def _():
        m_ref[...] = jnp.full_like(m_ref, NEG_BIG)
        l_ref[...] = jnp.zeros_like(l_ref)
        acc_ref[...] = jnp.zeros_like(acc_ref)
        for h in heads:
            cqb_ref[h] = jnp.broadcast_to(cq_ref[:, LANE_F + h:LANE_F + h + 1] * LOG2E, (tq, LANES))

    def update(diagonal):
        ss = [lax.dot_general(q_ref[:, hs(h)], k_ref[:, hs(h)], (((1,), (1,)), ((), ())),
                              preferred_element_type=F32) for h in heads]
        tts = [s - ck_ref[0, LANE_F + h:LANE_F + h + 1, :] * LOG2E for h, s in zip(heads, ss)]
        if diagonal:
            visible = (lax.broadcasted_iota(jnp.int32, (tq, tk), 1)
                       <= lax.broadcasted_iota(jnp.int32, (tq, tk), 0))
            tts = [jnp.where(visible, tt, NEG_BIG) for tt in tts]
        cqs = [cqb_ref[h] for h in heads]
        m_olds = [m_ref[h] for h in heads]
        m_news = [jnp.maximum(m_old, jnp.broadcast_to(jnp.max(tt, axis=1, keepdims=True), (tq, LANES)) + cq)
                  for m_old, tt, cq in zip(m_olds, tts, cqs)]
        shifts = [jnp.concatenate([m_new - cq] * (tk // LANES), axis=1) for m_new, cq in zip(m_news, cqs)]
        ps = [jnp.exp2(tt - shift).astype(BF16) for tt, shift in zip(tts, shifts)]
        ones = jnp.ones((tk, HEAD_DIM), BF16)
        pvs = [jnp.dot(p, jnp.concatenate([v_ref[:, hs(h)], ones], axis=1), preferred_element_type=F32)
               for h, p in zip(heads, ps)]
        alphas = [jnp.exp2(m_old - m_new) for m_old, m_new in zip(m_olds, m_news)]
        accs = [alpha * acc_ref[h] + pv[:, 0:HEAD_DIM] for h, alpha, pv in zip(heads, alphas, pvs)]
        ls = [alpha * l_ref[h] + pv[:, HEAD_DIM:2 * HEAD_DIM] for h, alpha, pv in zip(heads, alphas, pvs)]
        return m_news, ls, accs

    @pl.when(kj < qi)
    def _():
        m_news, ls, accs = update(False)
        for h in heads:
            m_ref[h] = m_news[h]
            l_ref[h] = ls[h]
            acc_ref[h] = accs[h]

    @pl.when(kj == qi)
    def _():
        _, ls, accs = update(True)
        for h in heads:
            o_ref[:, hs(h)] = ((accs[h] / ls[h]) * gate_ref[:, hs(h)].astype(F32)).astype(o_ref.dtype)


def _outproj_kernel(x_ref, oa_ref, ob_ref, gate_ref, w_ref, y_ref):
    mix = (jnp.dot(oa_ref[...].astype(BF16), w_ref[0:WIDTH, :], preferred_element_type=F32)
           + jnp.dot(ob_ref[...].astype(BF16), w_ref[WIDTH:2 * WIDTH, :], preferred_element_type=F32))
    y_ref[...] = x_ref[...] + gate_ref[0, 0] * mix


def _outproj(x2d, oa, ob, mod4, per_row_mod, tm, w_out_b):
    rows = x2d.shape[0]
    nblk = rows // tm
    if per_row_mod:
        mod_block = (1, 1, tm, D_MODEL)
        g_map = lambda i: (2, 0, i, 0)
    else:
        blocks_per_batch = nblk // mod4.shape[1]
        mod_block = (1, 1, 1, D_MODEL)
        g_map = lambda i: (2, i // blocks_per_batch, 0, 0)
    row_spec = lambda n: pl.BlockSpec((tm, n), lambda i: (i, 0))
    return pl.pallas_call(
        _outproj_kernel,
        grid=(nblk,),
        in_specs=[row_spec(D_MODEL), row_spec(WIDTH), row_spec(WIDTH),
                  pl.BlockSpec(mod_block, g_map),
                  pl.BlockSpec((2 * WIDTH, D_MODEL), lambda i: (0, 0))],
        out_specs=row_spec(D_MODEL),
        out_shape=jax.ShapeDtypeStruct((rows, D_MODEL), F32),
        compiler_params=_params(1),
        name="out_proj",
    )(x2d, oa, ob, mod4, w_out_b)


def _smid_kernel(u_ref, s0_ref, s1_ref, s2_ref, cw_ref, sm_ref, par_ref,
                 qn_ref, kn_ref, vn_ref, sc_ref):
    conv = (s0_ref[...] * cw_ref[0:1, :] + s1_ref[...] * cw_ref[1:2, :]
            + s2_ref[...] * cw_ref[2:3, :] + u_ref[...] * cw_ref[3:4, :])
    act = _silu(conv)
    for h in range(HEADS):
        hs = slice(h * HEAD_DIM, (h + 1) * HEAD_DIM)
        q = act[:, h * HEAD_DIM:(h + 1) * HEAD_DIM]
        k = act[:, WIDTH + h * HEAD_DIM:WIDTH + (h + 1) * HEAD_DIM]
        qn_ref[:, hs] = q * lax.rsqrt(jnp.sum(q * q, axis=-1, keepdims=True) + NORM_EPS)
        kn_ref[:, hs] = k * lax.rsqrt(jnp.sum(k * k, axis=-1, keepdims=True) + NORM_EPS)
    vn_ref[...] = act[:, 2 * WIDTH:3 * WIDTH]
    beta, g, lf = _token_scalars(sm_ref[...], par_ref[...])
    lane = lax.broadcasted_iota(jnp.int32, sm_ref.shape, 1)
    sc_ref[...] = jnp.where(lane < LANE_G, beta, jnp.where(lane < LANE_F, g, lf))


def _sample_mid(u, s0, s1, s2, conv_w, small, par):
    nb = u.shape[0]
    full = lambda a: pl.BlockSpec(a.shape, lambda: (0,) * a.ndim)
    args = (u, s0, s1, s2, conv_w, small, par)
    return pl.pallas_call(
        _smid_kernel,
        in_specs=[full(a) for a in args],
        out_specs=[pl.BlockSpec((nb, WIDTH), lambda: (0, 0))] * 3 + [pl.BlockSpec((nb, LANES), lambda: (0, 0))],
        out_shape=[jax.ShapeDtypeStruct((nb, WIDTH), F32)] * 3 + [jax.ShapeDtypeStruct((nb, LANES), F32)],
        compiler_params=pltpu.CompilerParams(vmem_limit_bytes=VMEM_LIMIT),
        name="sample_conv",
    )(*args)


def _gdn_step_kernel(t_ref, s_ref, gz_ref, gn_ref, o_ref, sn_ref):
    scale = HEAD_DIM ** -0.5
    row = lax.broadcasted_iota(jnp.int32, (SUBLANES, HEAD_DIM), 0)
    items = [(b, h) for b in range(STEP_BATCH) for h in range(HEADS)]
    ts = [t_ref[b, h] for b, h in items]
    rs = [_mm(t, s_ref[b, h]) for t, (b, h) in zip(ts, items)]
    egs, v_news = [], []
    for t, r, (b, h) in zip(ts, rs, items):
        k, q, v, beta, g = (t[i:i + 1, :] for i in range(5))
        eg = jnp.exp(g)
        v_new = beta * (v - eg * r[0:1, :])
        o = scale * (eg * r[1:2, :] + jnp.sum(q * k, axis=-1, keepdims=True) * v_new)
        on = (o * lax.rsqrt(jnp.mean(o * o, axis=-1, keepdims=True) + NORM_EPS)) * gn_ref[...]
        o_ref[b, h:h + 1, :] = on * _silu(gz_ref[b, h:h + 1, :])
        egs.append(eg)
        v_news.append(v_new)
    outers = [_mm_tn(jnp.where(row == 0, t, 0.0),
                     jnp.where(row == 0, jnp.broadcast_to(v_new, (SUBLANES, HEAD_DIM)), 0.0))
              for t, v_new in zip(ts, v_news)]
    for eg, outer, (b, h) in zip(egs, outers, items):
        sn_ref[b, h] = s_ref[b, h] * eg + outer


def _gdn_step(t8, state, gz3, gdn_norm_g):
    nb = state.shape[0]
    sb = STEP_BATCH
    return pl.pallas_call(
        _gdn_step_kernel,
        grid=(nb // sb,),
        in_specs=[pl.BlockSpec((sb, HEADS, SUBLANES, HEAD_DIM), lambda b: (b, 0, 0, 0)),
                  pl.BlockSpec((sb, HEADS, HEAD_DIM, HEAD_DIM), lambda b: (b, 0, 0, 0)),
                  pl.BlockSpec((sb, HEADS, HEAD_DIM), lambda b: (b, 0, 0)),
                  pl.BlockSpec((1, HEAD_DIM), lambda b: (0, 0))],
        out_specs=[pl.BlockSpec((sb, HEADS, HEAD_DIM), lambda b: (b, 0, 0)),
                   pl.BlockSpec((sb, HEADS, HEAD_DIM, HEAD_DIM), lambda b: (b, 0, 0, 0))],
        out_shape=[jax.ShapeDtypeStruct((nb, HEADS, HEAD_DIM), F32),
                   jax.ShapeDtypeStruct((nb, HEADS, HEAD_DIM, HEAD_DIM), F32)],
        compiler_params=_params(1),
        name="gdn_step",
    )(t8, state, gz3, gdn_norm_g)


def _class_reduce(x, op):
    shift = HEADS
    while shift < LANES:
        x = op(x, pltpu.roll(x, shift, axis=1))
        shift *= 2
    return x


def _lane_to_col(v, eye8):
    return jnp.sum(jnp.where(eye8, jnp.broadcast_to(v, (SUBLANES, LANES)), 0.0), axis=1, keepdims=True)


def _dec_sig(r, a):
    return (DEC_PAGES - 1 - r) * HEADS + a


def _dec_group_copies(pt_ref, hbm, bufs, sems, seq, g, slot):
    (ck_hbm, cv_hbm, clf_hbm), (kbuf, vbuf, lfbuf) = hbm, bufs
    n_pages = pt_ref.shape[1]
    out = []
    for r in range(DEC_PAGES):
        page = pt_ref[seq, n_pages - 1 - (g * DEC_PAGES + r)]
        row0 = pl.multiple_of(page * PAGE_ROWS, PAGE_ROWS)
        out.append(pltpu.make_async_copy(ck_hbm.at[pl.ds(row0, PAGE_ROWS), :], kbuf.at[slot, r], sems.at[0, slot]))
        out.append(pltpu.make_async_copy(cv_hbm.at[pl.ds(row0, PAGE_ROWS), :], vbuf.at[slot, r], sems.at[1, slot]))
        out.append(pltpu.make_async_copy(clf_hbm.at[page], lfbuf.at[slot, pl.ds(_dec_sig(r, 0), HEADS), :],
                                         sems.at[2, slot]))
    return out


def _dec_start(b, pt_ref, hbm, bufs, sems):
    @pl.when(b == 0)
    def _():
        for c in _dec_group_copies(pt_ref, hbm, bufs, sems, 0, 0, 0):
            c.start()

    for c in _dec_group_copies(pt_ref, hbm, bufs, sems, b, 1, 1):
        c.start()


def _dec_finish(b, nb, pt_ref, q8_ref, kn_ref, vn_ref, gate_ref, lfn_ref, hbm, o_ref, bufs, sems, sd_ref,
                before_last_wait=None, side_stages=None):
    kbuf, vbuf, lfbuf = bufs
    side = {"live": side_stages is not None}

    def advance_side(r):
        if side["live"] and r % DEC_SIDE_EVERY == DEC_SIDE_EVERY - 1:
            if next(side_stages, None) is _GDN_TAIL:
                side["live"] = False

    if side["live"]:
        next(side_stages)
    n_pages = pt_ref.shape[1]
    gp = DEC_PAGES
    ng = n_pages // gp
    nrow = gp * HEADS
    sig_of = _dec_sig
    group_copies = functools.partial(_dec_group_copies, pt_ref, hbm, bufs, sems)

    ri = lax.broadcasted_iota(jnp.int32, (LANES, LANES), 0)
    ci = lax.broadcasted_iota(jnp.int32, (LANES, LANES), 1)
    eye = ri == ci
    same_head = (ri & (HEADS - 1)) == (ci & (HEADS - 1))
    eye8 = (lax.broadcasted_iota(jnp.int32, (SUBLANES, LANES), 0)
            == lax.broadcasted_iota(jnp.int32, (SUBLANES, LANES), 1))
    si = lax.broadcasted_iota(jnp.int32, (nrow, nrow), 0)
    sj = lax.broadcasted_iota(jnp.int32, (nrow, nrow), 1)
    ones = jnp.ones((LANES, LANES), BF16)
    qrep = jnp.concatenate([q8_ref[0]] * (LANES // Q_ROWS), axis=0)

    m_run = jnp.full((1, LANES), NEG_BIG, F32)
    l_run = jnp.zeros((1, LANES), F32)
    acc = jnp.zeros((SUBLANES, HEAD_DIM), F32)
    carry = lfn_ref[0]
    for g in range(ng):
        slot = g % 2
        if 0 < g < ng - 1:
            for c in group_copies(b, g + 1, 1 - slot):
                c.start()
        elif g == ng - 1:
            @pl.when(b + 1 < nb)
            def _():
                for c in group_copies(b + 1, 0, 1 - slot):
                    c.start()
            if before_last_wait is not None:
                before_last_wait()
        for c in group_copies(b, g, slot):
            c.wait()

        lf = lfbuf[slot]
        within = _mm_exact_lhs(lf, _ones_mask(same_head & (ri > ci)))
        tot = _mm_exact_lhs(lf, _ones_mask(same_head))
        later = _mm_exact_rhs(_ones_mask(sj > si), tot)
        bias = (within + later + carry) * LOG2E
        carry = carry + jnp.sum(tot, axis=0, keepdims=True)

        for r in range(gp):
            s2 = _mm_nt(kbuf[slot, r], qrep)
            for a in range(HEADS):
                blk = s2[a * LANES:(a + 1) * LANES, :]
                sig = sig_of(r, a)
                sd_ref[sig:sig + 1, :] = jnp.sum(jnp.where(eye, blk, 0.0), axis=0, keepdims=True)
            advance_side(r)
        for sub in range(gp // DEC_SUB):
            pages = range(sub * DEC_SUB, (sub + 1) * DEC_SUB)
            lo = sig_of(pages[-1], 0)
            rows = slice(lo, lo + DEC_SUB * HEADS)
            tt = sd_ref[rows, :] + bias[rows, :]
            mx = jnp.max(tt, axis=0, keepdims=True)
            mx = _class_reduce(jnp.broadcast_to(mx, (SUBLANES, LANES)), jnp.maximum)[0:1, :]
            m_new = jnp.maximum(m_run, mx)
            pf = jnp.exp2(tt - m_new).astype(BF16).astype(F32)
            alpha = jnp.exp2(m_run - m_new)
            l_run = alpha * l_run + jnp.sum(pf, axis=0, keepdims=True)
            m_run = m_new
            acc = acc * _lane_to_col(alpha, eye8)
            for r in pages:
                parts = []
                for a in range(HEADS):
                    sig = sig_of(r, a) - lo
                    prow = jnp.broadcast_to(pf[sig:sig + 1, :], (LANES, LANES))
                    parts.append(jnp.where(eye, prow, 0.0).astype(BF16))
                pcol = jnp.dot(jnp.concatenate(parts, axis=0), ones, preferred_element_type=F32)
                pv = pcol * vbuf[slot, r]
                acc = acc + jnp.sum(pv.reshape(PAGE_ROWS // SUBLANES, SUBLANES, LANES), axis=0)
                advance_side(r)
    if side_stages is not None:
        _run(side_stages)

    acc4 = acc[0:HEADS, :] + acc[HEADS:2 * HEADS, :]
    mcol = _lane_to_col(m_run, eye8)[0:HEADS, :]
    lsum = _class_reduce(jnp.broadcast_to(l_run, (SUBLANES, LANES)), jnp.add)[0:1, :]
    lcol = _lane_to_col(lsum, eye8)[0:HEADS, :]
    q4 = q8_ref[0][0:HEADS, :].astype(F32)
    s_new = jnp.sum(q4 * kn_ref[0], axis=-1, keepdims=True)
    m_f = jnp.maximum(mcol, s_new)
    a1 = jnp.exp2(mcol - m_f)
    a2 = jnp.exp2(s_new - m_f)
    o = (acc4 * a1 + a2 * vn_ref[0]) / (lcol * a1 + a2)
    o_ref[0] = o * gate_ref[0].astype(F32)


def _mixer_kernel(pt_ref, ab_ref, qi_ref, kj_ref,
                  q8_ref, kn_ref, vn_ref, gate_ref, lfn_ref, ck_hbm, cv_hbm, clf_hbm,
                  aq_ref, ak_ref, av_ref, ack_ref, acq_ref, agate_ref,
                  gqkv_ref, ggz_ref, gcol_ref, grow_ref, gcw_ref, ggn_ref,
                  o_ref, ao_ref, go_ref, gssm_ref, gconv_ref,
                  kbuf, vbuf, lfbuf, sems, sd_ref, m_ref, l_ref, acc_ref, cqb_ref,
                  gs_ref, gx_ref, gq_ref, gk_ref, gv_ref, *, n_att, n_gdn, gdn_blocks):
    del ab_ref
    step = pl.program_id(0)
    n_seq = pt_ref.shape[0]
    hbm, bufs = (ck_hbm, cv_hbm, clf_hbm), (kbuf, vbuf, lfbuf)

    def gdn_stages():
        return _gdn_block_stages(lax.rem(step, gdn_blocks), gdn_blocks, gqkv_ref, ggz_ref, gcol_ref, grow_ref,
                                 gcw_ref, ggn_ref, go_ref, gssm_ref, gconv_ref, gs_ref, gx_ref, gq_ref, gk_ref, gv_ref)

    @pl.when(step < n_seq)
    def _():
        _dec_start(step, pt_ref, hbm, bufs, sems)

    @pl.when(step < n_att)
    def _():
        _att_step(qi_ref[step], kj_ref[step], aq_ref, ak_ref, av_ref, ack_ref, acq_ref, agate_ref, ao_ref,
                  m_ref, l_ref, acc_ref, cqb_ref)

    dec_args = (step, n_seq, pt_ref, q8_ref, kn_ref, vn_ref, gate_ref, lfn_ref, hbm, o_ref, bufs, sems, sd_ref)
    if n_gdn == n_seq:
        @pl.when(step < n_seq)
        def _():
            _dec_finish(*dec_args, side_stages=gdn_stages())
    else:
        @pl.when(step < n_seq)
        def _():
            _dec_finish(*dec_args, before_last_wait=lambda: pl.when(step < n_gdn)(lambda: _run(gdn_stages())))

        @pl.when((step >= n_seq) & (step < n_gdn))
        def _():
            _run(gdn_stages())


def _mixers(page_table, q8, k_new, v_new, gate, lf_new, cache_k2, cache_v2, cache_lf3,
            aq, ak, av, row, col, agate, gqkv, ggz, conv_w, gdn_norm_g, batch, seq):
    nb, n_pages = page_table.shape
    gp = DEC_PAGES
    assert n_pages % (2 * gp) == 0, "an even number of page groups per sequence keeps the buffer slots static"
    n = seq // ATT_T
    steps = [(bb, i, j) for bb in range(batch) for i in range(n) for j in range(i + 1)]
    n_att = len(steps)
    gdn_blocks = seq // GDN_TB
    n_gdn = batch * gdn_blocks
    n_steps = max(nb, n_att, n_gdn)
    steps = steps + [steps[-1]] * (n_steps - n_att)
    ab_tab, qi_tab, kj_tab = (jnp.asarray([s[k] for s in steps], jnp.int32) for k in range(3))

    per_b = lambda s, *_: (jnp.minimum(s, nb - 1), 0, 0)
    qmap = lambda s, pt, ab, qi, kj: (ab[s] * n + qi[s], 0)
    kmap = lambda s, pt, ab, qi, kj: (ab[s] * n + kj[s], 0)
    gblk = lambda s: jnp.minimum(s, n_gdn - 1)
    grow = lambda s, *_: (gblk(s), 0)
    gseq = lambda s, *_: (gblk(s) // gdn_blocks, 0, 0)
    const2 = lambda s, *_: (0, 0)
    hbm = pl.BlockSpec(memory_space=pl.ANY)
    grid_spec = pltpu.PrefetchScalarGridSpec(
        num_scalar_prefetch=4,
        grid=(n_steps,),
        in_specs=[pl.BlockSpec((1, Q_ROWS, HEAD_DIM), per_b),
                  pl.BlockSpec((1, HEADS, HEAD_DIM), per_b),
                  pl.BlockSpec((1, HEADS, HEAD_DIM), per_b),
                  pl.BlockSpec((1, HEADS, HEAD_DIM), per_b),
                  pl.BlockSpec((1, 1, LANES), per_b),
                  hbm, hbm, hbm,
                  pl.BlockSpec((ATT_T, WIDTH), qmap),
                  pl.BlockSpec((ATT_T, WIDTH), kmap),
                  pl.BlockSpec((ATT_T, WIDTH), kmap),
                  pl.BlockSpec((1, 2 * SUBLANES, ATT_T), lambda s, pt, ab, qi, kj: (ab[s], 0, kj[s])),
                  pl.BlockSpec((ATT_T, LANES), qmap),
                  pl.BlockSpec((ATT_T, WIDTH), qmap),
                  pl.BlockSpec((GDN_TB, CONV_DIM), grow),
                  pl.BlockSpec((GDN_TB, WIDTH), grow),
                  pl.BlockSpec((GDN_TB, LANES), grow),
                  pl.BlockSpec((1, 2 * SUBLANES, GDN_TB),
                               lambda s, *_: (gblk(s) // gdn_blocks, 0, gblk(s) % gdn_blocks)),
                  pl.BlockSpec((CONV_W, CONV_DIM), const2),
                  pl.BlockSpec((1, HEAD_DIM), const2)],
        out_specs=[pl.BlockSpec((1, HEADS, HEAD_DIM), per_b),
                   pl.BlockSpec((ATT_T, WIDTH), qmap),
                   pl.BlockSpec((GDN_TB, WIDTH), grow),
                   pl.BlockSpec((1, HEADS, HEAD_DIM, HEAD_DIM), lambda s, *_: (gblk(s) // gdn_blocks, 0, 0, 0)),
                   pl.BlockSpec((1, CONV_W - 1, CONV_DIM), gseq)],
        scratch_shapes=[pltpu.VMEM((2, gp, PAGE_ROWS, HEAD_DIM), F32),
                        pltpu.VMEM((2, gp, PAGE_ROWS, HEAD_DIM), F32),
                        pltpu.VMEM((2, gp * HEADS, LANES), F32),
                        pltpu.SemaphoreType.DMA((3, 2)),
                        pltpu.VMEM((gp * HEADS, LANES), F32)]
                       + [pltpu.VMEM((HEADS, ATT_T, LANES), F32)] * 4
                       + [pltpu.VMEM((HEADS, HEAD_DIM, HEAD_DIM), F32),
                          pltpu.VMEM((GDN_TB + SUBLANES, CONV_DIM), F32)]
                       + [pltpu.VMEM((GDN_TB, WIDTH), F32)] * 3)
    return pl.pallas_call(
        functools.partial(_mixer_kernel, n_att=n_att, n_gdn=n_gdn, gdn_blocks=gdn_blocks),
        grid_spec=grid_spec,
        out_shape=[jax.ShapeDtypeStruct((nb, HEADS, HEAD_DIM), F32),
                   jax.ShapeDtypeStruct((batch * seq, WIDTH), BF16),
                   jax.ShapeDtypeStruct((batch * seq, WIDTH), BF16),
                   jax.ShapeDtypeStruct((batch, HEADS, HEAD_DIM, HEAD_DIM), F32),
                   jax.ShapeDtypeStruct((batch, CONV_W - 1, CONV_DIM), F32)],
        compiler_params=pltpu.CompilerParams(dimension_semantics=("arbitrary",),
                                             vmem_limit_bytes=DEC_VMEM_LIMIT),
        name="mixers",
    )(page_table, ab_tab, qi_tab, kj_tab, q8, k_new, v_new, gate, lf_new, cache_k2, cache_v2, cache_lf3,
      aq, ak, av, row, col, agate, gqkv, ggz, col, row, conv_w, gdn_norm_g)


def _layer(x_prompt, x_sample, cache_k, cache_v, cache_logf, state_ssm, state_conv, page_table,
           c_prompt, c_sample, norm_g, w_ada, b_ada, w_in, b_f, conv_w, a_log, dt_bias,
           gdn_norm_g, q_norm_g, k_norm_g, w_out):
    batch, seq, _ = x_prompt.shape
    nb = x_sample.shape[0]
    w = WIDTH
    o_small = 4 * w
    o_fox = o_small + 2 * HEADS
    w_big = jnp.concatenate([w_in[:, 0:o_small], w_in[:, o_fox:o_fox + 4 * w]], axis=1).astype(BF16)
    w_small = jnp.concatenate([w_in[:, o_small:o_fox], w_in[:, o_fox + 4 * w:],
                               jnp.zeros((D_MODEL, LANES - 3 * HEADS), F32)], axis=1).astype(BF16)
    w_out_b = w_out.astype(BF16)
    par = jnp.zeros((SUBLANES, LANES), F32)
    par = par.at[0, LANE_G:LANE_G + HEADS].set(a_log)
    par = par.at[1, LANE_G:LANE_G + HEADS].set(dt_bias)
    par = par.at[2, LANE_F:LANE_F + HEADS].set(b_f)
    ng2, qg2, kg2, gg2 = (a.reshape(1, -1) for a in (norm_g, q_norm_g, k_norm_g, gdn_norm_g))

    cp8 = jnp.pad(c_prompt, ((0, SUBLANES - batch), (0, 0)))
    mod_p, mod_s = _ada_mod(cp8, c_sample, w_ada, b_ada.reshape(1, -1))
    mod_p4 = mod_p[:, 0:batch].reshape(3, batch, 1, D_MODEL)
    mod_s4 = mod_s.reshape(3, 1, nb, D_MODEL)

    xp = x_prompt.reshape(batch * seq, D_MODEL)
    qkv, gz, small, fq, fk, fkb, fv, fvb, fzg = _inproj(xp, mod_p4, False, PROMPT_TM, ng2, w_big, w_small, qg2, kg2)
    col, row = _small_prep(small, par, batch, seq)
    k_p = fk.reshape(batch, seq, HEADS, HEAD_DIM)
    v_p = fv.reshape(batch, seq, HEADS, HEAD_DIM)
    lf_p = col[:, LANE_LF:LANE_LF + HEADS].reshape(batch, seq, HEADS)

    xs = x_sample.reshape(nb, D_MODEL)
    qkv_s, gz_s, small_s, fq_s, fk_s, _, fv_s, _, fzg_s = _inproj(xs, mod_s4, True, nb, ng2, w_big, w_small, qg2, kg2)
    qn_s, kn_s, vn_s, sc_s = _sample_mid(qkv_s, state_conv[:, 0], state_conv[:, 1], state_conv[:, 2],
                                         conv_w, small_s, par)
    h3 = lambda a: a.reshape(nb, HEADS, HEAD_DIM)
    bcast = lambda a: jnp.broadcast_to(a[:, :, None], (nb, HEADS, HEAD_DIM))
    t8 = jnp.stack([h3(kn_s), h3(qn_s), h3(vn_s), bcast(sc_s[:, LANE_B:LANE_B + HEADS]),
                    bcast(sc_s[:, LANE_G:LANE_G + HEADS])]
                   + [jnp.zeros((nb, HEADS, HEAD_DIM), F32)] * (SUBLANES - 5), axis=2)
    o_a_s, ssm_s = _gdn_step(t8, state_ssm, h3(gz_s), gg2)
    lf_s = sc_s[:, LANE_F:LANE_F + HEADS]
    q8 = jnp.tile(h3(fq_s), (1, Q_ROWS // HEADS, 1))
    lf_new = jnp.tile(lf_s, (1, LANES // HEADS)).reshape(nb, 1, LANES)
    n_pool = cache_k.shape[0]
    o_b_s, o_b, o_a, ssm_p, conv_p = _mixers(
        page_table, q8, h3(fk_s), h3(fv_s), h3(fzg_s), lf_new,
        cache_k.reshape(n_pool * PAGE_ROWS, HEAD_DIM), cache_v.reshape(n_pool * PAGE_ROWS, HEAD_DIM),
        cache_logf.reshape(n_pool, HEADS, LANES),
        fq, fkb, fvb, row, col, fzg, qkv, gz, conv_w, gg2, batch, seq)
    y_p = _outproj(xp, o_a, o_b, mod_p4, False, OUT_TM, w_out_b).reshape(batch, seq, D_MODEL)
    y_s = _outproj(xs, o_a_s.reshape(nb, w), o_b_s.reshape(nb, w), mod_s4, True, nb, w_out_b).reshape(nb, 1, D_MODEL)
    conv_s = jnp.stack([state_conv[:, 1], state_conv[:, 2], qkv_s], axis=1)
    return (y_p, y_s, k_p, v_p, lf_p, ssm_p, conv_p,
            h3(fk_s).reshape(nb, 1, HEADS, HEAD_DIM), h3(fv_s).reshape(nb, 1, HEADS, HEAD_DIM),
            lf_s.reshape(nb, 1, HEADS), ssm_s, conv_s)


def kernel(x_prompt, x_sample, cache_k, cache_v, cache_logf, state_ssm, state_conv, page_table, c_prompt, c_sample, norm_g, w_ada, b_ada, w_in, b_f, conv_w, a_log, dt_bias, gdn_norm_g, q_norm_g, k_norm_g, w_out):
    assert w_ada.shape[0] == 1, "single layer"
    outs = _layer(x_prompt, x_sample, cache_k[0], cache_v[0], cache_logf[0], state_ssm[0], state_conv[0],
                  page_table, c_prompt, c_sample, norm_g[0], w_ada[0], b_ada[0], w_in[0], b_f[0], conv_w[0],
                  a_log[0], dt_bias[0], gdn_norm_g[0], q_norm_g[0], k_norm_g[0], w_out[0])
    y_p, y_s = outs[0], outs[1]
    return (y_p, y_s) + tuple(o[None] for o in outs[2:])
```

```python
import functools

import jax
import jax.numpy as jnp
from jax import lax
from jax.experimental import pallas as pl
from jax.experimental.pallas import tpu as pltpu

F32 = jnp.float32
BF16 = jnp.bfloat16

D_MODEL = 1024
HEADS = 4
HEAD_DIM = 128
WIDTH = HEADS * HEAD_DIM
CONV_W = 4
CONV_DIM = 3 * WIDTH
CHUNK = 64
PAGE_SIZE = 128
PAGE_ROWS = PAGE_SIZE * HEADS
NORM_EPS = 1e-6
LANES = 128
SUBLANES = 8
NEG_BIG = -1e30
LOG2E = 1.4426950408889634
ATT_Q_SCALE = LOG2E * HEAD_DIM ** -0.5
VMEM_LIMIT = 48 * 1024 * 1024

COL_GQKV = 0
COL_GZ = 3 * WIDTH
COL_FQ = 4 * WIDTH
COL_FK = 5 * WIDTH
COL_FV = 6 * WIDTH
COL_FZ = 7 * WIDTH
BIG_COLS = 8 * WIDTH
LANE_B, LANE_G, LANE_F, LANE_LF = 0, 4, 8, 12

PROMPT_TM = 256
OUT_TM = 512
SMALL_TB = 512
GDN_TB = 128
ATT_T = 512
DEC_PAGES = 32
DEC_SUB = 16
DEC_SIDE_EVERY = 8
DEC_VMEM_LIMIT = 56 * 1024 * 1024
Q_ROWS = 16
STEP_BATCH = 4


def _mm(a, b):
    return jnp.dot(a.astype(BF16), b.astype(BF16), preferred_element_type=F32)


def _mm_nt(a, b):
    return lax.dot_general(a.astype(BF16), b.astype(BF16), (((1,), (1,)), ((), ())),
                           preferred_element_type=F32)


def _mm_tn(a, b):
    return lax.dot_general(a.astype(BF16), b.astype(BF16), (((0,), (0,)), ((), ())),
                           preferred_element_type=F32)


def _mm_exact_rhs(mat_b, y):
    hi = y.astype(BF16)
    r1 = y - hi.astype(F32)
    mid = r1.astype(BF16)
    lo = (r1 - mid.astype(F32)).astype(BF16)
    dot = functools.partial(jnp.dot, preferred_element_type=F32)
    return dot(mat_b, hi) + dot(mat_b, mid) + dot(mat_b, lo)


def _mm_exact_lhs(y, mat_b):
    hi = y.astype(BF16)
    r1 = y - hi.astype(F32)
    mid = r1.astype(BF16)
    lo = (r1 - mid.astype(F32)).astype(BF16)
    dot = functools.partial(jnp.dot, preferred_element_type=F32)
    return dot(hi, mat_b) + dot(mid, mat_b) + dot(lo, mat_b)


def _sigmoid(x):
    return 1.0 / (1.0 + jnp.exp(-x))


def _silu(x):
    return x * _sigmoid(x)


def _softplus(x):
    return jnp.maximum(x, 0.0) + jnp.log(1.0 + jnp.exp(-jnp.abs(x)))


def _ones_mask(mask):
    return jnp.where(mask, 1.0, 0.0).astype(BF16)


def _params(n_axes):
    return pltpu.CompilerParams(dimension_semantics=("arbitrary",) * n_axes,
                                vmem_limit_bytes=VMEM_LIMIT)


def _ada_kernel(cp_ref, cs_ref, w_ref, b_ref, op_ref, os_ref):
    w = w_ref[...].astype(BF16)
    b = b_ref[...]
    op_ref[0] = jnp.dot(_silu(cp_ref[...]).astype(BF16), w, preferred_element_type=F32) + b
    os_ref[0] = jnp.dot(_silu(cs_ref[...]).astype(BF16), w, preferred_element_type=F32) + b


def _ada_mod(c_prompt8, c_sample, w_ada, b_ada):
    nb = c_sample.shape[0]
    return pl.pallas_call(
        _ada_kernel,
        grid=(3,),
        in_specs=[pl.BlockSpec((SUBLANES, D_MODEL), lambda k: (0, 0)),
                  pl.BlockSpec((nb, D_MODEL), lambda k: (0, 0)),
                  pl.BlockSpec((D_MODEL, D_MODEL), lambda k: (0, k)),
                  pl.BlockSpec((1, D_MODEL), lambda k: (0, k))],
        out_specs=[pl.BlockSpec((1, SUBLANES, D_MODEL), lambda k: (k, 0, 0)),
                   pl.BlockSpec((1, nb, D_MODEL), lambda k: (k, 0, 0))],
        out_shape=[jax.ShapeDtypeStruct((3, SUBLANES, D_MODEL), F32),
                   jax.ShapeDtypeStruct((3, nb, D_MODEL), F32)],
        compiler_params=_params(1),
        name="ada_mod",
    )(c_prompt8, c_sample, w_ada, b_ada)


def _conv_silu_norm(xbuf_ref, cw_ref, out_ref, tb, part):
    halo = SUBLANES
    lo = halo - (CONV_W - 1)
    scale = HEAD_DIM ** -0.5
    for cb in range(part * HEADS, (part + 1) * HEADS):
        sl = slice(cb * LANES, (cb + 1) * LANES)
        acc = xbuf_ref[lo:lo + tb, sl] * cw_ref[0:1, sl]
        for i in range(1, CONV_W):
            acc = acc + xbuf_ref[lo + i:lo + i + tb, sl] * cw_ref[i:i + 1, sl]
        act = _silu(acc)
        if part == 0:
            act = act * lax.rsqrt(jnp.sum(act * act, axis=-1, keepdims=True) + NORM_EPS) * scale
        elif part == 1:
            act = act * lax.rsqrt(jnp.sum(act * act, axis=-1, keepdims=True) + NORM_EPS)
        out_ref[:, sl] = act


def _inproj_kernel(x_ref, sh_ref, sc_ref, ng_ref, wb_ref, ws_ref, qg_ref, kg_ref, *refs, conv_blocks):
    if conv_blocks is None:
        qkv_ref, gz_ref, sm_ref, fq_ref, fk_ref, fkb_ref, fv_ref, fvb_ref, fzg_ref = refs
    else:
        cw_ref, qkv_ref, gz_ref, sm_ref, fq_ref, fk_ref, fkb_ref, fv_ref, fvb_ref, fzg_ref, tail_ref, xbuf_ref = refs
    x = x_ref[...]
    tm = x.shape[0]
    y = x * lax.rsqrt(jnp.mean(x * x, axis=-1, keepdims=True) + NORM_EPS)
    hdn = (y * ng_ref[...]) * (1.0 + sc_ref[0, 0]) + sh_ref[0, 0]
    hb = hdn.astype(BF16)
    dot = functools.partial(jnp.dot, preferred_element_type=F32)
    third = lambda p: slice(COL_GQKV + p * WIDTH, COL_GQKV + (p + 1) * WIDTH)
    if conv_blocks is None:
        qkv_ref[...] = dot(hb, wb_ref[:, COL_GQKV:COL_GZ])
        conv_part = lambda p: None
    else:
        halo = SUBLANES
        blk = lax.rem(pl.program_id(0), conv_blocks)

        @pl.when(blk == 0)
        def _():
            xbuf_ref[0:halo, :] = jnp.zeros((halo, CONV_DIM), F32)

        conv_part = lambda p: _conv_silu_norm(xbuf_ref, cw_ref, qkv_ref, tm, p)
        for p in range(3):
            xbuf_ref[halo:halo + tm, third(p)] = dot(hb, wb_ref[:, third(p)])
    gz_ref[...] = dot(hb, wb_ref[:, COL_GZ:COL_FQ])
    sm_ref[...] = dot(hb, ws_ref[...])
    fq = dot(hb, wb_ref[:, COL_FQ:COL_FK])
    fk = dot(hb, wb_ref[:, COL_FK:COL_FV])
    fv = dot(hb, wb_ref[:, COL_FV:COL_FZ])
    fz = dot(hb, wb_ref[:, COL_FZ:BIG_COLS])
    for p in range(3):
        conv_part(p)
    fvb_ref[...] = fv.astype(BF16)
    fzg_ref[...] = _silu(fz).astype(BF16)
    scale = ATT_Q_SCALE
    for h in range(HEADS):
        sl = slice(h * HEAD_DIM, (h + 1) * HEAD_DIM)
        qh = fq[:, sl]
        qn = (qh * lax.rsqrt(jnp.mean(qh * qh, axis=-1, keepdims=True) + NORM_EPS)) * qg_ref[...]
        fq_ref[:, sl] = (qn * scale).astype(BF16)
        kh = fk[:, sl]
        kn = (kh * lax.rsqrt(jnp.mean(kh * kh, axis=-1, keepdims=True) + NORM_EPS)) * kg_ref[...]
        fkb_ref[:, sl] = kn.astype(BF16)
        fk_ref[pl.ds(h, tm, stride=HEADS), :] = kn
        fv_ref[pl.ds(h, tm, stride=HEADS), :] = fv[:, sl]
    if conv_blocks is not None:
        tail = xbuf_ref[tm:tm + halo, :]
        xbuf_ref[0:halo, :] = tail

        @pl.when(blk == conv_blocks - 1)
        def _():
            tail_ref[0] = tail[halo - (CONV_W - 1):halo, :]


def _inproj(x2d, mod4, per_row_mod, tm, norm_g, w_big, w_small, q_norm_g, k_norm_g, conv_w=None):
    rows = x2d.shape[0]
    nblk = rows // tm
    if per_row_mod:
        blocks_per_batch = None
        mod_block = (1, 1, tm, D_MODEL)
        sh_map = lambda i: (0, 0, i, 0)
        sc_map = lambda i: (1, 0, i, 0)
    else:
        blocks_per_batch = nblk // mod4.shape[1]
        mod_block = (1, 1, 1, D_MODEL)
        sh_map = lambda i: (0, i // blocks_per_batch, 0, 0)
        sc_map = lambda i: (1, i // blocks_per_batch, 0, 0)
    row_spec = lambda n: pl.BlockSpec((tm, n), lambda i: (i, 0))
    const = lambda shape: pl.BlockSpec(shape, lambda i: (0,) * len(shape))
    outs = [(CONV_DIM, F32, 1), (WIDTH, F32, 1), (LANES, F32, 1), (WIDTH, BF16, 1), (HEAD_DIM, F32, HEADS),
            (WIDTH, BF16, 1), (HEAD_DIM, F32, HEADS), (WIDTH, BF16, 1), (WIDTH, BF16, 1)]
    in_specs = [row_spec(D_MODEL),
                pl.BlockSpec(mod_block, sh_map), pl.BlockSpec(mod_block, sc_map),
                const((1, D_MODEL)), const((D_MODEL, BIG_COLS)), const((D_MODEL, LANES)),
                const((1, HEAD_DIM)), const((1, HEAD_DIM))]
    out_specs = [pl.BlockSpec((tm * r, n), lambda i: (i, 0)) for n, _, r in outs]
    out_shape = [jax.ShapeDtypeStruct((rows * r, n), dt) for n, dt, r in outs]
    args = [x2d, mod4, mod4, norm_g, w_big, w_small, q_norm_g, k_norm_g]
    scratch = []
    if conv_w is not None:
        assert not per_row_mod
        in_specs.append(const((CONV_W, CONV_DIM)))
        args.append(conv_w)
        out_specs.append(pl.BlockSpec((1, CONV_W - 1, CONV_DIM), lambda i: (i // blocks_per_batch, 0, 0)))
        out_shape.append(jax.ShapeDtypeStruct((mod4.shape[1], CONV_W - 1, CONV_DIM), F32))
        scratch.append(pltpu.VMEM((tm + SUBLANES, CONV_DIM), F32))
    return pl.pallas_call(
        functools.partial(_inproj_kernel, conv_blocks=blocks_per_batch if conv_w is not None else None),
        grid=(nblk,),
        in_specs=in_specs,
        out_specs=out_specs,
        out_shape=out_shape,
        scratch_shapes=scratch,
        compiler_params=_params(1),
        name="in_proj",
    )(*args)


def _token_scalars(x, par):
    beta = _sigmoid(x)
    g = -jnp.exp(par[0:1, :]) * _softplus(x + par[1:2, :])
    lf = -_softplus(-(x + par[2:3, :]))
    return beta, g, lf


def _small_kernel(sm_ref, par_ref, col_ref, row_ref, carry_ref):
    tb = SMALL_TB

    @pl.when(pl.program_id(1) == 0)
    def _():
        carry_ref[...] = jnp.zeros_like(carry_ref)

    x = sm_ref[...]
    beta, g, lf = _token_scalars(x, par_ref[...])
    lane = lax.broadcasted_iota(jnp.int32, (tb, LANES), 1)
    is_b = lane < LANE_G
    is_g = (lane >= LANE_G) & (lane < LANE_F)
    is_f = (lane >= LANE_F) & (lane < LANE_LF)
    gm = jnp.where(is_g, g, 0.0)
    fm = jnp.where(is_f, lf, 0.0)
    r = lax.broadcasted_iota(jnp.int32, (tb, tb), 0)
    c = lax.broadcasted_iota(jnp.int32, (tb, tb), 1)
    tri = c <= r
    same_chunk = jnp.right_shift(r, 6) == jnp.right_shift(c, 6)
    gcum = _mm_exact_rhs(_ones_mask(tri & same_chunk), gm)
    cum = _mm_exact_rhs(_ones_mask(tri), fm) + carry_ref[...]
    cum = jnp.where(is_f, cum, 0.0)
    carry_ref[...] = cum[tb - 1:tb, :]
    col = jnp.where(is_b, beta, 0.0) + gcum + cum + pltpu.roll(fm, LANE_LF - LANE_F, axis=1)
    col_ref[...] = col
    for i in range(tb // LANES):
        t = col[i * LANES:(i + 1) * LANES, :]
        row_ref[0, :, i * LANES:(i + 1) * LANES] = t.T[0:2 * SUBLANES, :]


def _small_prep(small, par, batch, seq):
    nt = seq // SMALL_TB
    return pl.pallas_call(
        _small_kernel,
        grid=(batch, nt),
        in_specs=[pl.BlockSpec((SMALL_TB, LANES), lambda b, t: (b * nt + t, 0)),
                  pl.BlockSpec((SUBLANES, LANES), lambda b, t: (0, 0))],
        out_specs=[pl.BlockSpec((SMALL_TB, LANES), lambda b, t: (b * nt + t, 0)),
                   pl.BlockSpec((1, 2 * SUBLANES, SMALL_TB), lambda b, t: (b, 0, t))],
        out_shape=[jax.ShapeDtypeStruct((batch * seq, LANES), F32),
                   jax.ShapeDtypeStruct((batch, 2 * SUBLANES, seq), F32)],
        scratch_shapes=[pltpu.VMEM((1, LANES), F32)],
        compiler_params=_params(2),
        name="token_scalars",
    )(small, par)


_GDN_TAIL = "tail"


def _run(stages):
    try:
        while True:
            next(stages)
    except StopIteration as done:
        return done.value


def _unit_lower_inverses(ms, eye, ri, ci):
    base = 3
    same = jnp.right_shift(ri, base) == jnp.right_shift(ci, base)
    xps = [-jnp.where(same, m, 0.0) for m in ms]
    ps = [eye + xp for xp in xps]
    for _ in range(base - 1):
        xps = [_mm(xp, xp) for xp in xps]
        yield
        ps = [p + _mm(p, xp) for p, xp in zip(ps, xps)]
        yield
    size = base
    while (1 << size) < CHUNK:
        off = ((jnp.right_shift(ri, size + 1) == jnp.right_shift(ci, size + 1))
               & (jnp.right_shift(ri, size) != jnp.right_shift(ci, size)))
        pcs = [_mm(p, jnp.where(off, m, 0.0)) for p, m in zip(ps, ms)]
        yield
        ps = [p - _mm(pc, p) for p, pc in zip(ps, pcs)]
        yield
        size += 1
    return ps


def _gdn_chunk_preps(qs, ks, vs, betas, gcs, grs, incl, strict, eye, ri, ci):
    decays = [jnp.where(incl, jnp.exp(jnp.where(incl, gc - gr, 0.0)), 0.0) for gc, gr in zip(gcs, grs)]
    kbs = [k * beta for k, beta in zip(ks, betas)]
    boths = [_mm_nt(jnp.concatenate([kb, q], axis=0), k) for kb, q, k in zip(kbs, qs, ks)]
    yield
    ms = [jnp.where(strict, both[0:CHUNK] * decay, 0.0) for both, decay in zip(boths, decays)]
    a_intras = [both[CHUNK:2 * CHUNK] * decay for both, decay in zip(boths, decays)]
    tinvs = yield from _unit_lower_inverses(ms, eye, ri, ci)
    egs = [jnp.exp(gc) for gc in gcs]
    uws = [_mm(tinv, jnp.concatenate([v * beta, kb * eg], axis=1))
           for tinv, v, beta, kb, eg in zip(tinvs, vs, betas, kbs, egs)]
    yield
    preps = []
    for uw, q, k, eg, gc, a_intra in zip(uws, qs, ks, egs, gcs, a_intras):
        g_last = gc[CHUNK - 1:CHUNK, :]
        wq = jnp.concatenate([uw[:, HEAD_DIM:2 * HEAD_DIM], q * eg], axis=0)
        preps.append((uw[:, 0:HEAD_DIM], wq, a_intra, k * jnp.exp(g_last - gc), jnp.exp(g_last)))
    return preps


def _gdn_chunk_steps(preps, states):
    boths = [_mm(p[1], s) for p, s in zip(preps, states)]
    yield
    v_news = [p[0] - both[0:CHUNK] for p, both in zip(preps, boths)]
    outs = [both[CHUNK:2 * CHUNK] + _mm(p[2], v_new) for p, both, v_new in zip(preps, boths, v_news)]
    new_states = [s * p[4] + _mm_tn(p[3], v_new) for p, s, v_new in zip(preps, states, v_news)]
    yield
    return outs, new_states


def _gdn_block_stages(t, n_t, qkv_ref, gz_ref, col_ref, row_ref, gn_ref, o_ref, ssm_ref, s_ref):
    tb = GDN_TB

    @pl.when(t == 0)
    def _():
        s_ref[...] = jnp.zeros_like(s_ref)

    yield

    ri = lax.broadcasted_iota(jnp.int32, (CHUNK, CHUNK), 0)
    ci = lax.broadcasted_iota(jnp.int32, (CHUNK, CHUNK), 1)
    incl = ci <= ri
    strict = ci < ri
    eye = jnp.where(ci == ri, 1.0, 0.0)

    colt = col_ref[...]
    rowt = row_ref[0]
    n_cc = tb // CHUNK
    items = [(cc, h) for cc in range(n_cc) for h in range(HEADS)]
    rs = lambda cc: slice(cc * CHUNK, (cc + 1) * CHUNK)
    hs = lambda h: slice(h * HEAD_DIM, (h + 1) * HEAD_DIM)
    part = lambda p, h: slice(p * WIDTH + h * HEAD_DIM, p * WIDTH + (h + 1) * HEAD_DIM)
    preps = yield from _gdn_chunk_preps(
        [qkv_ref[rs(cc), part(0, h)] for cc, h in items],
        [qkv_ref[rs(cc), part(1, h)] for cc, h in items],
        [qkv_ref[rs(cc), part(2, h)] for cc, h in items],
        [colt[rs(cc), LANE_B + h:LANE_B + h + 1] for cc, h in items],
        [colt[rs(cc), LANE_G + h:LANE_G + h + 1] for cc, h in items],
        [rowt[LANE_G + h:LANE_G + h + 1, rs(cc)] for cc, h in items],
        incl, strict, eye, ri, ci)
    states = [s_ref[h] for h in range(HEADS)]
    for cc in range(n_cc):
        outs, states = yield from _gdn_chunk_steps(preps[cc * HEADS:(cc + 1) * HEADS], states)
        for h, o in enumerate(outs):
            on = (o * lax.rsqrt(jnp.mean(o * o, axis=-1, keepdims=True) + NORM_EPS)) * gn_ref[...]
            o_ref[rs(cc), hs(h)] = (on * _silu(gz_ref[rs(cc), hs(h)])).astype(o_ref.dtype)
    for h in range(HEADS):
        s_ref[h] = states[h]
    yield _GDN_TAIL

    @pl.when(t == n_t - 1)
    def _():
        ssm_ref[0] = s_ref[...]


def _att_step(qi, kj, q_ref, k_ref, v_ref, ck_ref, cq_ref, gate_ref, o_ref, m_ref, l_ref, acc_ref, cqb_ref):
    tq = tk = ATT_T
    heads = range(HEADS)
    hs = lambda h: slice(h * HEAD_DIM, (h + 1) * HEAD_DIM)

    @pl.when(kj == 0)
    def _():
        m_ref[...] = jnp.full_like(m_ref, NEG_BIG)
        l_ref[...] = jnp.zeros_like(l_ref)
        acc_ref[...] = jnp.zeros_like(acc_ref)
        for h in heads:
            cqb_ref[h] = jnp.broadcast_to(cq_ref[:, LANE_F + h:LANE_F + h + 1] * LOG2E, (tq, LANES))

    def update(diagonal):
        ss = [lax.dot_general(q_ref[:, hs(h)], k_ref[:, hs(h)], (((1,), (1,)), ((), ())),
                              preferred_element_type=F32) for h in heads]
        tts = [s - ck_ref[0, LANE_F + h:LANE_F + h + 1, :] * LOG2E for h, s in zip(heads, ss)]
        if diagonal:
            visible = (lax.broadcasted_iota(jnp.int32, (tq, tk), 1)
                       <= lax.broadcasted_iota(jnp.int32, (tq, tk), 0))
            tts = [jnp.where(visible, tt, NEG_BIG) for tt in tts]
        cqs = [cqb_ref[h] for h in heads]
        m_olds = [m_ref[h] for h in heads]
        m_news = [jnp.maximum(m_old, jnp.broadcast_to(jnp.max(tt, axis=1, keepdims=True), (tq, LANES)) + cq)
                  for m_old, tt, cq in zip(m_olds, tts, cqs)]
        shifts = [jnp.concatenate([m_new - cq] * (tk // LANES), axis=1) for m_new, cq in zip(m_news, cqs)]
        ps = [jnp.exp2(tt - shift).astype(BF16) for tt, shift in zip(tts, shifts)]
        ones = jnp.ones((tk, HEAD_DIM), BF16)
        pvs = [jnp.dot(p, jnp.concatenate([v_ref[:, hs(h)], ones], axis=1), preferred_element_type=F32)
               for h, p in zip(heads, ps)]
        alphas = [jnp.exp2(m_old - m_new) for m_old, m_new in zip(m_olds, m_news)]
        accs = [alpha * acc_ref[h] + pv[:, 0:HEAD_DIM] for h, alpha, pv in zip(heads, alphas, pvs)]
        ls = [alpha * l_ref[h] + pv[:, HEAD_DIM:2 * HEAD_DIM] for h, alpha, pv in zip(heads, alphas, pvs)]
        return m_news, ls, accs

    @pl.when(kj < qi)
    def _():
        m_news, ls, accs = update(False)
        for h in heads:
            m_ref[h] = m_news[h]
            l_ref[h] = ls[h]
            acc_ref[h] = accs[h]

    @pl.when(kj == qi)
    def _():
        _, ls, accs = update(True)
        for h in heads:
            o_ref[:, hs(h)] = ((accs[h] / ls[h]) * gate_ref[:, hs(h)].astype(F32)).astype(o_ref.dtype)


def _outproj_kernel(x_ref, oa_ref, ob_ref, gate_ref, w_ref, y_ref):
    mix = (jnp.dot(oa_ref[...].astype(BF16), w_ref[0:WIDTH, :], preferred_element_type=F32)
           + jnp.dot(ob_ref[...].astype(BF16), w_ref[WIDTH:2 * WIDTH, :], preferred_element_type=F32))
    y_ref[...] = x_ref[...] + gate_ref[0, 0] * mix


def _outproj(x2d, oa, ob, mod4, per_row_mod, tm, w_out_b):
    rows = x2d.shape[0]
    nblk = rows // tm
    if per_row_mod:
        mod_block = (1, 1, tm, D_MODEL)
        g_map = lambda i: (2, 0, i, 0)
    else:
        blocks_per_batch = nblk // mod4.shape[1]
        mod_block = (1, 1, 1, D_MODEL)
        g_map = lambda i: (2, i // blocks_per_batch, 0, 0)
    row_spec = lambda n: pl.BlockSpec((tm, n), lambda i: (i, 0))
    return pl.pallas_call(
        _outproj_kernel,
        grid=(nblk,),
        in_specs=[row_spec(D_MODEL), row_spec(WIDTH), row_spec(WIDTH),
                  pl.BlockSpec(mod_block, g_map),
                  pl.BlockSpec((2 * WIDTH, D_MODEL), lambda i: (0, 0))],
        out_specs=row_spec(D_MODEL),
        out_shape=jax.ShapeDtypeStruct((rows, D_MODEL), F32),
        compiler_params=_params(1),
        name="out_proj",
    )(x2d, oa, ob, mod4, w_out_b)


def _smid_kernel(u_ref, s0_ref, s1_ref, s2_ref, cw_ref, sm_ref, par_ref,
                 qn_ref, kn_ref, vn_ref, sc_ref):
    conv = (s0_ref[...] * cw_ref[0:1, :] + s1_ref[...] * cw_ref[1:2, :]
            + s2_ref[...] * cw_ref[2:3, :] + u_ref[...] * cw_ref[3:4, :])
    act = _silu(conv)
    for h in range(HEADS):
        hs = slice(h * HEAD_DIM, (h + 1) * HEAD_DIM)
        q = act[:, h * HEAD_DIM:(h + 1) * HEAD_DIM]
        k = act[:, WIDTH + h * HEAD_DIM:WIDTH + (h + 1) * HEAD_DIM]
        qn_ref[:, hs] = q * lax.rsqrt(jnp.sum(q * q, axis=-1, keepdims=True) + NORM_EPS)
        kn_ref[:, hs] = k * lax.rsqrt(jnp.sum(k * k, axis=-1, keepdims=True) + NORM_EPS)
    vn_ref[...] = act[:, 2 * WIDTH:3 * WIDTH]
    beta, g, lf = _token_scalars(sm_ref[...], par_ref[...])
    lane = lax.broadcasted_iota(jnp.int32, sm_ref.shape, 1)
    sc_ref[...] = jnp.where(lane < LANE_G, beta, jnp.where(lane < LANE_F, g, lf))


def _sample_mid(u, s0, s1, s2, conv_w, small, par):
    nb = u.shape[0]
    full = lambda a: pl.BlockSpec(a.shape, lambda: (0,) * a.ndim)
    args = (u, s0, s1, s2, conv_w, small, par)
    return pl.pallas_call(
        _smid_kernel,
        in_specs=[full(a) for a in args],
        out_specs=[pl.BlockSpec((nb, WIDTH), lambda: (0, 0))] * 3 + [pl.BlockSpec((nb, LANES), lambda: (0, 0))],
        out_shape=[jax.ShapeDtypeStruct((nb, WIDTH), F32)] * 3 + [jax.ShapeDtypeStruct((nb, LANES), F32)],
        compiler_params=pltpu.CompilerParams(vmem_limit_bytes=VMEM_LIMIT),
        name="sample_conv",
    )(*args)


def _gdn_step_kernel(t_ref, s_ref, gz_ref, gn_ref, o_ref, sn_ref):
    scale = HEAD_DIM ** -0.5
    row = lax.broadcasted_iota(jnp.int32, (SUBLANES, HEAD_DIM), 0)
    items = [(b, h) for b in range(STEP_BATCH) for h in range(HEADS)]
    ts = [t_ref[b, h] for b, h in items]
    rs = [_mm(t, s_ref[b, h]) for t, (b, h) in zip(ts, items)]
    egs, v_news = [], []
    for t, r, (b, h) in zip(ts, rs, items):
        k, q, v, beta, g = (t[i:i + 1, :] for i in range(5))
        eg = jnp.exp(g)
        v_new = beta * (v - eg * r[0:1, :])
        o = scale * (eg * r[1:2, :] + jnp.sum(q * k, axis=-1, keepdims=True) * v_new)
        on = (o * lax.rsqrt(jnp.mean(o * o, axis=-1, keepdims=True) + NORM_EPS)) * gn_ref[...]
        o_ref[b, h:h + 1, :] = on * _silu(gz_ref[b, h:h + 1, :])
        egs.append(eg)
        v_news.append(v_new)
    outers = [_mm_tn(jnp.where(row == 0, t, 0.0),
                     jnp.where(row == 0, jnp.broadcast_to(v_new, (SUBLANES, HEAD_DIM)), 0.0))
              for t, v_new in zip(ts, v_news)]
    for eg, outer, (b, h) in zip(egs, outers, items):
        sn_ref[b, h] = s_ref[b, h] * eg + outer


def _gdn_step(t8, state, gz3, gdn_norm_g):
    nb = state.shape[0]
    sb = STEP_BATCH
    return pl.pallas_call(
        _gdn_step_kernel,
        grid=(nb // sb,),
        in_specs=[pl.BlockSpec((sb, HEADS, SUBLANES, HEAD_DIM), lambda b: (b, 0, 0, 0)),
                  pl.BlockSpec((sb, HEADS, HEAD_DIM, HEAD_DIM), lambda b: (b, 0, 0, 0)),
                  pl.BlockSpec((sb, HEADS, HEAD_DIM), lambda b: (b, 0, 0)),
                  pl.BlockSpec((1, HEAD_DIM), lambda b: (0, 0))],
        out_specs=[pl.BlockSpec((sb, HEADS, HEAD_DIM), lambda b: (b, 0, 0)),
                   pl.BlockSpec((sb, HEADS, HEAD_DIM, HEAD_DIM), lambda b: (b, 0, 0, 0))],
        out_shape=[jax.ShapeDtypeStruct((nb, HEADS, HEAD_DIM), F32),
                   jax.ShapeDtypeStruct((nb, HEADS, HEAD_DIM, HEAD_DIM), F32)],
        compiler_params=_params(1),
        name="gdn_step",
    )(t8, state, gz3, gdn_norm_g)


def _class_reduce(x, op):
    shift = HEADS
    while shift < LANES:
        x = op(x, pltpu.roll(x, shift, axis=1))
        shift *= 2
    return x


def _lane_to_col(v, eye8):
    return jnp.sum(jnp.where(eye8, jnp.broadcast_to(v, (SUBLANES, LANES)), 0.0), axis=1, keepdims=True)


def _dec_sig(r, a):
    return (DEC_PAGES - 1 - r) * HEADS + a


def _dec_group_copies(pt_ref, hbm, bufs, sems, seq, g, slot):
    (ck_hbm, cv_hbm, clf_hbm), (kbuf, vbuf, lfbuf) = hbm, bufs
    n_pages = pt_ref.shape[1]
    out = []
    for r in range(DEC_PAGES):
        page = pt_ref[seq, n_pages - 1 - (g * DEC_PAGES + r)]
        row0 = pl.multiple_of(page * PAGE_ROWS, PAGE_ROWS)
        out.append(pltpu.make_async_copy(ck_hbm.at[pl.ds(row0, PAGE_ROWS), :], kbuf.at[slot, r], sems.at[0, slot]))
        out.append(pltpu.make_async_copy(cv_hbm.at[pl.ds(row0, PAGE_ROWS), :], vbuf.at[slot, r], sems.at[1, slot]))
        out.append(pltpu.make_async_copy(clf_hbm.at[page], lfbuf.at[slot, pl.ds(_dec_sig(r, 0), HEADS), :],
                                         sems.at[2, slot]))
    return out


def _dec_start(b, pt_ref, hbm, bufs, sems):
    @pl.when(b == 0)
    def _():
        for c in _dec_group_copies(pt_ref, hbm, bufs, sems, 0, 0, 0):
            c.start()

    for c in _dec_group_copies(pt_ref, hbm, bufs, sems, b, 1, 1):
        c.start()


def _dec_finish(b, nb, pt_ref, q8_ref, kn_ref, vn_ref, gate_ref, lfn_ref, hbm, o_ref, bufs, sems, sd_ref,
                before_last_wait=None, side_stages=None):
    kbuf, vbuf, lfbuf = bufs
    side = {"live": side_stages is not None}

    def advance_side(r):
        if side["live"] and r % DEC_SIDE_EVERY == DEC_SIDE_EVERY - 1:
            if next(side_stages, None) is _GDN_TAIL:
                side["live"] = False

    if side["live"]:
        next(side_stages)
    n_pages = pt_ref.shape[1]
    gp = DEC_PAGES
    ng = n_pages // gp
    nrow = gp * HEADS
    sig_of = _dec_sig
    group_copies = functools.partial(_dec_group_copies, pt_ref, hbm, bufs, sems)

    ri = lax.broadcasted_iota(jnp.int32, (LANES, LANES), 0)
    ci = lax.broadcasted_iota(jnp.int32, (LANES, LANES), 1)
    eye = ri == ci
    same_head = (ri & (HEADS - 1)) == (ci & (HEADS - 1))
    eye8 = (lax.broadcasted_iota(jnp.int32, (SUBLANES, LANES), 0)
            == lax.broadcasted_iota(jnp.int32, (SUBLANES, LANES), 1))
    si = lax.broadcasted_iota(jnp.int32, (nrow, nrow), 0)
    sj = lax.broadcasted_iota(jnp.int32, (nrow, nrow), 1)
    ones = jnp.ones((LANES, LANES), BF16)
    qrep = jnp.concatenate([q8_ref[0]] * (LANES // Q_ROWS), axis=0)

    m_run = jnp.full((1, LANES), NEG_BIG, F32)
    l_run = jnp.zeros((1, LANES), F32)
    acc = jnp.zeros((SUBLANES, HEAD_DIM), F32)
    carry = lfn_ref[0]
    for g in range(ng):
        slot = g % 2
        if 0 < g < ng - 1:
            for c in group_copies(b, g + 1, 1 - slot):
                c.start()
        elif g == ng - 1:
            @pl.when(b + 1 < nb)
            def _():
                for c in group_copies(b + 1, 0, 1 - slot):
                    c.start()
            if before_last_wait is not None:
                before_last_wait()
        for c in group_copies(b, g, slot):
            c.wait()

        lf = lfbuf[slot]
        within = _mm_exact_lhs(lf, _ones_mask(same_head & (ri > ci)))
        tot = _mm_exact_lhs(lf, _ones_mask(same_head))
        later = _mm_exact_rhs(_ones_mask(sj > si), tot)
        bias = (within + later + carry) * LOG2E
        carry = carry + jnp.sum(tot, axis=0, keepdims=True)

        for r in range(gp):
            s2 = _mm_nt(kbuf[slot, r], qrep)
            for a in range(HEADS):
                blk = s2[a * LANES:(a + 1) * LANES, :]
                sig = sig_of(r, a)
                sd_ref[sig:sig + 1, :] = jnp.sum(jnp.where(eye, blk, 0.0), axis=0, keepdims=True)
            advance_side(r)
        for sub in range(gp // DEC_SUB):
            pages = range(sub * DEC_SUB, (sub + 1) * DEC_SUB)
            lo = sig_of(pages[-1], 0)
            rows = slice(lo, lo + DEC_SUB * HEADS)
            tt = sd_ref[rows, :] + bias[rows, :]
            mx = jnp.max(tt, axis=0, keepdims=True)
            mx = _class_reduce(jnp.broadcast_to(mx, (SUBLANES, LANES)), jnp.maximum)[0:1, :]
            m_new = jnp.maximum(m_run, mx)
            pf = jnp.exp2(tt - m_new).astype(BF16).astype(F32)
            alpha = jnp.exp2(m_run - m_new)
            l_run = alpha * l_run + jnp.sum(pf, axis=0, keepdims=True)
            m_run = m_new
            acc = acc * _lane_to_col(alpha, eye8)
            for r in pages:
                parts = []
                for a in range(HEADS):
                    sig = sig_of(r, a) - lo
                    prow = jnp.broadcast_to(pf[sig:sig + 1, :], (LANES, LANES))
                    parts.append(jnp.where(eye, prow, 0.0).astype(BF16))
                pcol = jnp.dot(jnp.concatenate(parts, axis=0), ones, preferred_element_type=F32)
                pv = pcol * vbuf[slot, r]
                acc = acc + jnp.sum(pv.reshape(PAGE_ROWS // SUBLANES, SUBLANES, LANES), axis=0)
                advance_side(r)
    if side_stages is not None:
        _run(side_stages)

    acc4 = acc[0:HEADS, :] + acc[HEADS:2 * HEADS, :]
    mcol = _lane_to_col(m_run, eye8)[0:HEADS, :]
    lsum = _class_reduce(jnp.broadcast_to(l_run, (SUBLANES, LANES)), jnp.add)[0:1, :]
    lcol = _lane_to_col(lsum, eye8)[0:HEADS, :]
    q4 = q8_ref[0][0:HEADS, :].astype(F32)
    s_new = jnp.sum(q4 * kn_ref[0], axis=-1, keepdims=True)
    m_f = jnp.maximum(mcol, s_new)
    a1 = jnp.exp2(mcol - m_f)
    a2 = jnp.exp2(s_new - m_f)
    o = (acc4 * a1 + a2 * vn_ref[0]) / (lcol * a1 + a2)
    o_ref[0] = o * gate_ref[0].astype(F32)


def _mixer_kernel(pt_ref, ab_ref, qi_ref, kj_ref,
                  q8_ref, kn_ref, vn_ref, gate_ref, lfn_ref, ck_hbm, cv_hbm, clf_hbm,
                  aq_ref, ak_ref, av_ref, ack_ref, acq_ref, agate_ref,
                  gqkv_ref, ggz_ref, gcol_ref, grow_ref, ggn_ref,
                  o_ref, ao_ref, go_ref, gssm_ref,
                  kbuf, vbuf, lfbuf, sems, sd_ref, m_ref, l_ref, acc_ref, cqb_ref, gs_ref,
                  *, n_att, n_gdn, gdn_blocks):
    del ab_ref
    step = pl.program_id(0)
    n_seq = pt_ref.shape[0]
    hbm, bufs = (ck_hbm, cv_hbm, clf_hbm), (kbuf, vbuf, lfbuf)

    def gdn_stages():
        return _gdn_block_stages(lax.rem(step, gdn_blocks), gdn_blocks, gqkv_ref, ggz_ref, gcol_ref, grow_ref,
                                 ggn_ref, go_ref, gssm_ref, gs_ref)

    @pl.when(step < n_seq)
    def _():
        _dec_start(step, pt_ref, hbm, bufs, sems)

    @pl.when(step < n_att)
    def _():
        _att_step(qi_ref[step], kj_ref[step], aq_ref, ak_ref, av_ref, ack_ref, acq_ref, agate_ref, ao_ref,
                  m_ref, l_ref, acc_ref, cqb_ref)

    dec_args = (step, n_seq, pt_ref, q8_ref, kn_ref, vn_ref, gate_ref, lfn_ref, hbm, o_ref, bufs, sems, sd_ref)
    if n_gdn == n_seq:
        @pl.when(step < n_seq)
        def _():
            _dec_finish(*dec_args, side_stages=gdn_stages())
    else:
        @pl.when(step < n_seq)
        def _():
            _dec_finish(*dec_args, before_last_wait=lambda: pl.when(step < n_gdn)(lambda: _run(gdn_stages())))

        @pl.when((step >= n_seq) & (step < n_gdn))
        def _():
            _run(gdn_stages())


def _mixers(page_table, q8, k_new, v_new, gate, lf_new, cache_k2, cache_v2, cache_lf3,
            aq, ak, av, row, col, agate, gqkv, ggz, gdn_norm_g, batch, seq):
    nb, n_pages = page_table.shape
    gp = DEC_PAGES
    assert n_pages % (2 * gp) == 0, "an even number of page groups per sequence keeps the buffer slots static"
    n = seq // ATT_T
    steps = [(bb, i, j) for bb in range(batch) for i in range(n) for j in range(i + 1)]
    n_att = len(steps)
    gdn_blocks = seq // GDN_TB
    n_gdn = batch * gdn_blocks
    n_steps = max(nb, n_att, n_gdn)
    steps = steps + [steps[-1]] * (n_steps - n_att)
    ab_tab, qi_tab, kj_tab = (jnp.asarray([s[k] for s in steps], jnp.int32) for k in range(3))

    per_b = lambda s, *_: (jnp.minimum(s, nb - 1), 0, 0)
    qmap = lambda s, pt, ab, qi, kj: (ab[s] * n + qi[s], 0)
    kmap = lambda s, pt, ab, qi, kj: (ab[s] * n + kj[s], 0)
    gblk = lambda s: jnp.minimum(s, n_gdn - 1)
    grow = lambda s, *_: (gblk(s), 0)
    const2 = lambda s, *_: (0, 0)
    hbm = pl.BlockSpec(memory_space=pl.ANY)
    grid_spec = pltpu.PrefetchScalarGridSpec(
        num_scalar_prefetch=4,
        grid=(n_steps,),
        in_specs=[pl.BlockSpec((1, Q_ROWS, HEAD_DIM), per_b),
                  pl.BlockSpec((1, HEADS, HEAD_DIM), per_b),
                  pl.BlockSpec((1, HEADS, HEAD_DIM), per_b),
                  pl.BlockSpec((1, HEADS, HEAD_DIM), per_b),
                  pl.BlockSpec((1, 1, LANES), per_b),
                  hbm, hbm, hbm,
                  pl.BlockSpec((ATT_T, WIDTH), qmap),
                  pl.BlockSpec((ATT_T, WIDTH), kmap),
                  pl.BlockSpec((ATT_T, WIDTH), kmap),
                  pl.BlockSpec((1, 2 * SUBLANES, ATT_T), lambda s, pt, ab, qi, kj: (ab[s], 0, kj[s])),
                  pl.BlockSpec((ATT_T, LANES), qmap),
                  pl.BlockSpec((ATT_T, WIDTH), qmap),
                  pl.BlockSpec((GDN_TB, CONV_DIM), grow),
                  pl.BlockSpec((GDN_TB, WIDTH), grow),
                  pl.BlockSpec((GDN_TB, LANES), grow),
                  pl.BlockSpec((1, 2 * SUBLANES, GDN_TB),
                               lambda s, *_: (gblk(s) // gdn_blocks, 0, gblk(s) % gdn_blocks)),
                  pl.BlockSpec((1, HEAD_DIM), const2)],
        out_specs=[pl.BlockSpec((1, HEADS, HEAD_DIM), per_b),
                   pl.BlockSpec((ATT_T, WIDTH), qmap),
                   pl.BlockSpec((GDN_TB, WIDTH), grow),
                   pl.BlockSpec((1, HEADS, HEAD_DIM, HEAD_DIM), lambda s, *_: (gblk(s) // gdn_blocks, 0, 0, 0))],
        scratch_shapes=[pltpu.VMEM((2, gp, PAGE_ROWS, HEAD_DIM), F32),
                        pltpu.VMEM((2, gp, PAGE_ROWS, HEAD_DIM), F32),
                        pltpu.VMEM((2, gp * HEADS, LANES), F32),
                        pltpu.SemaphoreType.DMA((3, 2)),
                        pltpu.VMEM((gp * HEADS, LANES), F32)]
                       + [pltpu.VMEM((HEADS, ATT_T, LANES), F32)] * 4
                       + [pltpu.VMEM((HEADS, HEAD_DIM, HEAD_DIM), F32)])
    return pl.pallas_call(
        functools.partial(_mixer_kernel, n_att=n_att, n_gdn=n_gdn, gdn_blocks=gdn_blocks),
        grid_spec=grid_spec,
        out_shape=[jax.ShapeDtypeStruct((nb, HEADS, HEAD_DIM), F32),
                   jax.ShapeDtypeStruct((batch * seq, WIDTH), BF16),
                   jax.ShapeDtypeStruct((batch * seq, WIDTH), BF16),
                   jax.ShapeDtypeStruct((batch, HEADS, HEAD_DIM, HEAD_DIM), F32)],
        compiler_params=pltpu.CompilerParams(dimension_semantics=("arbitrary",),
                                             vmem_limit_bytes=DEC_VMEM_LIMIT),
        name="mixers",
    )(page_table, ab_tab, qi_tab, kj_tab, q8, k_new, v_new, gate, lf_new, cache_k2, cache_v2, cache_lf3,
      aq, ak, av, row, col, agate, gqkv, ggz, col, row, gdn_norm_g)


def _layer(x_prompt, x_sample, cache_k, cache_v, cache_logf, state_ssm, state_conv, page_table,
           c_prompt, c_sample, norm_g, w_ada, b_ada, w_in, b_f, conv_w, a_log, dt_bias,
           gdn_norm_g, q_norm_g, k_norm_g, w_out):
    batch, seq, _ = x_prompt.shape
    nb = x_sample.shape[0]
    w = WIDTH
    o_small = 4 * w
    o_fox = o_small + 2 * HEADS
    w_big = jnp.concatenate([w_in[:, 0:o_small], w_in[:, o_fox:o_fox + 4 * w]], axis=1).astype(BF16)
    w_small = jnp.concatenate([w_in[:, o_small:o_fox], w_in[:, o_fox + 4 * w:],
                               jnp.zeros((D_MODEL, LANES - 3 * HEADS), F32)], axis=1).astype(BF16)
    w_out_b = w_out.astype(BF16)
    par = jnp.zeros((SUBLANES, LANES), F32)
    par = par.at[0, LANE_G:LANE_G + HEADS].set(a_log)
    par = par.at[1, LANE_G:LANE_G + HEADS].set(dt_bias)
    par = par.at[2, LANE_F:LANE_F + HEADS].set(b_f)
    ng2, qg2, kg2, gg2 = (a.reshape(1, -1) for a in (norm_g, q_norm_g, k_norm_g, gdn_norm_g))

    cp8 = jnp.pad(c_prompt, ((0, SUBLANES - batch), (0, 0)))
    mod_p, mod_s = _ada_mod(cp8, c_sample, w_ada, b_ada.reshape(1, -1))
    mod_p4 = mod_p[:, 0:batch].reshape(3, batch, 1, D_MODEL)
    mod_s4 = mod_s.reshape(3, 1, nb, D_MODEL)

    xp = x_prompt.reshape(batch * seq, D_MODEL)
    qkv, gz, small, fq, fk, fkb, fv, fvb, fzg, conv_p = _inproj(xp, mod_p4, False, PROMPT_TM, ng2, w_big, w_small,
                                                                qg2, kg2, conv_w=conv_w)
    col, row = _small_prep(small, par, batch, seq)
    k_p = fk.reshape(batch, seq, HEADS, HEAD_DIM)
    v_p = fv.reshape(batch, seq, HEADS, HEAD_DIM)
    lf_p = col[:, LANE_LF:LANE_LF + HEADS].reshape(batch, seq, HEADS)

    xs = x_sample.reshape(nb, D_MODEL)
    qkv_s, gz_s, small_s, fq_s, fk_s, _, fv_s, _, fzg_s = _inproj(xs, mod_s4, True, nb, ng2, w_big, w_small, qg2, kg2)
    qn_s, kn_s, vn_s, sc_s = _sample_mid(qkv_s, state_conv[:, 0], state_conv[:, 1], state_conv[:, 2],
                                         conv_w, small_s, par)
    h3 = lambda a: a.reshape(nb, HEADS, HEAD_DIM)
    bcast = lambda a: jnp.broadcast_to(a[:, :, None], (nb, HEADS, HEAD_DIM))
    t8 = jnp.stack([h3(kn_s), h3(qn_s), h3(vn_s), bcast(sc_s[:, LANE_B:LANE_B + HEADS]),
                    bcast(sc_s[:, LANE_G:LANE_G + HEADS])]
                   + [jnp.zeros((nb, HEADS, HEAD_DIM), F32)] * (SUBLANES - 5), axis=2)
    o_a_s, ssm_s = _gdn_step(t8, state_ssm, h3(gz_s), gg2)
    lf_s = sc_s[:, LANE_F:LANE_F + HEADS]
    q8 = jnp.tile(h3(fq_s), (1, Q_ROWS // HEADS, 1))
    lf_new = jnp.tile(lf_s, (1, LANES // HEADS)).reshape(nb, 1, LANES)
    n_pool = cache_k.shape[0]
    o_b_s, o_b, o_a, ssm_p = _mixers(
        page_table, q8, h3(fk_s), h3(fv_s), h3(fzg_s), lf_new,
        cache_k.reshape(n_pool * PAGE_ROWS, HEAD_DIM), cache_v.reshape(n_pool * PAGE_ROWS, HEAD_DIM),
        cache_logf.reshape(n_pool, HEADS, LANES),
        fq, fkb, fvb, row, col, fzg, qkv, gz, gg2, batch, seq)
    y_p = _outproj(xp, o_a, o_b, mod_p4, False, OUT_TM, w_out_b).reshape(batch, seq, D_MODEL)
    y_s = _outproj(xs, o_a_s.reshape(nb, w), o_b_s.reshape(nb, w), mod_s4, True, nb, w_out_b).reshape(nb, 1, D_MODEL)
    conv_s = jnp.stack([state_conv[:, 1], state_conv[:, 2], qkv_s], axis=1)
    return (y_p, y_s, k_p, v_p, lf_p, ssm_p, conv_p,
            h3(fk_s).reshape(nb, 1, HEADS, HEAD_DIM), h3(fv_s).reshape(nb, 1, HEADS, HEAD_DIM),
            lf_s.reshape(nb, 1, HEADS), ssm_s, conv_s)


def kernel(x_prompt, x_sample, cache_k, cache_v, cache_logf, state_ssm, state_conv, page_table, c_prompt, c_sample, norm_g, w_ada, b_ada, w_in, b_f, conv_w, a_log, dt_bias, gdn_norm_g, q_norm_g, k_norm_g, w_out):
    assert w_ada.shape[0] == 1, "single layer"
    outs = _layer(x_prompt, x_sample, cache_k[0], cache_v[0], cache_logf[0], state_ssm[0], state_conv[0],
                  page_table, c_prompt, c_sample, norm_g[0], w_ada[0], b_ada[0], w_in[0], b_f[0], conv_w[0],
                  a_log[0], dt_bias[0], gdn_norm_g[0], q_norm_g[0], k_norm_g[0], w_out[0])
    y_p, y_s = outs[0], outs[1]
    return (y_p, y_s) + tuple(o[None] for o in outs[2:])
```

```python
import functools

import jax
import jax.numpy as jnp
from jax import lax
from jax.experimental import pallas as pl
from jax.experimental.pallas import tpu as pltpu

F32 = jnp.float32
BF16 = jnp.bfloat16

D_MODEL = 1024
HEADS = 4
HEAD_DIM = 128
WIDTH = HEADS * HEAD_DIM
CONV_W = 4
CONV_DIM = 3 * WIDTH
CHUNK = 64
PAGE_SIZE = 128
PAGE_ROWS = PAGE_SIZE * HEADS
NORM_EPS = 1e-6
LANES = 128
SUBLANES = 8
NEG_BIG = -1e30
LOG2E = 1.4426950408889634
ATT_Q_SCALE = LOG2E * HEAD_DIM ** -0.5
VMEM_LIMIT = 48 * 1024 * 1024

COL_GQKV = 0
COL_GZ = 3 * WIDTH
COL_FQ = 4 * WIDTH
COL_FK = 5 * WIDTH
COL_FV = 6 * WIDTH
COL_FZ = 7 * WIDTH
BIG_COLS = 8 * WIDTH
LANE_B, LANE_G, LANE_F, LANE_LF = 0, 4, 8, 12

PROMPT_TM = 256
OUT_TM = 512
SMALL_TB = 512
GDN_TB = 128
ATT_T = 512
DEC_PAGES = 32
DEC_SUB = 16
DEC_SIDE_EVERY = 8
DEC_VMEM_LIMIT = 56 * 1024 * 1024
Q_ROWS = 16
STEP_BATCH = 4


def _mm(a, b):
    return jnp.dot(a.astype(BF16), b.astype(BF16), preferred_element_type=F32)


def _mm_nt(a, b):
    return lax.dot_general(a.astype(BF16), b.astype(BF16), (((1,), (1,)), ((), ())),
                           preferred_element_type=F32)


def _mm_tn(a, b):
    return lax.dot_general(a.astype(BF16), b.astype(BF16), (((0,), (0,)), ((), ())),
                           preferred_element_type=F32)


def _mm_exact_rhs(mat_b, y):
    hi = y.astype(BF16)
    r1 = y - hi.astype(F32)
    mid = r1.astype(BF16)
    lo = (r1 - mid.astype(F32)).astype(BF16)
    dot = functools.partial(jnp.dot, preferred_element_type=F32)
    return dot(mat_b, hi) + dot(mat_b, mid) + dot(mat_b, lo)


def _mm_exact_lhs(y, mat_b):
    hi = y.astype(BF16)
    r1 = y - hi.astype(F32)
    mid = r1.astype(BF16)
    lo = (r1 - mid.astype(F32)).astype(BF16)
    dot = functools.partial(jnp.dot, preferred_element_type=F32)
    return dot(hi, mat_b) + dot(mid, mat_b) + dot(lo, mat_b)


def _sigmoid(x):
    return 1.0 / (1.0 + jnp.exp(-x))


def _silu(x):
    return x * _sigmoid(x)


def _softplus(x):
    return jnp.maximum(x, 0.0) + jnp.log(1.0 + jnp.exp(-jnp.abs(x)))


def _ones_mask(mask):
    return jnp.where(mask, 1.0, 0.0).astype(BF16)


def _params(n_axes):
    return pltpu.CompilerParams(dimension_semantics=("arbitrary",) * n_axes,
                                vmem_limit_bytes=VMEM_LIMIT)


def _ada_kernel(cp_ref, cs_ref, w_ref, b_ref, op_ref, os_ref):
    w = w_ref[...].astype(BF16)
    b = b_ref[...]
    op_ref[0] = jnp.dot(_silu(cp_ref[...]).astype(BF16), w, preferred_element_type=F32) + b
    os_ref[0] = jnp.dot(_silu(cs_ref[...]).astype(BF16), w, preferred_element_type=F32) + b


def _ada_mod(c_prompt8, c_sample, w_ada, b_ada):
    nb = c_sample.shape[0]
    return pl.pallas_call(
        _ada_kernel,
        grid=(3,),
        in_specs=[pl.BlockSpec((SUBLANES, D_MODEL), lambda k: (0, 0)),
                  pl.BlockSpec((nb, D_MODEL), lambda k: (0, 0)),
                  pl.BlockSpec((D_MODEL, D_MODEL), lambda k: (0, k)),
                  pl.BlockSpec((1, D_MODEL), lambda k: (0, k))],
        out_specs=[pl.BlockSpec((1, SUBLANES, D_MODEL), lambda k: (k, 0, 0)),
                   pl.BlockSpec((1, nb, D_MODEL), lambda k: (k, 0, 0))],
        out_shape=[jax.ShapeDtypeStruct((3, SUBLANES, D_MODEL), F32),
                   jax.ShapeDtypeStruct((3, nb, D_MODEL), F32)],
        compiler_params=_params(1),
        name="ada_mod",
    )(c_prompt8, c_sample, w_ada, b_ada)


def _inproj_kernel(x_ref, sh_ref, sc_ref, ng_ref, wb_ref, ws_ref, qg_ref, kg_ref,
                   qkv_ref, gz_ref, sm_ref, fq_ref, fk_ref, fkb_ref, fv_ref, fvb_ref, fzg_ref):
    x = x_ref[...]
    y = x * lax.rsqrt(jnp.mean(x * x, axis=-1, keepdims=True) + NORM_EPS)
    hdn = (y * ng_ref[...]) * (1.0 + sc_ref[0, 0]) + sh_ref[0, 0]
    hb = hdn.astype(BF16)
    dot = lambda a, wt: lax.dot_general(a, wt, (((1,), (1,)), ((), ())), preferred_element_type=F32)
    qkv_ref[...] = dot(hb, wb_ref[COL_GQKV:COL_GZ, :])
    gz_ref[...] = dot(hb, wb_ref[COL_GZ:COL_FQ, :])
    sm_ref[...] = dot(hb, ws_ref[...])
    fq = dot(hb, wb_ref[COL_FQ:COL_FK, :])
    fk = dot(hb, wb_ref[COL_FK:COL_FV, :])
    fv = dot(hb, wb_ref[COL_FV:COL_FZ, :])
    fz = dot(hb, wb_ref[COL_FZ:BIG_COLS, :])
    fvb_ref[...] = fv.astype(BF16)
    fzg_ref[...] = _silu(fz).astype(BF16)
    scale = ATT_Q_SCALE
    tm = x.shape[0]
    for h in range(HEADS):
        sl = slice(h * HEAD_DIM, (h + 1) * HEAD_DIM)
        qh = fq[:, sl]
        qn = (qh * lax.rsqrt(jnp.mean(qh * qh, axis=-1, keepdims=True) + NORM_EPS)) * qg_ref[...]
        fq_ref[:, sl] = (qn * scale).astype(BF16)
        kh = fk[:, sl]
        kn = (kh * lax.rsqrt(jnp.mean(kh * kh, axis=-1, keepdims=True) + NORM_EPS)) * kg_ref[...]
        fkb_ref[:, sl] = kn.astype(BF16)
        fk_ref[pl.ds(h, tm, stride=HEADS), :] = kn
        fv_ref[pl.ds(h, tm, stride=HEADS), :] = fv[:, sl]


def _inproj(x2d, mod4, per_row_mod, tm, norm_g, w_big, w_small, q_norm_g, k_norm_g):
    rows = x2d.shape[0]
    nblk = rows // tm
    if per_row_mod:
        mod_block = (1, 1, tm, D_MODEL)
        sh_map = lambda i: (0, 0, i, 0)
        sc_map = lambda i: (1, 0, i, 0)
    else:
        blocks_per_batch = nblk // mod4.shape[1]
        mod_block = (1, 1, 1, D_MODEL)
        sh_map = lambda i: (0, i // blocks_per_batch, 0, 0)
        sc_map = lambda i: (1, i // blocks_per_batch, 0, 0)
    row_spec = lambda n: pl.BlockSpec((tm, n), lambda i: (i, 0))
    const = lambda shape: pl.BlockSpec(shape, lambda i: (0,) * len(shape))
    outs = [(CONV_DIM, F32, 1), (WIDTH, F32, 1), (LANES, F32, 1), (WIDTH, BF16, 1), (HEAD_DIM, F32, HEADS),
            (WIDTH, BF16, 1), (HEAD_DIM, F32, HEADS), (WIDTH, BF16, 1), (WIDTH, BF16, 1)]
    return pl.pallas_call(
        _inproj_kernel,
        grid=(nblk,),
        in_specs=[row_spec(D_MODEL),
                  pl.BlockSpec(mod_block, sh_map), pl.BlockSpec(mod_block, sc_map),
                  const((1, D_MODEL)), const((BIG_COLS, D_MODEL)), const((LANES, D_MODEL)),
                  const((1, HEAD_DIM)), const((1, HEAD_DIM))],
        out_specs=[pl.BlockSpec((tm * r, n), lambda i: (i, 0)) for n, _, r in outs],
        out_shape=[jax.ShapeDtypeStruct((rows * r, n), dt) for n, dt, r in outs],
        compiler_params=_params(1),
        name="in_proj",
    )(x2d, mod4, mod4, norm_g, w_big, w_small, q_norm_g, k_norm_g)


def _token_scalars(x, par):
    beta = _sigmoid(x)
    g = -jnp.exp(par[0:1, :]) * _softplus(x + par[1:2, :])
    lf = -_softplus(-(x + par[2:3, :]))
    return beta, g, lf


def _small_kernel(sm_ref, par_ref, col_ref, row_ref, carry_ref):
    tb = SMALL_TB

    @pl.when(pl.program_id(1) == 0)
    def _():
        carry_ref[...] = jnp.zeros_like(carry_ref)

    x = sm_ref[...]
    beta, g, lf = _token_scalars(x, par_ref[...])
    lane = lax.broadcasted_iota(jnp.int32, (tb, LANES), 1)
    is_b = lane < LANE_G
    is_g = (lane >= LANE_G) & (lane < LANE_F)
    is_f = (lane >= LANE_F) & (lane < LANE_LF)
    gm = jnp.where(is_g, g, 0.0)
    fm = jnp.where(is_f, lf, 0.0)
    r = lax.broadcasted_iota(jnp.int32, (tb, tb), 0)
    c = lax.broadcasted_iota(jnp.int32, (tb, tb), 1)
    tri = c <= r
    same_chunk = jnp.right_shift(r, 6) == jnp.right_shift(c, 6)
    gcum = _mm_exact_rhs(_ones_mask(tri & same_chunk), gm)
    cum = _mm_exact_rhs(_ones_mask(tri), fm) + carry_ref[...]
    cum = jnp.where(is_f, cum, 0.0)
    carry_ref[...] = cum[tb - 1:tb, :]
    col = jnp.where(is_b, beta, 0.0) + gcum + cum + pltpu.roll(fm, LANE_LF - LANE_F, axis=1)
    col_ref[...] = col
    for i in range(tb // LANES):
        t = col[i * LANES:(i + 1) * LANES, :]
        row_ref[0, :, i * LANES:(i + 1) * LANES] = t.T[0:2 * SUBLANES, :]


def _small_prep(small, par, batch, seq):
    nt = seq // SMALL_TB
    return pl.pallas_call(
        _small_kernel,
        grid=(batch, nt),
        in_specs=[pl.BlockSpec((SMALL_TB, LANES), lambda b, t: (b * nt + t, 0)),
                  pl.BlockSpec((SUBLANES, LANES), lambda b, t: (0, 0))],
        out_specs=[pl.BlockSpec((SMALL_TB, LANES), lambda b, t: (b * nt + t, 0)),
                   pl.BlockSpec((1, 2 * SUBLANES, SMALL_TB), lambda b, t: (b, 0, t))],
        out_shape=[jax.ShapeDtypeStruct((batch * seq, LANES), F32),
                   jax.ShapeDtypeStruct((batch, 2 * SUBLANES, seq), F32)],
        scratch_shapes=[pltpu.VMEM((1, LANES), F32)],
        compiler_params=_params(2),
        name="token_scalars",
    )(small, par)


_GDN_TAIL = "tail"


def _run(stages):
    try:
        while True:
            next(stages)
    except StopIteration as done:
        return done.value


def _unit_lower_inverses(ms, eye, ri, ci):
    base = 3
    same = jnp.right_shift(ri, base) == jnp.right_shift(ci, base)
    xps = [-jnp.where(same, m, 0.0) for m in ms]
    ps = [eye + xp for xp in xps]
    for _ in range(base - 1):
        xps = [_mm(xp, xp) for xp in xps]
        yield
        ps = [p + _mm(p, xp) for p, xp in zip(ps, xps)]
        yield
    size = base
    while (1 << size) < CHUNK:
        off = ((jnp.right_shift(ri, size + 1) == jnp.right_shift(ci, size + 1))
               & (jnp.right_shift(ri, size) != jnp.right_shift(ci, size)))
        pcs = [_mm(p, jnp.where(off, m, 0.0)) for p, m in zip(ps, ms)]
        yield
        ps = [p - _mm(pc, p) for p, pc in zip(ps, pcs)]
        yield
        size += 1
    return ps


def _gdn_chunk_preps(qs, ks, vs, betas, gcs, grs, incl, strict, eye, ri, ci):
    decays = [jnp.where(incl, jnp.exp(jnp.where(incl, gc - gr, 0.0)), 0.0) for gc, gr in zip(gcs, grs)]
    kbs = [k * beta for k, beta in zip(ks, betas)]
    boths = [_mm_nt(jnp.concatenate([kb, q], axis=0), k) for kb, q, k in zip(kbs, qs, ks)]
    yield
    ms = [jnp.where(strict, both[0:CHUNK] * decay, 0.0) for both, decay in zip(boths, decays)]
    a_intras = [both[CHUNK:2 * CHUNK] * decay for both, decay in zip(boths, decays)]
    tinvs = yield from _unit_lower_inverses(ms, eye, ri, ci)
    egs = [jnp.exp(gc) for gc in gcs]
    uws = [_mm(tinv, jnp.concatenate([v * beta, kb * eg], axis=1))
           for tinv, v, beta, kb, eg in zip(tinvs, vs, betas, kbs, egs)]
    yield
    preps = []
    for uw, q, k, eg, gc, a_intra in zip(uws, qs, ks, egs, gcs, a_intras):
        g_last = gc[CHUNK - 1:CHUNK, :]
        wq = jnp.concatenate([uw[:, HEAD_DIM:2 * HEAD_DIM], q * eg], axis=0)
        preps.append((uw[:, 0:HEAD_DIM], wq, a_intra, k * jnp.exp(g_last - gc), jnp.exp(g_last)))
    return preps


def _gdn_chunk_steps(preps, states):
    boths = [_mm(p[1], s) for p, s in zip(preps, states)]
    yield
    v_news = [p[0] - both[0:CHUNK] for p, both in zip(preps, boths)]
    outs = [both[CHUNK:2 * CHUNK] + _mm(p[2], v_new) for p, both, v_new in zip(preps, boths, v_news)]
    new_states = [s * p[4] + _mm_tn(p[3], v_new) for p, s, v_new in zip(preps, states, v_news)]
    yield
    return outs, new_states


def _gdn_block_stages(t, n_t, qkv_ref, gz_ref, col_ref, row_ref, cw_ref, gn_ref,
                      o_ref, ssm_ref, conv_ref, s_ref, xbuf_ref, qn_ref, kn_ref, vn_ref):
    tb = GDN_TB
    halo = SUBLANES

    @pl.when(t == 0)
    def _():
        s_ref[...] = jnp.zeros_like(s_ref)
        xbuf_ref[0:halo, :] = jnp.zeros((halo, CONV_DIM), F32)

    xbuf_ref[halo:halo + tb, :] = qkv_ref[...]
    scale = HEAD_DIM ** -0.5
    for cb in range(CONV_DIM // LANES):
        sl = slice(cb * LANES, (cb + 1) * LANES)
        acc = xbuf_ref[halo - 3:halo - 3 + tb, sl] * cw_ref[0:1, sl]
        for i in range(1, CONV_W):
            acc = acc + xbuf_ref[halo - 3 + i:halo - 3 + i + tb, sl] * cw_ref[i:i + 1, sl]
        act = _silu(acc)
        part, h = divmod(cb, HEADS)
        hs = slice(h * LANES, (h + 1) * LANES)
        if part == 0:
            qn_ref[:, hs] = act * lax.rsqrt(jnp.sum(act * act, axis=-1, keepdims=True) + NORM_EPS) * scale
        elif part == 1:
            kn_ref[:, hs] = act * lax.rsqrt(jnp.sum(act * act, axis=-1, keepdims=True) + NORM_EPS)
        else:
            vn_ref[:, hs] = act
    tail = xbuf_ref[tb:tb + halo, :]
    xbuf_ref[0:halo, :] = tail
    yield

    ri = lax.broadcasted_iota(jnp.int32, (CHUNK, CHUNK), 0)
    ci = lax.broadcasted_iota(jnp.int32, (CHUNK, CHUNK), 1)
    incl = ci <= ri
    strict = ci < ri
    eye = jnp.where(ci == ri, 1.0, 0.0)

    colt = col_ref[...]
    rowt = row_ref[0]
    n_cc = tb // CHUNK
    items = [(cc, h) for cc in range(n_cc) for h in range(HEADS)]
    rs = lambda cc: slice(cc * CHUNK, (cc + 1) * CHUNK)
    hs = lambda h: slice(h * HEAD_DIM, (h + 1) * HEAD_DIM)
    preps = yield from _gdn_chunk_preps(
        [qn_ref[rs(cc), hs(h)] for cc, h in items],
        [kn_ref[rs(cc), hs(h)] for cc, h in items],
        [vn_ref[rs(cc), hs(h)] for cc, h in items],
        [colt[rs(cc), LANE_B + h:LANE_B + h + 1] for cc, h in items],
        [colt[rs(cc), LANE_G + h:LANE_G + h + 1] for cc, h in items],
        [rowt[LANE_G + h:LANE_G + h + 1, rs(cc)] for cc, h in items],
        incl, strict, eye, ri, ci)
    states = [s_ref[h] for h in range(HEADS)]
    for cc in range(n_cc):
        outs, states = yield from _gdn_chunk_steps(preps[cc * HEADS:(cc + 1) * HEADS], states)
        for h, o in enumerate(outs):
            on = (o * lax.rsqrt(jnp.mean(o * o, axis=-1, keepdims=True) + NORM_EPS)) * gn_ref[...]
            o_ref[rs(cc), hs(h)] = (on * _silu(gz_ref[rs(cc), hs(h)])).astype(o_ref.dtype)
    for h in range(HEADS):
        s_ref[h] = states[h]
    yield _GDN_TAIL

    @pl.when(t == n_t - 1)
    def _():
        ssm_ref[0] = s_ref[...]
        conv_ref[0] = tail[halo - (CONV_W - 1):halo, :]


def _att_step(qi, kj, q_ref, k_ref, v_ref, ck_ref, cq_ref, gate_ref, o_ref, m_ref, l_ref, acc_ref, cqb_ref):
    tq = tk = ATT_T
    heads = range(HEADS)
    hs = lambda h: slice(h * HEAD_DIM, (h + 1) * HEAD_DIM)

    @pl.when(kj == 0)
    def _():
        m_ref[...] = jnp.full_like(m_ref, NEG_BIG)
        l_ref[...] = jnp.zeros_like(l_ref)
        acc_ref[...] = jnp.zeros_like(acc_ref)
        for h in heads:
            cqb_ref[h] = jnp.broadcast_to(cq_ref[:, LANE_F + h:LANE_F + h + 1] * LOG2E, (tq, LANES))

    def update(diagonal):
        ss = [lax.dot_general(q_ref[:, hs(h)], k_ref[:, hs(h)], (((1,), (1,)), ((), ())),
                              preferred_element_type=F32) for h in heads]
        tts = [s - ck_ref[0, LANE_F + h:LANE_F + h + 1, :] * LOG2E for h, s in zip(heads, ss)]
        if diagonal:
            visible = (lax.broadcasted_iota(jnp.int32, (tq, tk), 1)
                       <= lax.broadcasted_iota(jnp.int32, (tq, tk), 0))
            tts = [jnp.where(visible, tt, NEG_BIG) for tt in tts]
        cqs = [cqb_ref[h] for h in heads]
        m_olds = [m_ref[h] for h in heads]
        m_news = [jnp.maximum(m_old, jnp.broadcast_to(jnp.max(tt, axis=1, keepdims=True), (tq, LANES)) + cq)
                  for m_old, tt, cq in zip(m_olds, tts, cqs)]
        shifts = [jnp.concatenate([m_new - cq] * (tk // LANES), axis=1) for m_new, cq in zip(m_news, cqs)]
        ps = [jnp.exp2(tt - shift).astype(BF16) for tt, shift in zip(tts, shifts)]
        ones = jnp.ones((tk, HEAD_DIM), BF16)
        pvs = [jnp.dot(p, jnp.concatenate([v_ref[:, hs(h)], ones], axis=1), preferred_element_type=F32)
               for h, p in zip(heads, ps)]
        alphas = [jnp.exp2(m_old - m_new) for m_old, m_new in zip(m_olds, m_news)]
        accs = [alpha * acc_ref[h] + pv[:, 0:HEAD_DIM] for h, alpha, pv in zip(heads, alphas, pvs)]
        ls = [alpha * l_ref[h] + pv[:, HEAD_DIM:2 * HEAD_DIM] for h, alpha, pv in zip(heads, alphas, pvs)]
        return m_news, ls, accs

    @pl.when(kj < qi)
    def _():
        m_news, ls, accs = update(False)
        for h in heads:
            m_ref[h] = m_news[h]
            l_ref[h] = ls[h]
            acc_ref[h] = accs[h]

    @pl.when(kj == qi)
    def _():
        _, ls, accs = update(True)
        for h in heads:
            o_ref[:, hs(h)] = ((accs[h] / ls[h]) * gate_ref[:, hs(h)].astype(F32)).astype(o_ref.dtype)


def _outproj_kernel(x_ref, oa_ref, ob_ref, gate_ref, w_ref, y_ref):
    mix = (jnp.dot(oa_ref[...].astype(BF16), w_ref[0:WIDTH, :], preferred_element_type=F32)
           + jnp.dot(ob_ref[...].astype(BF16), w_ref[WIDTH:2 * WIDTH, :], preferred_element_type=F32))
    y_ref[...] = x_ref[...] + gate_ref[0, 0] * mix


def _outproj(x2d, oa, ob, mod4, per_row_mod, tm, w_out_b):
    rows = x2d.shape[0]
    nblk = rows // tm
    if per_row_mod:
        mod_block = (1, 1, tm, D_MODEL)
        g_map = lambda i: (2, 0, i, 0)
    else:
        blocks_per_batch = nblk // mod4.shape[1]
        mod_block = (1, 1, 1, D_MODEL)
        g_map = lambda i: (2, i // blocks_per_batch, 0, 0)
    row_spec = lambda n: pl.BlockSpec((tm, n), lambda i: (i, 0))
    return pl.pallas_call(
        _outproj_kernel,
        grid=(nblk,),
        in_specs=[row_spec(D_MODEL), row_spec(WIDTH), row_spec(WIDTH),
                  pl.BlockSpec(mod_block, g_map),
                  pl.BlockSpec((2 * WIDTH, D_MODEL), lambda i: (0, 0))],
        out_specs=row_spec(D_MODEL),
        out_shape=jax.ShapeDtypeStruct((rows, D_MODEL), F32),
        compiler_params=_params(1),
        name="out_proj",
    )(x2d, oa, ob, mod4, w_out_b)


def _smid_kernel(u_ref, s0_ref, s1_ref, s2_ref, cw_ref, sm_ref, par_ref,
                 qn_ref, kn_ref, vn_ref, sc_ref):
    conv = (s0_ref[...] * cw_ref[0:1, :] + s1_ref[...] * cw_ref[1:2, :]
            + s2_ref[...] * cw_ref[2:3, :] + u_ref[...] * cw_ref[3:4, :])
    act = _silu(conv)
    for h in range(HEADS):
        hs = slice(h * HEAD_DIM, (h + 1) * HEAD_DIM)
        q = act[:, h * HEAD_DIM:(h + 1) * HEAD_DIM]
        k = act[:, WIDTH + h * HEAD_DIM:WIDTH + (h + 1) * HEAD_DIM]
        qn_ref[:, hs] = q * lax.rsqrt(jnp.sum(q * q, axis=-1, keepdims=True) + NORM_EPS)
        kn_ref[:, hs] = k * lax.rsqrt(jnp.sum(k * k, axis=-1, keepdims=True) + NORM_EPS)
    vn_ref[...] = act[:, 2 * WIDTH:3 * WIDTH]
    beta, g, lf = _token_scalars(sm_ref[...], par_ref[...])
    lane = lax.broadcasted_iota(jnp.int32, sm_ref.shape, 1)
    sc_ref[...] = jnp.where(lane < LANE_G, beta, jnp.where(lane < LANE_F, g, lf))


def _sample_mid(u, s0, s1, s2, conv_w, small, par):
    nb = u.shape[0]
    full = lambda a: pl.BlockSpec(a.shape, lambda: (0,) * a.ndim)
    args = (u, s0, s1, s2, conv_w, small, par)
    return pl.pallas_call(
        _smid_kernel,
        in_specs=[full(a) for a in args],
        out_specs=[pl.BlockSpec((nb, WIDTH), lambda: (0, 0))] * 3 + [pl.BlockSpec((nb, LANES), lambda: (0, 0))],
        out_shape=[jax.ShapeDtypeStruct((nb, WIDTH), F32)] * 3 + [jax.ShapeDtypeStruct((nb, LANES), F32)],
        compiler_params=pltpu.CompilerParams(vmem_limit_bytes=VMEM_LIMIT),
        name="sample_conv",
    )(*args)


def _gdn_step_kernel(t_ref, s_ref, gz_ref, gn_ref, o_ref, sn_ref):
    scale = HEAD_DIM ** -0.5
    row = lax.broadcasted_iota(jnp.int32, (SUBLANES, HEAD_DIM), 0)
    items = [(b, h) for b in range(STEP_BATCH) for h in range(HEADS)]
    ts = [t_ref[b, h] for b, h in items]
    rs = [_mm(t, s_ref[b, h]) for t, (b, h) in zip(ts, items)]
    egs, v_news = [], []
    for t, r, (b, h) in zip(ts, rs, items):
        k, q, v, beta, g = (t[i:i + 1, :] for i in range(5))
        eg = jnp.exp(g)
        v_new = beta * (v - eg * r[0:1, :])
        o = scale * (eg * r[1:2, :] + jnp.sum(q * k, axis=-1, keepdims=True) * v_new)
        on = (o * lax.rsqrt(jnp.mean(o * o, axis=-1, keepdims=True) + NORM_EPS)) * gn_ref[...]
        o_ref[b, h:h + 1, :] = on * _silu(gz_ref[b, h:h + 1, :])
        egs.append(eg)
        v_news.append(v_new)
    outers = [_mm_tn(jnp.where(row == 0, t, 0.0),
                     jnp.where(row == 0, jnp.broadcast_to(v_new, (SUBLANES, HEAD_DIM)), 0.0))
              for t, v_new in zip(ts, v_news)]
    for eg, outer, (b, h) in zip(egs, outers, items):
        sn_ref[b, h] = s_ref[b, h] * eg + outer


def _gdn_step(t8, state, gz3, gdn_norm_g):
    nb = state.shape[0]
    sb = STEP_BATCH
    return pl.pallas_call(
        _gdn_step_kernel,
        grid=(nb // sb,),
        in_specs=[pl.BlockSpec((sb, HEADS, SUBLANES, HEAD_DIM), lambda b: (b, 0, 0, 0)),
                  pl.BlockSpec((sb, HEADS, HEAD_DIM, HEAD_DIM), lambda b: (b, 0, 0, 0)),
                  pl.BlockSpec((sb, HEADS, HEAD_DIM), lambda b: (b, 0, 0)),
                  pl.BlockSpec((1, HEAD_DIM), lambda b: (0, 0))],
        out_specs=[pl.BlockSpec((sb, HEADS, HEAD_DIM), lambda b: (b, 0, 0)),
                   pl.BlockSpec((sb, HEADS, HEAD_DIM, HEAD_DIM), lambda b: (b, 0, 0, 0))],
        out_shape=[jax.ShapeDtypeStruct((nb, HEADS, HEAD_DIM), F32),
                   jax.ShapeDtypeStruct((nb, HEADS, HEAD_DIM, HEAD_DIM), F32)],
        compiler_params=_params(1),
        name="gdn_step",
    )(t8, state, gz3, gdn_norm_g)


def _class_reduce(x, op):
    shift = HEADS
    while shift < LANES:
        x = op(x, pltpu.roll(x, shift, axis=1))
        shift *= 2
    return x


def _lane_to_col(v, eye8):
    return jnp.sum(jnp.where(eye8, jnp.broadcast_to(v, (SUBLANES, LANES)), 0.0), axis=1, keepdims=True)


def _dec_sig(r, a):
    return (DEC_PAGES - 1 - r) * HEADS + a


def _dec_group_copies(pt_ref, hbm, bufs, sems, seq, g, slot):
    (ck_hbm, cv_hbm, clf_hbm), (kbuf, vbuf, lfbuf) = hbm, bufs
    n_pages = pt_ref.shape[1]
    out = []
    for r in range(DEC_PAGES):
        page = pt_ref[seq, n_pages - 1 - (g * DEC_PAGES + r)]
        row0 = pl.multiple_of(page * PAGE_ROWS, PAGE_ROWS)
        out.append(pltpu.make_async_copy(ck_hbm.at[pl.ds(row0, PAGE_ROWS), :], kbuf.at[slot, r], sems.at[0, slot]))
        out.append(pltpu.make_async_copy(cv_hbm.at[pl.ds(row0, PAGE_ROWS), :], vbuf.at[slot, r], sems.at[1, slot]))
        out.append(pltpu.make_async_copy(clf_hbm.at[page], lfbuf.at[slot, pl.ds(_dec_sig(r, 0), HEADS), :],
                                         sems.at[2, slot]))
    return out


def _dec_start(b, pt_ref, hbm, bufs, sems):
    @pl.when(b == 0)
    def _():
        for c in _dec_group_copies(pt_ref, hbm, bufs, sems, 0, 0, 0):
            c.start()

    for c in _dec_group_copies(pt_ref, hbm, bufs, sems, b, 1, 1):
        c.start()


def _dec_finish(b, nb, pt_ref, q8_ref, kn_ref, vn_ref, gate_ref, lfn_ref, hbm, o_ref, bufs, sems, sd_ref,
                before_last_wait=None, side_stages=None):
    kbuf, vbuf, lfbuf = bufs
    side = {"live": side_stages is not None}

    def advance_side(r):
        if side["live"] and r % DEC_SIDE_EVERY == DEC_SIDE_EVERY - 1:
            if next(side_stages, None) is _GDN_TAIL:
                side["live"] = False

    if side["live"]:
        next(side_stages)
    n_pages = pt_ref.shape[1]
    gp = DEC_PAGES
    ng = n_pages // gp
    nrow = gp * HEADS
    sig_of = _dec_sig
    group_copies = functools.partial(_dec_group_copies, pt_ref, hbm, bufs, sems)

    ri = lax.broadcasted_iota(jnp.int32, (LANES, LANES), 0)
    ci = lax.broadcasted_iota(jnp.int32, (LANES, LANES), 1)
    eye = ri == ci
    same_head = (ri & (HEADS - 1)) == (ci & (HEADS - 1))
    eye8 = (lax.broadcasted_iota(jnp.int32, (SUBLANES, LANES), 0)
            == lax.broadcasted_iota(jnp.int32, (SUBLANES, LANES), 1))
    si = lax.broadcasted_iota(jnp.int32, (nrow, nrow), 0)
    sj = lax.broadcasted_iota(jnp.int32, (nrow, nrow), 1)
    ones = jnp.ones((LANES, LANES), BF16)
    qrep = jnp.concatenate([q8_ref[0]] * (LANES // Q_ROWS), axis=0)

    m_run = jnp.full((1, LANES), NEG_BIG, F32)
    l_run = jnp.zeros((1, LANES), F32)
    acc = jnp.zeros((SUBLANES, HEAD_DIM), F32)
    carry = lfn_ref[0]
    for g in range(ng):
        slot = g % 2
        if 0 < g < ng - 1:
            for c in group_copies(b, g + 1, 1 - slot):
                c.start()
        elif g == ng - 1:
            @pl.when(b + 1 < nb)
            def _():
                for c in group_copies(b + 1, 0, 1 - slot):
                    c.start()
            if before_last_wait is not None:
                before_last_wait()
        for c in group_copies(b, g, slot):
            c.wait()

        lf = lfbuf[slot]
        within = _mm_exact_lhs(lf, _ones_mask(same_head & (ri > ci)))
        tot = _mm_exact_lhs(lf, _ones_mask(same_head))
        later = _mm_exact_rhs(_ones_mask(sj > si), tot)
        bias = (within + later + carry) * LOG2E
        carry = carry + jnp.sum(tot, axis=0, keepdims=True)

        for r in range(gp):
            s2 = _mm_nt(kbuf[slot, r], qrep)
            for a in range(HEADS):
                blk = s2[a * LANES:(a + 1) * LANES, :]
                sig = sig_of(r, a)
                sd_ref[sig:sig + 1, :] = jnp.sum(jnp.where(eye, blk, 0.0), axis=0, keepdims=True)
            advance_side(r)
        for sub in range(gp // DEC_SUB):
            pages = range(sub * DEC_SUB, (sub + 1) * DEC_SUB)
            lo = sig_of(pages[-1], 0)
            rows = slice(lo, lo + DEC_SUB * HEADS)
            tt = sd_ref[rows, :] + bias[rows, :]
            mx = jnp.max(tt, axis=0, keepdims=True)
            mx = _class_reduce(jnp.broadcast_to(mx, (SUBLANES, LANES)), jnp.maximum)[0:1, :]
            m_new = jnp.maximum(m_run, mx)
            pf = jnp.exp2(tt - m_new).astype(BF16).astype(F32)
            alpha = jnp.exp2(m_run - m_new)
            l_run = alpha * l_run + jnp.sum(pf, axis=0, keepdims=True)
            m_run = m_new
            acc = acc * _lane_to_col(alpha, eye8)
            for r in pages:
                parts = []
                for a in range(HEADS):
                    sig = sig_of(r, a) - lo
                    prow = jnp.broadcast_to(pf[sig:sig + 1, :], (LANES, LANES))
                    parts.append(jnp.where(eye, prow, 0.0).astype(BF16))
                pcol = jnp.dot(jnp.concatenate(parts, axis=0), ones, preferred_element_type=F32)
                pv = pcol * vbuf[slot, r]
                acc = acc + jnp.sum(pv.reshape(PAGE_ROWS // SUBLANES, SUBLANES, LANES), axis=0)
                advance_side(r)
    if side_stages is not None:
        _run(side_stages)

    acc4 = acc[0:HEADS, :] + acc[HEADS:2 * HEADS, :]
    mcol = _lane_to_col(m_run, eye8)[0:HEADS, :]
    lsum = _class_reduce(jnp.broadcast_to(l_run, (SUBLANES, LANES)), jnp.add)[0:1, :]
    lcol = _lane_to_col(lsum, eye8)[0:HEADS, :]
    q4 = q8_ref[0][0:HEADS, :].astype(F32)
    s_new = jnp.sum(q4 * kn_ref[0], axis=-1, keepdims=True)
    m_f = jnp.maximum(mcol, s_new)
    a1 = jnp.exp2(mcol - m_f)
    a2 = jnp.exp2(s_new - m_f)
    o = (acc4 * a1 + a2 * vn_ref[0]) / (lcol * a1 + a2)
    o_ref[0] = o * gate_ref[0].astype(F32)


def _mixer_kernel(pt_ref, ab_ref, qi_ref, kj_ref,
                  q8_ref, kn_ref, vn_ref, gate_ref, lfn_ref, ck_hbm, cv_hbm, clf_hbm,
                  aq_ref, ak_ref, av_ref, ack_ref, acq_ref, agate_ref,
                  gqkv_ref, ggz_ref, gcol_ref, grow_ref, gcw_ref, ggn_ref,
                  o_ref, ao_ref, go_ref, gssm_ref, gconv_ref,
                  kbuf, vbuf, lfbuf, sems, sd_ref, m_ref, l_ref, acc_ref, cqb_ref,
                  gs_ref, gx_ref, gq_ref, gk_ref, gv_ref, *, n_att, n_gdn, gdn_blocks):
    del ab_ref
    step = pl.program_id(0)
    n_seq = pt_ref.shape[0]
    hbm, bufs = (ck_hbm, cv_hbm, clf_hbm), (kbuf, vbuf, lfbuf)

    def gdn_stages():
        return _gdn_block_stages(lax.rem(step, gdn_blocks), gdn_blocks, gqkv_ref, ggz_ref, gcol_ref, grow_ref,
                                 gcw_ref, ggn_ref, go_ref, gssm_ref, gconv_ref, gs_ref, gx_ref, gq_ref, gk_ref, gv_ref)

    @pl.when(step < n_seq)
    def _():
        _dec_start(step, pt_ref, hbm, bufs, sems)

    @pl.when(step < n_att)
    def _():
        _att_step(qi_ref[step], kj_ref[step], aq_ref, ak_ref, av_ref, ack_ref, acq_ref, agate_ref, ao_ref,
                  m_ref, l_ref, acc_ref, cqb_ref)

    dec_args = (step, n_seq, pt_ref, q8_ref, kn_ref, vn_ref, gate_ref, lfn_ref, hbm, o_ref, bufs, sems, sd_ref)
    if n_gdn == n_seq:
        @pl.when(step < n_seq)
        def _():
            _dec_finish(*dec_args, side_stages=gdn_stages())
    else:
        @pl.when(step < n_seq)
        def _():
            _dec_finish(*dec_args, before_last_wait=lambda: pl.when(step < n_gdn)(lambda: _run(gdn_stages())))

        @pl.when((step >= n_seq) & (step < n_gdn))
        def _():
            _run(gdn_stages())


def _mixers(page_table, q8, k_new, v_new, gate, lf_new, cache_k2, cache_v2, cache_lf3,
            aq, ak, av, row, col, agate, gqkv, ggz, conv_w, gdn_norm_g, batch, seq):
    nb, n_pages = page_table.shape
    gp = DEC_PAGES
    assert n_pages % (2 * gp) == 0, "an even number of page groups per sequence keeps the buffer slots static"
    n = seq // ATT_T
    steps = [(bb, i, j) for bb in range(batch) for i in range(n) for j in range(i + 1)]
    n_att = len(steps)
    gdn_blocks = seq // GDN_TB
    n_gdn = batch * gdn_blocks
    n_steps = max(nb, n_att, n_gdn)
    steps = steps + [steps[-1]] * (n_steps - n_att)
    ab_tab, qi_tab, kj_tab = (jnp.asarray([s[k] for s in steps], jnp.int32) for k in range(3))

    per_b = lambda s, *_: (jnp.minimum(s, nb - 1), 0, 0)
    qmap = lambda s, pt, ab, qi, kj: (ab[s] * n + qi[s], 0)
    kmap = lambda s, pt, ab, qi, kj: (ab[s] * n + kj[s], 0)
    gblk = lambda s: jnp.minimum(s, n_gdn - 1)
    grow = lambda s, *_: (gblk(s), 0)
    gseq = lambda s, *_: (gblk(s) // gdn_blocks, 0, 0)
    const2 = lambda s, *_: (0, 0)
    hbm = pl.BlockSpec(memory_space=pl.ANY)
    grid_spec = pltpu.PrefetchScalarGridSpec(
        num_scalar_prefetch=4,
        grid=(n_steps,),
        in_specs=[pl.BlockSpec((1, Q_ROWS, HEAD_DIM), per_b),
                  pl.BlockSpec((1, HEADS, HEAD_DIM), per_b),
                  pl.BlockSpec((1, HEADS, HEAD_DIM), per_b),
                  pl.BlockSpec((1, HEADS, HEAD_DIM), per_b),
                  pl.BlockSpec((1, 1, LANES), per_b),
                  hbm, hbm, hbm,
                  pl.BlockSpec((ATT_T, WIDTH), qmap),
                  pl.BlockSpec((ATT_T, WIDTH), kmap),
                  pl.BlockSpec((ATT_T, WIDTH), kmap),
                  pl.BlockSpec((1, 2 * SUBLANES, ATT_T), lambda s, pt, ab, qi, kj: (ab[s], 0, kj[s])),
                  pl.BlockSpec((ATT_T, LANES), qmap),
                  pl.BlockSpec((ATT_T, WIDTH), qmap),
                  pl.BlockSpec((GDN_TB, CONV_DIM), grow),
                  pl.BlockSpec((GDN_TB, WIDTH), grow),
                  pl.BlockSpec((GDN_TB, LANES), grow),
                  pl.BlockSpec((1, 2 * SUBLANES, GDN_TB),
                               lambda s, *_: (gblk(s) // gdn_blocks, 0, gblk(s) % gdn_blocks)),
                  pl.BlockSpec((CONV_W, CONV_DIM), const2),
                  pl.BlockSpec((1, HEAD_DIM), const2)],
        out_specs=[pl.BlockSpec((1, HEADS, HEAD_DIM), per_b),
                   pl.BlockSpec((ATT_T, WIDTH), qmap),
                   pl.BlockSpec((GDN_TB, WIDTH), grow),
                   pl.BlockSpec((1, HEADS, HEAD_DIM, HEAD_DIM), lambda s, *_: (gblk(s) // gdn_blocks, 0, 0, 0)),
                   pl.BlockSpec((1, CONV_W - 1, CONV_DIM), gseq)],
        scratch_shapes=[pltpu.VMEM((2, gp, PAGE_ROWS, HEAD_DIM), F32),
                        pltpu.VMEM((2, gp, PAGE_ROWS, HEAD_DIM), F32),
                        pltpu.VMEM((2, gp * HEADS, LANES), F32),
                        pltpu.SemaphoreType.DMA((3, 2)),
                        pltpu.VMEM((gp * HEADS, LANES), F32)]
                       + [pltpu.VMEM((HEADS, ATT_T, LANES), F32)] * 4
                       + [pltpu.VMEM((HEADS, HEAD_DIM, HEAD_DIM), F32),
                          pltpu.VMEM((GDN_TB + SUBLANES, CONV_DIM), F32)]
                       + [pltpu.VMEM((GDN_TB, WIDTH), F32)] * 3)
    return pl.pallas_call(
        functools.partial(_mixer_kernel, n_att=n_att, n_gdn=n_gdn, gdn_blocks=gdn_blocks),
        grid_spec=grid_spec,
        out_shape=[jax.ShapeDtypeStruct((nb, HEADS, HEAD_DIM), F32),
                   jax.ShapeDtypeStruct((batch * seq, WIDTH), BF16),
                   jax.ShapeDtypeStruct((batch * seq, WIDTH), BF16),
                   jax.ShapeDtypeStruct((batch, HEADS, HEAD_DIM, HEAD_DIM), F32),
                   jax.ShapeDtypeStruct((batch, CONV_W - 1, CONV_DIM), F32)],
        compiler_params=pltpu.CompilerParams(dimension_semantics=("arbitrary",),
                                             vmem_limit_bytes=DEC_VMEM_LIMIT),
        name="mixers",
    )(page_table, ab_tab, qi_tab, kj_tab, q8, k_new, v_new, gate, lf_new, cache_k2, cache_v2, cache_lf3,
      aq, ak, av, row, col, agate, gqkv, ggz, col, row, conv_w, gdn_norm_g)


def _layer(x_prompt, x_sample, cache_k, cache_v, cache_logf, state_ssm, state_conv, page_table,
           c_prompt, c_sample, norm_g, w_ada, b_ada, w_in, b_f, conv_w, a_log, dt_bias,
           gdn_norm_g, q_norm_g, k_norm_g, w_out):
    batch, seq, _ = x_prompt.shape
    nb = x_sample.shape[0]
    w = WIDTH
    o_small = 4 * w
    o_fox = o_small + 2 * HEADS
    w_t = w_in.T
    w_big = jnp.concatenate([w_t[0:o_small], w_t[o_fox:o_fox + 4 * w]], axis=0).astype(BF16)
    w_small = jnp.concatenate([w_t[o_small:o_fox], w_t[o_fox + 4 * w:],
                               jnp.zeros((LANES - 3 * HEADS, D_MODEL), F32)], axis=0).astype(BF16)
    w_out_b = w_out.astype(BF16)
    par = jnp.zeros((SUBLANES, LANES), F32)
    par = par.at[0, LANE_G:LANE_G + HEADS].set(a_log)
    par = par.at[1, LANE_G:LANE_G + HEADS].set(dt_bias)
    par = par.at[2, LANE_F:LANE_F + HEADS].set(b_f)
    ng2, qg2, kg2, gg2 = (a.reshape(1, -1) for a in (norm_g, q_norm_g, k_norm_g, gdn_norm_g))

    cp8 = jnp.pad(c_prompt, ((0, SUBLANES - batch), (0, 0)))
    mod_p, mod_s = _ada_mod(cp8, c_sample, w_ada, b_ada.reshape(1, -1))
    mod_p4 = mod_p[:, 0:batch].reshape(3, batch, 1, D_MODEL)
    mod_s4 = mod_s.reshape(3, 1, nb, D_MODEL)

    xp = x_prompt.reshape(batch * seq, D_MODEL)
    qkv, gz, small, fq, fk, fkb, fv, fvb, fzg = _inproj(xp, mod_p4, False, PROMPT_TM, ng2, w_big, w_small, qg2, kg2)
    col, row = _small_prep(small, par, batch, seq)
    k_p = fk.reshape(batch, seq, HEADS, HEAD_DIM)
    v_p = fv.reshape(batch, seq, HEADS, HEAD_DIM)
    lf_p = col[:, LANE_LF:LANE_LF + HEADS].reshape(batch, seq, HEADS)

    xs = x_sample.reshape(nb, D_MODEL)
    qkv_s, gz_s, small_s, fq_s, fk_s, _, fv_s, _, fzg_s = _inproj(xs, mod_s4, True, nb, ng2, w_big, w_small, qg2, kg2)
    qn_s, kn_s, vn_s, sc_s = _sample_mid(qkv_s, state_conv[:, 0], state_conv[:, 1], state_conv[:, 2],
                                         conv_w, small_s, par)
    h3 = lambda a: a.reshape(nb, HEADS, HEAD_DIM)
    bcast = lambda a: jnp.broadcast_to(a[:, :, None], (nb, HEADS, HEAD_DIM))
    t8 = jnp.stack([h3(kn_s), h3(qn_s), h3(vn_s), bcast(sc_s[:, LANE_B:LANE_B + HEADS]),
                    bcast(sc_s[:, LANE_G:LANE_G + HEADS])]
                   + [jnp.zeros((nb, HEADS, HEAD_DIM), F32)] * (SUBLANES - 5), axis=2)
    o_a_s, ssm_s = _gdn_step(t8, state_ssm, h3(gz_s), gg2)
    lf_s = sc_s[:, LANE_F:LANE_F + HEADS]
    q8 = jnp.tile(h3(fq_s), (1, Q_ROWS // HEADS, 1))
    lf_new = jnp.tile(lf_s, (1, LANES // HEADS)).reshape(nb, 1, LANES)
    n_pool = cache_k.shape[0]
    o_b_s, o_b, o_a, ssm_p, conv_p = _mixers(
        page_table, q8, h3(fk_s), h3(fv_s), h3(fzg_s), lf_new,
        cache_k.reshape(n_pool * PAGE_ROWS, HEAD_DIM), cache_v.reshape(n_pool * PAGE_ROWS, HEAD_DIM),
        cache_logf.reshape(n_pool, HEADS, LANES),
        fq, fkb, fvb, row, col, fzg, qkv, gz, conv_w, gg2, batch, seq)
    y_p = _outproj(xp, o_a, o_b, mod_p4, False, OUT_TM, w_out_b).reshape(batch, seq, D_MODEL)
    y_s = _outproj(xs, o_a_s.reshape(nb, w), o_b_s.reshape(nb, w), mod_s4, True, nb, w_out_b).reshape(nb, 1, D_MODEL)
    conv_s = jnp.stack([state_conv[:, 1], state_conv[:, 2], qkv_s], axis=1)
    return (y_p, y_s, k_p, v_p, lf_p, ssm_p, conv_p,
            h3(fk_s).reshape(nb, 1, HEADS, HEAD_DIM), h3(fv_s).reshape(nb, 1, HEADS, HEAD_DIM),
            lf_s.reshape(nb, 1, HEADS), ssm_s, conv_s)


def kernel(x_prompt, x_sample, cache_k, cache_v, cache_logf, state_ssm, state_conv, page_table, c_prompt, c_sample, norm_g, w_ada, b_ada, w_in, b_f, conv_w, a_log, dt_bias, gdn_norm_g, q_norm_g, k_norm_g, w_out):
    assert w_ada.shape[0] == 1, "single layer"
    outs = _layer(x_prompt, x_sample, cache_k[0], cache_v[0], cache_logf[0], state_ssm[0], state_conv[0],
                  page_table, c_prompt, c_sample, norm_g[0], w_ada[0], b_ada[0], w_in[0], b_f[0], conv_w[0],
                  a_log[0], dt_bias[0], gdn_norm_g[0], q_norm_g[0], k_norm_g[0], w_out[0])
    y_p, y_s = outs[0], outs[1]
    return (y_p, y_s) + tuple(o[None] for o in outs[2:])
```

```python
import functools

import jax
import jax.numpy as jnp
from jax import lax
from jax.experimental import pallas as pl
from jax.experimental.pallas import tpu as pltpu

F32 = jnp.float32
BF16 = jnp.bfloat16

D_MODEL = 1024
HEADS = 4
HEAD_DIM = 128
WIDTH = HEADS * HEAD_DIM
CONV_W = 4
CONV_DIM = 3 * WIDTH
CHUNK = 64
PAGE_SIZE = 128
PAGE_ROWS = PAGE_SIZE * HEADS
NORM_EPS = 1e-6
LANES = 128
SUBLANES = 8
NEG_BIG = -1e30
LOG2E = 1.4426950408889634
ATT_Q_SCALE = LOG2E * HEAD_DIM ** -0.5
VMEM_LIMIT = 48 * 1024 * 1024

COL_GQKV = 0
COL_GZ = 3 * WIDTH
COL_FQ = 4 * WIDTH
COL_FK = 5 * WIDTH
COL_FV = 6 * WIDTH
COL_FZ = 7 * WIDTH
BIG_COLS = 8 * WIDTH
LANE_B, LANE_G, LANE_F, LANE_LF = 0, 4, 8, 12

PROMPT_TM = 512
OUT_TM = 1024
SMALL_TB = 512
GDN_TB = 128
ATT_T = 512
DEC_PAGES = 32
DEC_SUB = 16
DEC_SIDE_EVERY = 8
DEC_VMEM_LIMIT = 56 * 1024 * 1024
Q_ROWS = 16
STEP_BATCH = 8


def _mm(a, b):
    return jnp.dot(a.astype(BF16), b.astype(BF16), preferred_element_type=F32)


def _mm_nt(a, b):
    return lax.dot_general(a.astype(BF16), b.astype(BF16), (((1,), (1,)), ((), ())),
                           preferred_element_type=F32)


def _mm_tn(a, b):
    return lax.dot_general(a.astype(BF16), b.astype(BF16), (((0,), (0,)), ((), ())),
                           preferred_element_type=F32)


def _mm_exact_rhs(mat_b, y):
    hi = y.astype(BF16)
    r1 = y - hi.astype(F32)
    mid = r1.astype(BF16)
    lo = (r1 - mid.astype(F32)).astype(BF16)
    dot = functools.partial(jnp.dot, preferred_element_type=F32)
    return dot(mat_b, hi) + dot(mat_b, mid) + dot(mat_b, lo)


def _mm_exact_lhs(y, mat_b):
    hi = y.astype(BF16)
    r1 = y - hi.astype(F32)
    mid = r1.astype(BF16)
    lo = (r1 - mid.astype(F32)).astype(BF16)
    dot = functools.partial(jnp.dot, preferred_element_type=F32)
    return dot(hi, mat_b) + dot(mid, mat_b) + dot(lo, mat_b)


def _sigmoid(x):
    return 1.0 / (1.0 + jnp.exp(-x))


def _silu(x):
    return x * _sigmoid(x)


def _softplus(x):
    return jnp.maximum(x, 0.0) + jnp.log(1.0 + jnp.exp(-jnp.abs(x)))


def _ones_mask(mask):
    return jnp.where(mask, 1.0, 0.0).astype(BF16)


def _params(n_axes):
    return pltpu.CompilerParams(dimension_semantics=("arbitrary",) * n_axes,
                                vmem_limit_bytes=VMEM_LIMIT)


def _ada_kernel(cp_ref, cs_ref, w_ref, b_ref, op_ref, os_ref):
    w = w_ref[...].astype(BF16)
    b = b_ref[...]
    op_ref[0] = jnp.dot(_silu(cp_ref[...]).astype(BF16), w, preferred_element_type=F32) + b
    os_ref[0] = jnp.dot(_silu(cs_ref[...]).astype(BF16), w, preferred_element_type=F32) + b


def _ada_mod(c_prompt8, c_sample, w_ada, b_ada):
    nb = c_sample.shape[0]
    return pl.pallas_call(
        _ada_kernel,
        grid=(3,),
        in_specs=[pl.BlockSpec((SUBLANES, D_MODEL), lambda k: (0, 0)),
                  pl.BlockSpec((nb, D_MODEL), lambda k: (0, 0)),
                  pl.BlockSpec((D_MODEL, D_MODEL), lambda k: (0, k)),
                  pl.BlockSpec((1, D_MODEL), lambda k: (0, k))],
        out_specs=[pl.BlockSpec((1, SUBLANES, D_MODEL), lambda k: (k, 0, 0)),
                   pl.BlockSpec((1, nb, D_MODEL), lambda k: (k, 0, 0))],
        out_shape=[jax.ShapeDtypeStruct((3, SUBLANES, D_MODEL), F32),
                   jax.ShapeDtypeStruct((3, nb, D_MODEL), F32)],
        compiler_params=_params(1),
        name="ada_mod",
    )(c_prompt8, c_sample, w_ada, b_ada)


def _inproj_kernel(x_ref, sh_ref, sc_ref, ng_ref, wb_ref, ws_ref, qg_ref, kg_ref,
                   qkv_ref, gz_ref, sm_ref, fq_ref, fk_ref, fkb_ref, fv_ref, fvb_ref, fzg_ref):
    x = x_ref[...]
    y = x * lax.rsqrt(jnp.mean(x * x, axis=-1, keepdims=True) + NORM_EPS)
    hdn = (y * ng_ref[...]) * (1.0 + sc_ref[0, 0]) + sh_ref[0, 0]
    hb = hdn.astype(BF16)
    dot = lambda a, wt: lax.dot_general(a, wt, (((1,), (1,)), ((), ())), preferred_element_type=F32)
    qkv_ref[...] = dot(hb, wb_ref[COL_GQKV:COL_GZ, :])
    gz_ref[...] = dot(hb, wb_ref[COL_GZ:COL_FQ, :])
    sm_ref[...] = dot(hb, ws_ref[...])
    fq = dot(hb, wb_ref[COL_FQ:COL_FK, :])
    fk = dot(hb, wb_ref[COL_FK:COL_FV, :])
    fv = dot(hb, wb_ref[COL_FV:COL_FZ, :])
    fz = dot(hb, wb_ref[COL_FZ:BIG_COLS, :])
    fvb_ref[...] = fv.astype(BF16)
    fzg_ref[...] = _silu(fz).astype(BF16)
    scale = ATT_Q_SCALE
    tm = x.shape[0]
    for h in range(HEADS):
        sl = slice(h * HEAD_DIM, (h + 1) * HEAD_DIM)
        qh = fq[:, sl]
        qn = (qh * lax.rsqrt(jnp.mean(qh * qh, axis=-1, keepdims=True) + NORM_EPS)) * qg_ref[...]
        fq_ref[:, sl] = (qn * scale).astype(BF16)
        kh = fk[:, sl]
        kn = (kh * lax.rsqrt(jnp.mean(kh * kh, axis=-1, keepdims=True) + NORM_EPS)) * kg_ref[...]
        fkb_ref[:, sl] = kn.astype(BF16)
        fk_ref[pl.ds(h, tm, stride=HEADS), :] = kn
        fv_ref[pl.ds(h, tm, stride=HEADS), :] = fv[:, sl]


def _inproj(x2d, mod4, per_row_mod, tm, norm_g, w_big, w_small, q_norm_g, k_norm_g):
    rows = x2d.shape[0]
    nblk = rows // tm
    if per_row_mod:
        mod_block = (1, 1, tm, D_MODEL)
        sh_map = lambda i: (0, 0, i, 0)
        sc_map = lambda i: (1, 0, i, 0)
    else:
        blocks_per_batch = nblk // mod4.shape[1]
        mod_block = (1, 1, 1, D_MODEL)
        sh_map = lambda i: (0, i // blocks_per_batch, 0, 0)
        sc_map = lambda i: (1, i // blocks_per_batch, 0, 0)
    row_spec = lambda n: pl.BlockSpec((tm, n), lambda i: (i, 0))
    const = lambda shape: pl.BlockSpec(shape, lambda i: (0,) * len(shape))
    outs = [(CONV_DIM, F32, 1), (WIDTH, F32, 1), (LANES, F32, 1), (WIDTH, BF16, 1), (HEAD_DIM, F32, HEADS),
            (WIDTH, BF16, 1), (HEAD_DIM, F32, HEADS), (WIDTH, BF16, 1), (WIDTH, BF16, 1)]
    return pl.pallas_call(
        _inproj_kernel,
        grid=(nblk,),
        in_specs=[row_spec(D_MODEL),
                  pl.BlockSpec(mod_block, sh_map), pl.BlockSpec(mod_block, sc_map),
                  const((1, D_MODEL)), const((BIG_COLS, D_MODEL)), const((LANES, D_MODEL)),
                  const((1, HEAD_DIM)), const((1, HEAD_DIM))],
        out_specs=[pl.BlockSpec((tm * r, n), lambda i: (i, 0)) for n, _, r in outs],
        out_shape=[jax.ShapeDtypeStruct((rows * r, n), dt) for n, dt, r in outs],
        compiler_params=_params(1),
        name="in_proj",
    )(x2d, mod4, mod4, norm_g, w_big, w_small, q_norm_g, k_norm_g)


def _token_scalars(x, par):
    beta = _sigmoid(x)
    g = -jnp.exp(par[0:1, :]) * _softplus(x + par[1:2, :])
    lf = -_softplus(-(x + par[2:3, :]))
    return beta, g, lf


def _small_kernel(sm_ref, par_ref, col_ref, row_ref, carry_ref):
    tb = SMALL_TB

    @pl.when(pl.program_id(1) == 0)
    def _():
        carry_ref[...] = jnp.zeros_like(carry_ref)

    x = sm_ref[...]
    beta, g, lf = _token_scalars(x, par_ref[...])
    lane = lax.broadcasted_iota(jnp.int32, (tb, LANES), 1)
    is_b = lane < LANE_G
    is_g = (lane >= LANE_G) & (lane < LANE_F)
    is_f = (lane >= LANE_F) & (lane < LANE_LF)
    gm = jnp.where(is_g, g, 0.0)
    fm = jnp.where(is_f, lf, 0.0)
    r = lax.broadcasted_iota(jnp.int32, (tb, tb), 0)
    c = lax.broadcasted_iota(jnp.int32, (tb, tb), 1)
    tri = c <= r
    same_chunk = jnp.right_shift(r, 6) == jnp.right_shift(c, 6)
    gcum = _mm_exact_rhs(_ones_mask(tri & same_chunk), gm)
    cum = _mm_exact_rhs(_ones_mask(tri), fm) + carry_ref[...]
    cum = jnp.where(is_f, cum, 0.0)
    carry_ref[...] = cum[tb - 1:tb, :]
    col = jnp.where(is_b, beta, 0.0) + gcum + cum + pltpu.roll(fm, LANE_LF - LANE_F, axis=1)
    col_ref[...] = col
    for i in range(tb // LANES):
        t = col[i * LANES:(i + 1) * LANES, :]
        row_ref[0, :, i * LANES:(i + 1) * LANES] = t.T[0:2 * SUBLANES, :]


def _small_prep(small, par, batch, seq):
    nt = seq // SMALL_TB
    return pl.pallas_call(
        _small_kernel,
        grid=(batch, nt),
        in_specs=[pl.BlockSpec((SMALL_TB, LANES), lambda b, t: (b * nt + t, 0)),
                  pl.BlockSpec((SUBLANES, LANES), lambda b, t: (0, 0))],
        out_specs=[pl.BlockSpec((SMALL_TB, LANES), lambda b, t: (b * nt + t, 0)),
                   pl.BlockSpec((1, 2 * SUBLANES, SMALL_TB), lambda b, t: (b, 0, t))],
        out_shape=[jax.ShapeDtypeStruct((batch * seq, LANES), F32),
                   jax.ShapeDtypeStruct((batch, 2 * SUBLANES, seq), F32)],
        scratch_shapes=[pltpu.VMEM((1, LANES), F32)],
        compiler_params=_params(2),
        name="token_scalars",
    )(small, par)


_GDN_TAIL = "tail"


def _run(stages):
    try:
        while True:
            next(stages)
    except StopIteration as done:
        return done.value


def _unit_lower_inverses(ms, eye, ri, ci):
    base = 3
    same = jnp.right_shift(ri, base) == jnp.right_shift(ci, base)
    xps = [-jnp.where(same, m, 0.0) for m in ms]
    ps = [eye + xp for xp in xps]
    for _ in range(base - 1):
        xps = [_mm(xp, xp) for xp in xps]
        yield
        ps = [p + _mm(p, xp) for p, xp in zip(ps, xps)]
        yield
    size = base
    while (1 << size) < CHUNK:
        off = ((jnp.right_shift(ri, size + 1) == jnp.right_shift(ci, size + 1))
               & (jnp.right_shift(ri, size) != jnp.right_shift(ci, size)))
        pcs = [_mm(p, jnp.where(off, m, 0.0)) for p, m in zip(ps, ms)]
        yield
        ps = [p - _mm(pc, p) for p, pc in zip(ps, pcs)]
        yield
        size += 1
    return ps


def _gdn_chunk_preps(qs, ks, vs, betas, gcs, grs, incl, strict, eye, ri, ci):
    decays = [jnp.where(incl, jnp.exp(jnp.where(incl, gc - gr, 0.0)), 0.0) for gc, gr in zip(gcs, grs)]
    kbs = [k * beta for k, beta in zip(ks, betas)]
    boths = [_mm_nt(jnp.concatenate([kb, q], axis=0), k) for kb, q, k in zip(kbs, qs, ks)]
    yield
    ms = [jnp.where(strict, both[0:CHUNK] * decay, 0.0) for both, decay in zip(boths, decays)]
    a_intras = [both[CHUNK:2 * CHUNK] * decay for both, decay in zip(boths, decays)]
    tinvs = yield from _unit_lower_inverses(ms, eye, ri, ci)
    egs = [jnp.exp(gc) for gc in gcs]
    uws = [_mm(tinv, jnp.concatenate([v * beta, kb * eg], axis=1))
           for tinv, v, beta, kb, eg in zip(tinvs, vs, betas, kbs, egs)]
    yield
    preps = []
    for uw, q, k, eg, gc, a_intra in zip(uws, qs, ks, egs, gcs, a_intras):
        g_last = gc[CHUNK - 1:CHUNK, :]
        wq = jnp.concatenate([uw[:, HEAD_DIM:2 * HEAD_DIM], q * eg], axis=0)
        preps.append((uw[:, 0:HEAD_DIM], wq, a_intra, k * jnp.exp(g_last - gc), jnp.exp(g_last)))
    return preps


def _gdn_chunk_steps(preps, states):
    boths = [_mm(p[1], s) for p, s in zip(preps, states)]
    yield
    v_news = [p[0] - both[0:CHUNK] for p, both in zip(preps, boths)]
    outs = [both[CHUNK:2 * CHUNK] + _mm(p[2], v_new) for p, both, v_new in zip(preps, boths, v_news)]
    new_states = [s * p[4] + _mm_tn(p[3], v_new) for p, s, v_new in zip(preps, states, v_news)]
    yield
    return outs, new_states


def _gdn_block_stages(t, n_t, qkv_ref, gz_ref, col_ref, row_ref, cw_ref, gn_ref,
                      o_ref, ssm_ref, conv_ref, s_ref, xbuf_ref, qn_ref, kn_ref, vn_ref):
    tb = GDN_TB
    halo = SUBLANES

    @pl.when(t == 0)
    def _():
        s_ref[...] = jnp.zeros_like(s_ref)
        xbuf_ref[0:halo, :] = jnp.zeros((halo, CONV_DIM), F32)

    xbuf_ref[halo:halo + tb, :] = qkv_ref[...]
    scale = HEAD_DIM ** -0.5
    for cb in range(CONV_DIM // LANES):
        sl = slice(cb * LANES, (cb + 1) * LANES)
        acc = xbuf_ref[halo - 3:halo - 3 + tb, sl] * cw_ref[0:1, sl]
        for i in range(1, CONV_W):
            acc = acc + xbuf_ref[halo - 3 + i:halo - 3 + i + tb, sl] * cw_ref[i:i + 1, sl]
        act = _silu(acc)
        part, h = divmod(cb, HEADS)
        hs = slice(h * LANES, (h + 1) * LANES)
        if part == 0:
            qn_ref[:, hs] = act * lax.rsqrt(jnp.sum(act * act, axis=-1, keepdims=True) + NORM_EPS) * scale
        elif part == 1:
            kn_ref[:, hs] = act * lax.rsqrt(jnp.sum(act * act, axis=-1, keepdims=True) + NORM_EPS)
        else:
            vn_ref[:, hs] = act
    tail = xbuf_ref[tb:tb + halo, :]
    xbuf_ref[0:halo, :] = tail
    yield

    ri = lax.broadcasted_iota(jnp.int32, (CHUNK, CHUNK), 0)
    ci = lax.broadcasted_iota(jnp.int32, (CHUNK, CHUNK), 1)
    incl = ci <= ri
    strict = ci < ri
    eye = jnp.where(ci == ri, 1.0, 0.0)

    colt = col_ref[...]
    rowt = row_ref[0]
    n_cc = tb // CHUNK
    items = [(cc, h) for cc in range(n_cc) for h in range(HEADS)]
    rs = lambda cc: slice(cc * CHUNK, (cc + 1) * CHUNK)
    hs = lambda h: slice(h * HEAD_DIM, (h + 1) * HEAD_DIM)
    preps = yield from _gdn_chunk_preps(
        [qn_ref[rs(cc), hs(h)] for cc, h in items],
        [kn_ref[rs(cc), hs(h)] for cc, h in items],
        [vn_ref[rs(cc), hs(h)] for cc, h in items],
        [colt[rs(cc), LANE_B + h:LANE_B + h + 1] for cc, h in items],
        [colt[rs(cc), LANE_G + h:LANE_G + h + 1] for cc, h in items],
        [rowt[LANE_G + h:LANE_G + h + 1, rs(cc)] for cc, h in items],
        incl, strict, eye, ri, ci)
    states = [s_ref[h] for h in range(HEADS)]
    for cc in range(n_cc):
        outs, states = yield from _gdn_chunk_steps(preps[cc * HEADS:(cc + 1) * HEADS], states)
        for h, o in enumerate(outs):
            on = (o * lax.rsqrt(jnp.mean(o * o, axis=-1, keepdims=True) + NORM_EPS)) * gn_ref[...]
            o_ref[rs(cc), hs(h)] = (on * _silu(gz_ref[rs(cc), hs(h)])).astype(o_ref.dtype)
    for h in range(HEADS):
        s_ref[h] = states[h]
    yield _GDN_TAIL

    @pl.when(t == n_t - 1)
    def _():
        ssm_ref[0] = s_ref[...]
        conv_ref[0] = tail[halo - (CONV_W - 1):halo, :]


def _att_step(qi, kj, q_ref, k_ref, v_ref, ck_ref, cq_ref, gate_ref, o_ref, m_ref, l_ref, acc_ref, cqb_ref):
    tq = tk = ATT_T
    heads = range(HEADS)
    hs = lambda h: slice(h * HEAD_DIM, (h + 1) * HEAD_DIM)

    @pl.when(kj == 0)
    def _():
        m_ref[...] = jnp.full_like(m_ref, NEG_BIG)
        l_ref[...] = jnp.zeros_like(l_ref)
        acc_ref[...] = jnp.zeros_like(acc_ref)
        for h in heads:
            cqb_ref[h] = jnp.broadcast_to(cq_ref[:, LANE_F + h:LANE_F + h + 1] * LOG2E, (tq, LANES))

    def update(diagonal):
        ss = [lax.dot_general(q_ref[:, hs(h)], k_ref[:, hs(h)], (((1,), (1,)), ((), ())),
                              preferred_element_type=F32) for h in heads]
        tts = [s - ck_ref[0, LANE_F + h:LANE_F + h + 1, :] * LOG2E for h, s in zip(heads, ss)]
        if diagonal:
            visible = (lax.broadcasted_iota(jnp.int32, (tq, tk), 1)
                       <= lax.broadcasted_iota(jnp.int32, (tq, tk), 0))
            tts = [jnp.where(visible, tt, NEG_BIG) for tt in tts]
        cqs = [cqb_ref[h] for h in heads]
        m_olds = [m_ref[h] for h in heads]
        m_news = [jnp.maximum(m_old, jnp.broadcast_to(jnp.max(tt, axis=1, keepdims=True), (tq, LANES)) + cq)
                  for m_old, tt, cq in zip(m_olds, tts, cqs)]
        shifts = [jnp.concatenate([m_new - cq] * (tk // LANES), axis=1) for m_new, cq in zip(m_news, cqs)]
        ps = [jnp.exp2(tt - shift).astype(BF16) for tt, shift in zip(tts, shifts)]
        ones = jnp.ones((tk, HEAD_DIM), BF16)
        pvs = [jnp.dot(p, jnp.concatenate([v_ref[:, hs(h)], ones], axis=1), preferred_element_type=F32)
               for h, p in zip(heads, ps)]
        alphas = [jnp.exp2(m_old - m_new) for m_old, m_new in zip(m_olds, m_news)]
        accs = [alpha * acc_ref[h] + pv[:, 0:HEAD_DIM] for h, alpha, pv in zip(heads, alphas, pvs)]
        ls = [alpha * l_ref[h] + pv[:, HEAD_DIM:2 * HEAD_DIM] for h, alpha, pv in zip(heads, alphas, pvs)]
        return m_news, ls, accs

    @pl.when(kj < qi)
    def _():
        m_news, ls, accs = update(False)
        for h in heads:
            m_ref[h] = m_news[h]
            l_ref[h] = ls[h]
            acc_ref[h] = accs[h]

    @pl.when(kj == qi)
    def _():
        _, ls, accs = update(True)
        for h in heads:
            o_ref[:, hs(h)] = ((accs[h] / ls[h]) * gate_ref[:, hs(h)].astype(F32)).astype(o_ref.dtype)


def _outproj_kernel(x_ref, oa_ref, ob_ref, gate_ref, w_ref, y_ref):
    mix = (jnp.dot(oa_ref[...].astype(BF16), w_ref[0:WIDTH, :], preferred_element_type=F32)
           + jnp.dot(ob_ref[...].astype(BF16), w_ref[WIDTH:2 * WIDTH, :], preferred_element_type=F32))
    y_ref[...] = x_ref[...] + gate_ref[0, 0] * mix


def _outproj(x2d, oa, ob, mod4, per_row_mod, tm, w_out_b):
    rows = x2d.shape[0]
    nblk = rows // tm
    if per_row_mod:
        mod_block = (1, 1, tm, D_MODEL)
        g_map = lambda i: (2, 0, i, 0)
    else:
        blocks_per_batch = nblk // mod4.shape[1]
        mod_block = (1, 1, 1, D_MODEL)
        g_map = lambda i: (2, i // blocks_per_batch, 0, 0)
    row_spec = lambda n: pl.BlockSpec((tm, n), lambda i: (i, 0))
    return pl.pallas_call(
        _outproj_kernel,
        grid=(nblk,),
        in_specs=[row_spec(D_MODEL), row_spec(WIDTH), row_spec(WIDTH),
                  pl.BlockSpec(mod_block, g_map),
                  pl.BlockSpec((2 * WIDTH, D_MODEL), lambda i: (0, 0))],
        out_specs=row_spec(D_MODEL),
        out_shape=jax.ShapeDtypeStruct((rows, D_MODEL), F32),
        compiler_params=_params(1),
        name="out_proj",
    )(x2d, oa, ob, mod4, w_out_b)


def _smid_kernel(u_ref, s0_ref, s1_ref, s2_ref, cw_ref, sm_ref, par_ref,
                 qn_ref, kn_ref, vn_ref, sc_ref):
    conv = (s0_ref[...] * cw_ref[0:1, :] + s1_ref[...] * cw_ref[1:2, :]
            + s2_ref[...] * cw_ref[2:3, :] + u_ref[...] * cw_ref[3:4, :])
    act = _silu(conv)
    for h in range(HEADS):
        hs = slice(h * HEAD_DIM, (h + 1) * HEAD_DIM)
        q = act[:, h * HEAD_DIM:(h + 1) * HEAD_DIM]
        k = act[:, WIDTH + h * HEAD_DIM:WIDTH + (h + 1) * HEAD_DIM]
        qn_ref[:, hs] = q * lax.rsqrt(jnp.sum(q * q, axis=-1, keepdims=True) + NORM_EPS)
        kn_ref[:, hs] = k * lax.rsqrt(jnp.sum(k * k, axis=-1, keepdims=True) + NORM_EPS)
    vn_ref[...] = act[:, 2 * WIDTH:3 * WIDTH]
    beta, g, lf = _token_scalars(sm_ref[...], par_ref[...])
    lane = lax.broadcasted_iota(jnp.int32, sm_ref.shape, 1)
    sc_ref[...] = jnp.where(lane < LANE_G, beta, jnp.where(lane < LANE_F, g, lf))


def _sample_mid(u, s0, s1, s2, conv_w, small, par):
    nb = u.shape[0]
    full = lambda a: pl.BlockSpec(a.shape, lambda: (0,) * a.ndim)
    args = (u, s0, s1, s2, conv_w, small, par)
    return pl.pallas_call(
        _smid_kernel,
        in_specs=[full(a) for a in args],
        out_specs=[pl.BlockSpec((nb, WIDTH), lambda: (0, 0))] * 3 + [pl.BlockSpec((nb, LANES), lambda: (0, 0))],
        out_shape=[jax.ShapeDtypeStruct((nb, WIDTH), F32)] * 3 + [jax.ShapeDtypeStruct((nb, LANES), F32)],
        compiler_params=pltpu.CompilerParams(vmem_limit_bytes=VMEM_LIMIT),
        name="sample_conv",
    )(*args)


def _gdn_step_kernel(t_ref, s_ref, gz_ref, gn_ref, o_ref, sn_ref):
    scale = HEAD_DIM ** -0.5
    row = lax.broadcasted_iota(jnp.int32, (SUBLANES, HEAD_DIM), 0)
    items = [(b, h) for b in range(STEP_BATCH) for h in range(HEADS)]
    ts = [t_ref[b, h] for b, h in items]
    rs = [_mm(t, s_ref[b, h]) for t, (b, h) in zip(ts, items)]
    egs, v_news = [], []
    for t, r, (b, h) in zip(ts, rs, items):
        k, q, v, beta, g = (t[i:i + 1, :] for i in range(5))
        eg = jnp.exp(g)
        v_new = beta * (v - eg * r[0:1, :])
        o = scale * (eg * r[1:2, :] + jnp.sum(q * k, axis=-1, keepdims=True) * v_new)
        on = (o * lax.rsqrt(jnp.mean(o * o, axis=-1, keepdims=True) + NORM_EPS)) * gn_ref[...]
        o_ref[b, h:h + 1, :] = on * _silu(gz_ref[b, h:h + 1, :])
        egs.append(eg)
        v_news.append(v_new)
    outers = [_mm_tn(jnp.where(row == 0, t, 0.0),
                     jnp.where(row == 0, jnp.broadcast_to(v_new, (SUBLANES, HEAD_DIM)), 0.0))
              for t, v_new in zip(ts, v_news)]
    for eg, outer, (b, h) in zip(egs, outers, items):
        sn_ref[b, h] = s_ref[b, h] * eg + outer


def _gdn_step(t8, state, gz3, gdn_norm_g):
    nb = state.shape[0]
    sb = STEP_BATCH
    return pl.pallas_call(
        _gdn_step_kernel,
        grid=(nb // sb,),
        in_specs=[pl.BlockSpec((sb, HEADS, SUBLANES, HEAD_DIM), lambda b: (b, 0, 0, 0)),
                  pl.BlockSpec((sb, HEADS, HEAD_DIM, HEAD_DIM), lambda b: (b, 0, 0, 0)),
                  pl.BlockSpec((sb, HEADS, HEAD_DIM), lambda b: (b, 0, 0)),
                  pl.BlockSpec((1, HEAD_DIM), lambda b: (0, 0))],
        out_specs=[pl.BlockSpec((sb, HEADS, HEAD_DIM), lambda b: (b, 0, 0)),
                   pl.BlockSpec((sb, HEADS, HEAD_DIM, HEAD_DIM), lambda b: (b, 0, 0, 0))],
        out_shape=[jax.ShapeDtypeStruct((nb, HEADS, HEAD_DIM), F32),
                   jax.ShapeDtypeStruct((nb, HEADS, HEAD_DIM, HEAD_DIM), F32)],
        compiler_params=_params(1),
        name="gdn_step",
    )(t8, state, gz3, gdn_norm_g)


def _class_reduce(x, op):
    shift = HEADS
    while shift < LANES:
        x = op(x, pltpu.roll(x, shift, axis=1))
        shift *= 2
    return x


def _lane_to_col(v, eye8):
    return jnp.sum(jnp.where(eye8, jnp.broadcast_to(v, (SUBLANES, LANES)), 0.0), axis=1, keepdims=True)


def _dec_sig(r, a):
    return (DEC_PAGES - 1 - r) * HEADS + a


def _dec_group_copies(pt_ref, hbm, bufs, sems, seq, g, slot):
    (ck_hbm, cv_hbm, clf_hbm), (kbuf, vbuf, lfbuf) = hbm, bufs
    n_pages = pt_ref.shape[1]
    out = []
    for r in range(DEC_PAGES):
        page = pt_ref[seq, n_pages - 1 - (g * DEC_PAGES + r)]
        row0 = pl.multiple_of(page * PAGE_ROWS, PAGE_ROWS)
        out.append(pltpu.make_async_copy(ck_hbm.at[pl.ds(row0, PAGE_ROWS), :], kbuf.at[slot, r], sems.at[0, slot]))
        out.append(pltpu.make_async_copy(cv_hbm.at[pl.ds(row0, PAGE_ROWS), :], vbuf.at[slot, r], sems.at[1, slot]))
        out.append(pltpu.make_async_copy(clf_hbm.at[page], lfbuf.at[slot, pl.ds(_dec_sig(r, 0), HEADS), :],
                                         sems.at[2, slot]))
    return out


def _dec_start(b, pt_ref, hbm, bufs, sems):
    @pl.when(b == 0)
    def _():
        for c in _dec_group_copies(pt_ref, hbm, bufs, sems, 0, 0, 0):
            c.start()

    for c in _dec_group_copies(pt_ref, hbm, bufs, sems, b, 1, 1):
        c.start()


def _dec_finish(b, nb, pt_ref, q8_ref, kn_ref, vn_ref, gate_ref, lfn_ref, hbm, o_ref, bufs, sems, sd_ref,
                before_last_wait=None, side_stages=None):
    kbuf, vbuf, lfbuf = bufs
    side = {"live": side_stages is not None}

    def advance_side(r):
        if side["live"] and r % DEC_SIDE_EVERY == DEC_SIDE_EVERY - 1:
            if next(side_stages, None) is _GDN_TAIL:
                side["live"] = False

    if side["live"]:
        next(side_stages)
    n_pages = pt_ref.shape[1]
    gp = DEC_PAGES
    ng = n_pages // gp
    nrow = gp * HEADS
    sig_of = _dec_sig
    group_copies = functools.partial(_dec_group_copies, pt_ref, hbm, bufs, sems)

    ri = lax.broadcasted_iota(jnp.int32, (LANES, LANES), 0)
    ci = lax.broadcasted_iota(jnp.int32, (LANES, LANES), 1)
    eye = ri == ci
    same_head = (ri & (HEADS - 1)) == (ci & (HEADS - 1))
    eye8 = (lax.broadcasted_iota(jnp.int32, (SUBLANES, LANES), 0)
            == lax.broadcasted_iota(jnp.int32, (SUBLANES, LANES), 1))
    si = lax.broadcasted_iota(jnp.int32, (nrow, nrow), 0)
    sj = lax.broadcasted_iota(jnp.int32, (nrow, nrow), 1)
    ones = jnp.ones((LANES, LANES), BF16)
    qrep = jnp.concatenate([q8_ref[0]] * (LANES // Q_ROWS), axis=0)

    m_run = jnp.full((1, LANES), NEG_BIG, F32)
    l_run = jnp.zeros((1, LANES), F32)
    acc = jnp.zeros((SUBLANES, HEAD_DIM), F32)
    carry = lfn_ref[0]
    for g in range(ng):
        slot = g % 2
        if 0 < g < ng - 1:
            for c in group_copies(b, g + 1, 1 - slot):
                c.start()
        elif g == ng - 1:
            @pl.when(b + 1 < nb)
            def _():
                for c in group_copies(b + 1, 0, 1 - slot):
                    c.start()
            if before_last_wait is not None:
                before_last_wait()
        for c in group_copies(b, g, slot):
            c.wait()

        lf = lfbuf[slot]
        within = _mm_exact_lhs(lf, _ones_mask(same_head & (ri > ci)))
        tot = _mm_exact_lhs(lf, _ones_mask(same_head))
        later = _mm_exact_rhs(_ones_mask(sj > si), tot)
        bias = (within + later + carry) * LOG2E
        carry = carry + jnp.sum(tot, axis=0, keepdims=True)

        for r in range(gp):
            s2 = _mm_nt(kbuf[slot, r], qrep)
            for a in range(HEADS):
                blk = s2[a * LANES:(a + 1) * LANES, :]
                sig = sig_of(r, a)
                sd_ref[sig:sig + 1, :] = jnp.sum(jnp.where(eye, blk, 0.0), axis=0, keepdims=True)
            advance_side(r)
        for sub in range(gp // DEC_SUB):
            pages = range(sub * DEC_SUB, (sub + 1) * DEC_SUB)
            lo = sig_of(pages[-1], 0)
            rows = slice(lo, lo + DEC_SUB * HEADS)
            tt = sd_ref[rows, :] + bias[rows, :]
            mx = jnp.max(tt, axis=0, keepdims=True)
            mx = _class_reduce(jnp.broadcast_to(mx, (SUBLANES, LANES)), jnp.maximum)[0:1, :]
            m_new = jnp.maximum(m_run, mx)
            pf = jnp.exp2(tt - m_new).astype(BF16).astype(F32)
            alpha = jnp.exp2(m_run - m_new)
            l_run = alpha * l_run + jnp.sum(pf, axis=0, keepdims=True)
            m_run = m_new
            acc = acc * _lane_to_col(alpha, eye8)
            for r in pages:
                parts = []
                for a in range(HEADS):
                    sig = sig_of(r, a) - lo
                    prow = jnp.broadcast_to(pf[sig:sig + 1, :], (LANES, LANES))
                    parts.append(jnp.where(eye, prow, 0.0).astype(BF16))
                pcol = jnp.dot(jnp.concatenate(parts, axis=0), ones, preferred_element_type=F32)
                pv = pcol * vbuf[slot, r]
                acc = acc + jnp.sum(pv.reshape(PAGE_ROWS // SUBLANES, SUBLANES, LANES), axis=0)
                advance_side(r)
    if side_stages is not None:
        _run(side_stages)

    acc4 = acc[0:HEADS, :] + acc[HEADS:2 * HEADS, :]
    mcol = _lane_to_col(m_run, eye8)[0:HEADS, :]
    lsum = _class_reduce(jnp.broadcast_to(l_run, (SUBLANES, LANES)), jnp.add)[0:1, :]
    lcol = _lane_to_col(lsum, eye8)[0:HEADS, :]
    q4 = q8_ref[0][0:HEADS, :].astype(F32)
    s_new = jnp.sum(q4 * kn_ref[0], axis=-1, keepdims=True)
    m_f = jnp.maximum(mcol, s_new)
    a1 = jnp.exp2(mcol - m_f)
    a2 = jnp.exp2(s_new - m_f)
    o = (acc4 * a1 + a2 * vn_ref[0]) / (lcol * a1 + a2)
    o_ref[0] = o * gate_ref[0].astype(F32)


def _mixer_kernel(pt_ref, ab_ref, qi_ref, kj_ref,
                  q8_ref, kn_ref, vn_ref, gate_ref, lfn_ref, ck_hbm, cv_hbm, clf_hbm,
                  aq_ref, ak_ref, av_ref, ack_ref, acq_ref, agate_ref,
                  gqkv_ref, ggz_ref, gcol_ref, grow_ref, gcw_ref, ggn_ref,
                  o_ref, ao_ref, go_ref, gssm_ref, gconv_ref,
                  kbuf, vbuf, lfbuf, sems, sd_ref, m_ref, l_ref, acc_ref, cqb_ref,
                  gs_ref, gx_ref, gq_ref, gk_ref, gv_ref, *, n_att, n_gdn, gdn_blocks):
    del ab_ref
    step = pl.program_id(0)
    n_seq = pt_ref.shape[0]
    hbm, bufs = (ck_hbm, cv_hbm, clf_hbm), (kbuf, vbuf, lfbuf)

    def gdn_stages():
        return _gdn_block_stages(lax.rem(step, gdn_blocks), gdn_blocks, gqkv_ref, ggz_ref, gcol_ref, grow_ref,
                                 gcw_ref, ggn_ref, go_ref, gssm_ref, gconv_ref, gs_ref, gx_ref, gq_ref, gk_ref, gv_ref)

    @pl.when(step < n_seq)
    def _():
        _dec_start(step, pt_ref, hbm, bufs, sems)

    @pl.when(step < n_att)
    def _():
        _att_step(qi_ref[step], kj_ref[step], aq_ref, ak_ref, av_ref, ack_ref, acq_ref, agate_ref, ao_ref,
                  m_ref, l_ref, acc_ref, cqb_ref)

    dec_args = (step, n_seq, pt_ref, q8_ref, kn_ref, vn_ref, gate_ref, lfn_ref, hbm, o_ref, bufs, sems, sd_ref)
    if n_gdn == n_seq:
        @pl.when(step < n_seq)
        def _():
            _dec_finish(*dec_args, side_stages=gdn_stages())
    else:
        @pl.when(step < n_seq)
        def _():
            _dec_finish(*dec_args, before_last_wait=lambda: pl.when(step < n_gdn)(lambda: _run(gdn_stages())))

        @pl.when((step >= n_seq) & (step < n_gdn))
        def _():
            _run(gdn_stages())


def _mixers(page_table, q8, k_new, v_new, gate, lf_new, cache_k2, cache_v2, cache_lf3,
            aq, ak, av, row, col, agate, gqkv, ggz, conv_w, gdn_norm_g, batch, seq):
    nb, n_pages = page_table.shape
    gp = DEC_PAGES
    assert n_pages % (2 * gp) == 0, "an even number of page groups per sequence keeps the buffer slots static"
    n = seq // ATT_T
    steps = [(bb, i, j) for bb in range(batch) for i in range(n) for j in range(i + 1)]
    n_att = len(steps)
    gdn_blocks = seq // GDN_TB
    n_gdn = batch * gdn_blocks
    n_steps = max(nb, n_att, n_gdn)
    steps = steps + [steps[-1]] * (n_steps - n_att)
    ab_tab, qi_tab, kj_tab = (jnp.asarray([s[k] for s in steps], jnp.int32) for k in range(3))

    per_b = lambda s, *_: (jnp.minimum(s, nb - 1), 0, 0)
    qmap = lambda s, pt, ab, qi, kj: (ab[s] * n + qi[s], 0)
    kmap = lambda s, pt, ab, qi, kj: (ab[s] * n + kj[s], 0)
    gblk = lambda s: jnp.minimum(s, n_gdn - 1)
    grow = lambda s, *_: (gblk(s), 0)
    gseq = lambda s, *_: (gblk(s) // gdn_blocks, 0, 0)
    const2 = lambda s, *_: (0, 0)
    hbm = pl.BlockSpec(memory_space=pl.ANY)
    grid_spec = pltpu.PrefetchScalarGridSpec(
        num_scalar_prefetch=4,
        grid=(n_steps,),
        in_specs=[pl.BlockSpec((1, Q_ROWS, HEAD_DIM), per_b),
                  pl.BlockSpec((1, HEADS, HEAD_DIM), per_b),
                  pl.BlockSpec((1, HEADS, HEAD_DIM), per_b),
                  pl.BlockSpec((1, HEADS, HEAD_DIM), per_b),
                  pl.BlockSpec((1, 1, LANES), per_b),
                  hbm, hbm, hbm,
                  pl.BlockSpec((ATT_T, WIDTH), qmap),
                  pl.BlockSpec((ATT_T, WIDTH), kmap),
                  pl.BlockSpec((ATT_T, WIDTH), kmap),
                  pl.BlockSpec((1, 2 * SUBLANES, ATT_T), lambda s, pt, ab, qi, kj: (ab[s], 0, kj[s])),
                  pl.BlockSpec((ATT_T, LANES), qmap),
                  pl.BlockSpec((ATT_T, WIDTH), qmap),
                  pl.BlockSpec((GDN_TB, CONV_DIM), grow),
                  pl.BlockSpec((GDN_TB, WIDTH), grow),
                  pl.BlockSpec((GDN_TB, LANES), grow),
                  pl.BlockSpec((1, 2 * SUBLANES, GDN_TB),
                               lambda s, *_: (gblk(s) // gdn_blocks, 0, gblk(s) % gdn_blocks)),
                  pl.BlockSpec((CONV_W, CONV_DIM), const2),
                  pl.BlockSpec((1, HEAD_DIM), const2)],
        out_specs=[pl.BlockSpec((1, HEADS, HEAD_DIM), per_b),
                   pl.BlockSpec((ATT_T, WIDTH), qmap),
                   pl.BlockSpec((GDN_TB, WIDTH), grow),
                   pl.BlockSpec((1, HEADS, HEAD_DIM, HEAD_DIM), lambda s, *_: (gblk(s) // gdn_blocks, 0, 0, 0)),
                   pl.BlockSpec((1, CONV_W - 1, CONV_DIM), gseq)],
        scratch_shapes=[pltpu.VMEM((2, gp, PAGE_ROWS, HEAD_DIM), F32),
                        pltpu.VMEM((2, gp, PAGE_ROWS, HEAD_DIM), F32),
                        pltpu.VMEM((2, gp * HEADS, LANES), F32),
                        pltpu.SemaphoreType.DMA((3, 2)),
                        pltpu.VMEM((gp * HEADS, LANES), F32)]
                       + [pltpu.VMEM((HEADS, ATT_T, LANES), F32)] * 4
                       + [pltpu.VMEM((HEADS, HEAD_DIM, HEAD_DIM), F32),
                          pltpu.VMEM((GDN_TB + SUBLANES, CONV_DIM), F32)]
                       + [pltpu.VMEM((GDN_TB, WIDTH), F32)] * 3)
    return pl.pallas_call(
        functools.partial(_mixer_kernel, n_att=n_att, n_gdn=n_gdn, gdn_blocks=gdn_blocks),
        grid_spec=grid_spec,
        out_shape=[jax.ShapeDtypeStruct((nb, HEADS, HEAD_DIM), F32),
                   jax.ShapeDtypeStruct((batch * seq, WIDTH), BF16),
                   jax.ShapeDtypeStruct((batch * seq, WIDTH), BF16),
                   jax.ShapeDtypeStruct((batch, HEADS, HEAD_DIM, HEAD_DIM), F32),
                   jax.ShapeDtypeStruct((batch, CONV_W - 1, CONV_DIM), F32)],
        compiler_params=pltpu.CompilerParams(dimension_semantics=("arbitrary",),
                                             vmem_limit_bytes=DEC_VMEM_LIMIT),
        name="mixers",
    )(page_table, ab_tab, qi_tab, kj_tab, q8, k_new, v_new, gate, lf_new, cache_k2, cache_v2, cache_lf3,
      aq, ak, av, row, col, agate, gqkv, ggz, col, row, conv_w, gdn_norm_g)


def _layer(x_prompt, x_sample, cache_k, cache_v, cache_logf, state_ssm, state_conv, page_table,
           c_prompt, c_sample, norm_g, w_ada, b_ada, w_in, b_f, conv_w, a_log, dt_bias,
           gdn_norm_g, q_norm_g, k_norm_g, w_out):
    batch, seq, _ = x_prompt.shape
    nb = x_sample.shape[0]
    w = WIDTH
    o_small = 4 * w
    o_fox = o_small + 2 * HEADS
    w_t = w_in.T
    w_big = jnp.concatenate([w_t[0:o_small], w_t[o_fox:o_fox + 4 * w]], axis=0).astype(BF16)
    w_small = jnp.concatenate([w_t[o_small:o_fox], w_t[o_fox + 4 * w:],
                               jnp.zeros((LANES - 3 * HEADS, D_MODEL), F32)], axis=0).astype(BF16)
    w_out_b = w_out.astype(BF16)
    par = jnp.zeros((SUBLANES, LANES), F32)
    par = par.at[0, LANE_G:LANE_G + HEADS].set(a_log)
    par = par.at[1, LANE_G:LANE_G + HEADS].set(dt_bias)
    par = par.at[2, LANE_F:LANE_F + HEADS].set(b_f)
    ng2, qg2, kg2, gg2 = (a.reshape(1, -1) for a in (norm_g, q_norm_g, k_norm_g, gdn_norm_g))

    cp8 = jnp.pad(c_prompt, ((0, SUBLANES - batch), (0, 0)))
    mod_p, mod_s = _ada_mod(cp8, c_sample, w_ada, b_ada.reshape(1, -1))
    mod_p4 = mod_p[:, 0:batch].reshape(3, batch, 1, D_MODEL)
    mod_s4 = mod_s.reshape(3, 1, nb, D_MODEL)

    xp = x_prompt.reshape(batch * seq, D_MODEL)
    qkv, gz, small, fq, fk, fkb, fv, fvb, fzg = _inproj(xp, mod_p4, False, PROMPT_TM, ng2, w_big, w_small, qg2, kg2)
    col, row = _small_prep(small, par, batch, seq)
    k_p = fk.reshape(batch, seq, HEADS, HEAD_DIM)
    v_p = fv.reshape(batch, seq, HEADS, HEAD_DIM)
    lf_p = col[:, LANE_LF:LANE_LF + HEADS].reshape(batch, seq, HEADS)

    xs = x_sample.reshape(nb, D_MODEL)
    qkv_s, gz_s, small_s, fq_s, fk_s, _, fv_s, _, fzg_s = _inproj(xs, mod_s4, True, nb, ng2, w_big, w_small, qg2, kg2)
    qn_s, kn_s, vn_s, sc_s = _sample_mid(qkv_s, state_conv[:, 0], state_conv[:, 1], state_conv[:, 2],
                                         conv_w, small_s, par)
    h3 = lambda a: a.reshape(nb, HEADS, HEAD_DIM)
    bcast = lambda a: jnp.broadcast_to(a[:, :, None], (nb, HEADS, HEAD_DIM))
    t8 = jnp.stack([h3(kn_s), h3(qn_s), h3(vn_s), bcast(sc_s[:, LANE_B:LANE_B + HEADS]),
                    bcast(sc_s[:, LANE_G:LANE_G + HEADS])]
                   + [jnp.zeros((nb, HEADS, HEAD_DIM), F32)] * (SUBLANES - 5), axis=2)
    o_a_s, ssm_s = _gdn_step(t8, state_ssm, h3(gz_s), gg2)
    lf_s = sc_s[:, LANE_F:LANE_F + HEADS]
    q8 = jnp.tile(h3(fq_s), (1, Q_ROWS // HEADS, 1))
    lf_new = jnp.tile(lf_s, (1, LANES // HEADS)).reshape(nb, 1, LANES)
    n_pool = cache_k.shape[0]
    o_b_s, o_b, o_a, ssm_p, conv_p = _mixers(
        page_table, q8, h3(fk_s), h3(fv_s), h3(fzg_s), lf_new,
        cache_k.reshape(n_pool * PAGE_ROWS, HEAD_DIM), cache_v.reshape(n_pool * PAGE_ROWS, HEAD_DIM),
        cache_logf.reshape(n_pool, HEADS, LANES),
        fq, fkb, fvb, row, col, fzg, qkv, gz, conv_w, gg2, batch, seq)
    y_p = _outproj(xp, o_a, o_b, mod_p4, False, OUT_TM, w_out_b).reshape(batch, seq, D_MODEL)
    y_s = _outproj(xs, o_a_s.reshape(nb, w), o_b_s.reshape(nb, w), mod_s4, True, nb, w_out_b).reshape(nb, 1, D_MODEL)
    conv_s = jnp.stack([state_conv[:, 1], state_conv[:, 2], qkv_s], axis=1)
    return (y_p, y_s, k_p, v_p, lf_p, ssm_p, conv_p,
            h3(fk_s).reshape(nb, 1, HEADS, HEAD_DIM), h3(fv_s).reshape(nb, 1, HEADS, HEAD_DIM),
            lf_s.reshape(nb, 1, HEADS), ssm_s, conv_s)


def kernel(x_prompt, x_sample, cache_k, cache_v, cache_logf, state_ssm, state_conv, page_table, c_prompt, c_sample, norm_g, w_ada, b_ada, w_in, b_f, conv_w, a_log, dt_bias, gdn_norm_g, q_norm_g, k_norm_g, w_out):
    assert w_ada.shape[0] == 1, "single layer"
    outs = _layer(x_prompt, x_sample, cache_k[0], cache_v[0], cache_logf[0], state_ssm[0], state_conv[0],
                  page_table, c_prompt, c_sample, norm_g[0], w_ada[0], b_ada[0], w_in[0], b_f[0], conv_w[0],
                  a_log[0], dt_bias[0], gdn_norm_g[0], q_norm_g[0], k_norm_g[0], w_out[0])
    y_p, y_s = outs[0], outs[1]
    return (y_p, y_s) + tuple(o[None] for o in outs[2:])
```
